```python
import math
import jax
import jax.numpy as jnp
from jax import lax
import numpy as np

D_MODEL = 1024
BATCH = 1
SEQ = 16384
DEPTH = 4

GRID_W = 64
CTX_LEN = 256
HEAD_DIM = 64
ROPE_BASE = 10000.0
NORM_EPS = 1e-6
NEG_INF = -1e30
Q_BLOCK = 128
DIFF_HEADS = 4
DIFF_WIDTH = DIFF_HEADS * 2 * HEAD_DIM
NA_HEADS = 8
NA_WIDTH = NA_HEADS * HEAD_DIM
NA_MAX_ROWS = 8
NA_COLS = 16
AB_Q_COLS = DIFF_WIDTH + NA_WIDTH
AB_IN = 3 * DIFF_WIDTH + 3 * NA_WIDTH
AB_OUT = DIFF_WIDTH + NA_WIDTH
GQA_HEADS = 16
GQA_KV_HEADS = 4
GQA_GROUP = GQA_HEADS // GQA_KV_HEADS
SWA_WINDOW = 128
C_Q_COLS = GQA_HEADS * HEAD_DIM
C_KV_COLS = GQA_KV_HEADS * HEAD_DIM
C_IN = C_Q_COLS + 2 * C_KV_COLS
FFN_HIDDEN = (8 * D_MODEL + 3 * 256 - 1) // (3 * 256) * 256
N_EVEN = (DEPTH + 1) // 2
N_ODD = DEPTH // 2

kernel_name = "hybrid_diffattn_natten_swa_prefix_trunk"


def rmsnorm(x, g):
    xf = x.astype(jnp.float32)
    y = xf * lax.rsqrt(jnp.mean(xf * xf, axis=-1, keepdims=True) + NORM_EPS)
    return (y * g.astype(jnp.float32)).astype(x.dtype)


def modulate(h, shift, scale):
    return h * (1 + scale) + shift


def swiglu(h, w1, w3, w2):
    return (jax.nn.silu(h @ w1) * (h @ w3)) @ w2


def axial_rope_tables(n, dtype):
    t = jnp.arange(n)
    row = (t // GRID_W).astype(jnp.float32)
    col = (t % GRID_W).astype(jnp.float32)
    quarter = HEAD_DIM // 4
    inv = ROPE_BASE ** (-jnp.arange(quarter, dtype=jnp.float32) / quarter)
    ar = row[:, None] * inv
    ac = col[:, None] * inv
    ang = jnp.concatenate([ar, ar, ac, ac], axis=-1)
    return jnp.cos(ang).astype(dtype), jnp.sin(ang).astype(dtype)


def apply_rope(x, cos, sin):
    xs = x.reshape(x.shape[:-1] + (2, 2, HEAD_DIM // 4))
    rot = jnp.stack([-xs[..., 1, :], xs[..., 0, :]], axis=-2).reshape(x.shape)
    return x * cos[None, :, None, :] + rot * sin[None, :, None, :]


def heads(t, n_heads):
    return t.reshape(t.shape[:2] + (n_heads, -1))


def block_sweep(fn, q):
    B, S = q.shape[:2]
    nb = S // Q_BLOCK
    qb = jnp.moveaxis(q.reshape((B, nb, Q_BLOCK) + q.shape[2:]), 1, 0)
    starts = jnp.arange(nb) * Q_BLOCK
    out = lax.map(lambda a: fn(a[0], a[1]), (qb, starts))
    return jnp.moveaxis(out, 0, 1).reshape((B, S) + out.shape[3:])


def plain_attention(q, k, v):
    s = jnp.einsum('bqhd,bnhd->bhqn', q, k).astype(jnp.float32) * (HEAD_DIM ** -0.5)
    p = jax.nn.softmax(s, axis=-1).astype(v.dtype)
    return jnp.einsum('bhqn,bnhd->bqhd', p, v)


def diff_attend(q, k, v, lam_val):
    B, Q = q.shape[:2]
    s = jnp.einsum('bqhd,bnhd->bhqn', q, k).astype(jnp.float32) * (HEAD_DIM ** -0.5)
    p = jax.nn.softmax(s, axis=-1).reshape(B, DIFF_HEADS, 2, Q, -1)
    p = p[:, :, 0] - lam_val * p[:, :, 1]
    return jnp.einsum('bhqn,bnhd->bqhd', p.astype(v.dtype), v)


def diff_out(a, sub_g, lam_init):
    return (rmsnorm(a, sub_g) * (1.0 - lam_init)).reshape(a.shape[:2] + (-1,))


def neighbourhood_attention(q, k, v, k_c, v_c, rpb):
    B, S, _ = q.shape
    rows = S // GRID_W
    kr = min(NA_MAX_ROWS, rows)
    kc = NA_COLS
    grid = lambda t: t.reshape(B, rows, GRID_W, NA_HEADS, HEAD_DIM)
    qg, kg, vg = grid(q), grid(k), grid(v)
    kcx, vcx = heads(k_c, NA_HEADS), heads(v_c, NA_HEADS)
    col = jnp.arange(GRID_W)
    col_start = jnp.clip(col - kc // 2, 0, GRID_W - kc)
    col_idx = col_start[:, None] + jnp.arange(kc)
    dc_idx = col_idx - col[:, None] + (NA_COLS - 1)
    rpb_cols = rpb[:, :, dc_idx]
    scale = HEAD_DIM ** -0.5

    def row_fn(args):
        q_r, r = args
        rs = jnp.clip(r - kr // 2, 0, rows - kr)
        k_r = lax.dynamic_slice_in_dim(kg, rs, kr, axis=1)[:, :, col_idx]
        v_r = lax.dynamic_slice_in_dim(vg, rs, kr, axis=1)[:, :, col_idx]
        dr_idx = rs + jnp.arange(kr) - r + (NA_MAX_ROWS - 1)
        bias = jnp.transpose(rpb_cols[:, dr_idx], (0, 2, 1, 3))
        s_nb = jnp.einsum('bchd,bicjhd->bhcij', q_r, k_r).astype(jnp.float32) * scale + bias
        s_ctx = jnp.einsum('bchd,bnhd->bhcn', q_r, kcx).astype(jnp.float32) * scale
        s = jnp.concatenate([s_nb.reshape(B, NA_HEADS, GRID_W, kr * kc), s_ctx], axis=-1)
        p = jax.nn.softmax(s, axis=-1).astype(v.dtype)
        p_nb = p[..., :kr * kc].reshape(B, NA_HEADS, GRID_W, kr, kc)
        p_ctx = p[..., kr * kc:]
        return (jnp.einsum('bhcij,bicjhd->bchd', p_nb, v_r)
                + jnp.einsum('bhcn,bnhd->bchd', p_ctx, vcx))

    out = lax.map(row_fn, (jnp.moveaxis(qg, 1, 0), jnp.arange(rows)))
    return jnp.moveaxis(out, 0, 1).reshape(B, S, NA_WIDTH)


def mixer_ab(h_x, h_c, w_in, w_out, lam, sub_g, rpb, lam_init, cos, sin, with_ctx_out):
    B, S, _ = h_x.shape
    n_c = h_c.shape[1]
    cuts = [DIFF_WIDTH, AB_Q_COLS, AB_Q_COLS + DIFF_WIDTH, AB_Q_COLS + 2 * DIFF_WIDTH,
            AB_Q_COLS + 2 * DIFF_WIDTH + NA_WIDTH]
    aq_x, bq_x, ak_x, av_x, bk_x, bv_x = jnp.split(h_x @ w_in, cuts, axis=-1)
    if with_ctx_out:
        p_c = h_c @ w_in
        aq_c, bq_c, kv_c = p_c[..., :DIFF_WIDTH], p_c[..., DIFF_WIDTH:AB_Q_COLS], p_c[..., AB_Q_COLS:]
    else:
        kv_c = h_c @ w_in[:, AB_Q_COLS:]
    ak_c, av_c, bk_c, bv_c = jnp.split(kv_c, [DIFF_WIDTH, 2 * DIFF_WIDTH, 2 * DIFF_WIDTH + NA_WIDTH], axis=-1)

    lf = lam.astype(jnp.float32)
    lam_val = jnp.exp(jnp.sum(lf[0] * lf[1])) - jnp.exp(jnp.sum(lf[2] * lf[3])) + lam_init
    aq_xh = apply_rope(heads(aq_x, 2 * DIFF_HEADS), cos, sin)
    ak_xh = apply_rope(heads(ak_x, 2 * DIFF_HEADS), cos, sin)
    ak_ch, av_ch = heads(ak_c, 2 * DIFF_HEADS), heads(av_c, DIFF_HEADS)
    k_all = jnp.concatenate([ak_ch, ak_xh], axis=1)
    v_all = jnp.concatenate([av_ch, heads(av_x, DIFF_HEADS)], axis=1)
    a_x = block_sweep(lambda qb, start: diff_attend(qb, k_all, v_all, lam_val), aq_xh)
    a_x = diff_out(a_x, sub_g, lam_init)

    b_x = neighbourhood_attention(bq_x, bk_x, bv_x, bk_c, bv_c, rpb)
    out_x = jnp.concatenate([a_x, b_x], axis=-1) @ w_out

    out_c = None
    if with_ctx_out:
        a_c = diff_out(diff_attend(heads(aq_c, 2 * DIFF_HEADS), ak_ch, av_ch, lam_val), sub_g, lam_init)
        b_c = plain_attention(heads(bq_c, NA_HEADS), heads(bk_c, NA_HEADS),
                              heads(bv_c, NA_HEADS)).reshape(B, n_c, NA_WIDTH)
        out_c = jnp.concatenate([a_c, b_c], axis=-1) @ w_out
    return out_x, out_c


def gqa_sink_attend(q, k, v, mask, sinks):
    B, Q = q.shape[:2]
    qg = q.reshape(B, Q, GQA_KV_HEADS, GQA_GROUP, HEAD_DIM)
    s = jnp.einsum('bqkgd,bnkd->bkgqn', qg, k).astype(jnp.float32) * (HEAD_DIM ** -0.5)
    if mask is not None:
        s = jnp.where(mask, s, NEG_INF)
    sink = jnp.broadcast_to(sinks.astype(jnp.float32).reshape(GQA_KV_HEADS, GQA_GROUP, 1, 1),
                            (B, GQA_KV_HEADS, GQA_GROUP, Q, 1))
    p = jax.nn.softmax(jnp.concatenate([s, sink], axis=-1), axis=-1)[..., :-1]
    o = jnp.einsum('bkgqn,bnkd->bqkgd', p.astype(v.dtype), v)
    return o.reshape(B, Q, GQA_HEADS, HEAD_DIM)


def mixer_c(h_x, h_c, w_in, w_out, sinks, cos, sin, with_ctx_out):
    B, S, _ = h_x.shape
    q_x, k_x, v_x = jnp.split(h_x @ w_in, [C_Q_COLS, C_Q_COLS + C_KV_COLS], axis=-1)
    q_x = apply_rope(heads(q_x, GQA_HEADS), cos, sin)
    k_x = apply_rope(heads(k_x, GQA_KV_HEADS), cos, sin)
    v_x = heads(v_x, GQA_KV_HEADS)
    if with_ctx_out:
        p_c = h_c @ w_in
        q_c, kv_c = p_c[..., :C_Q_COLS], p_c[..., C_Q_COLS:]
    else:
        kv_c = h_c @ w_in[:, C_Q_COLS:]
    k_c, v_c = jnp.split(kv_c, 2, axis=-1)
    k_c, v_c = heads(k_c, GQA_KV_HEADS), heads(v_c, GQA_KV_HEADS)
    n_c = k_c.shape[1]

    pad = ((0, 0), (SWA_WINDOW, SWA_WINDOW), (0, 0), (0, 0))
    k_pad, v_pad = jnp.pad(k_x, pad), jnp.pad(v_x, pad)
    span = Q_BLOCK + 2 * SWA_WINDOW
    ctx_mask = jnp.ones((Q_BLOCK, n_c), dtype=bool)

    def blk(q_b, start):
        k_b = lax.dynamic_slice_in_dim(k_pad, start, span, axis=1)
        v_b = lax.dynamic_slice_in_dim(v_pad, start, span, axis=1)
        q_pos = start + jnp.arange(Q_BLOCK)
        k_pos = start - SWA_WINDOW + jnp.arange(span)
        band = ((k_pos >= 0) & (k_pos < S))[None, :] & (jnp.abs(q_pos[:, None] - k_pos[None, :]) <= SWA_WINDOW)
        return gqa_sink_attend(q_b, jnp.concatenate([k_c, k_b], axis=1),
                               jnp.concatenate([v_c, v_b], axis=1),
                               jnp.concatenate([ctx_mask, band], axis=-1), sinks)

    out_x = block_sweep(blk, q_x).reshape(B, S, C_Q_COLS) @ w_out
    out_c = None
    if with_ctx_out:
        o_c = gqa_sink_attend(heads(q_c, GQA_HEADS), k_c, v_c, None, sinks)
        out_c = o_c.reshape(B, n_c, C_Q_COLS) @ w_out
    return out_x, out_c


def setup_inputs(seed: int = 0) -> dict:
    key = jax.random.key(seed)
    ks = jax.random.split(key, 19)
    nrm = lambda k, shape, std: jax.random.normal(k, shape, jnp.float32) * std
    D = D_MODEL
    return {
        'x': nrm(ks[0], (BATCH, SEQ, D), 1.0),
        'c': nrm(ks[1], (BATCH, D), 1.0),
        'ctx': nrm(ks[2], (BATCH, CTX_LEN, D), 1.0),
        'c_ctx': nrm(ks[3], (D,), 1.0),
        'ada_w': nrm(ks[4], (DEPTH, D, 6 * D), 0.5 * D ** -0.5),
        'ada_b': nrm(ks[5], (DEPTH, 6 * D), 0.02),
        'norm_g': 1.0 + nrm(ks[6], (DEPTH, 2, D), 0.01),
        'w_in_ab': nrm(ks[7], (N_EVEN, D, AB_IN), D ** -0.5),
        'w_out_ab': nrm(ks[8], (N_EVEN, AB_OUT, D), AB_OUT ** -0.5),
        'diff_lambda': nrm(ks[9], (N_EVEN, 4, HEAD_DIM), 0.1),
        'diff_sub_g': 1.0 + nrm(ks[10], (N_EVEN, 2 * HEAD_DIM), 0.01),
        'na_rpb': nrm(ks[11], (N_EVEN, NA_HEADS, 2 * NA_MAX_ROWS - 1, 2 * NA_COLS - 1), 0.1),
        'w_in_c': nrm(ks[12], (N_ODD, D, C_IN), D ** -0.5),
        'w_out_c': nrm(ks[13], (N_ODD, C_Q_COLS, D), C_Q_COLS ** -0.5),
        'attn_sinks': nrm(ks[14], (N_ODD, GQA_HEADS), 0.5),
        'ffn_w1': nrm(ks[15], (DEPTH, D, FFN_HIDDEN), D ** -0.5),
        'ffn_w3': nrm(ks[16], (DEPTH, D, FFN_HIDDEN), D ** -0.5),
        'ffn_w2': nrm(ks[17], (DEPTH, FFN_HIDDEN, D), FFN_HIDDEN ** -0.5),
        'final_g': 1.0 + nrm(ks[18], (D,), 0.01),
    }


def reference(x, c, ctx, c_ctx, ada_w, ada_b, norm_g, w_in_ab, w_out_ab, diff_lambda, diff_sub_g,
              na_rpb, w_in_c, w_out_c, attn_sinks, ffn_w1, ffn_w3, ffn_w2, final_g):
    B, S, _ = x.shape
    cos, sin = axial_rope_tables(S, x.dtype)
    for layer in range(DEPTH):
        ctx_out = layer < DEPTH - 1
        mod_x = jnp.split((jax.nn.silu(c) @ ada_w[layer] + ada_b[layer])[:, None, :], 6, axis=-1)
        mod_c = jnp.split(jax.nn.silu(c_ctx) @ ada_w[layer] + ada_b[layer], 6, axis=-1)
        h_x = modulate(rmsnorm(x, norm_g[layer, 0]), mod_x[0], mod_x[1])
        h_c = modulate(rmsnorm(ctx, norm_g[layer, 0]), mod_c[0], mod_c[1])
        if layer % 2 == 0:
            e = layer // 2
            lam_init = 0.8 - 0.6 * math.exp(-0.3 * layer)
            o_x, o_c = mixer_ab(h_x, h_c, w_in_ab[e], w_out_ab[e], diff_lambda[e], diff_sub_g[e],
                                na_rpb[e], lam_init, cos, sin, ctx_out)
        else:
            o = layer // 2
            o_x, o_c = mixer_c(h_x, h_c, w_in_c[o], w_out_c[o], attn_sinks[o], cos, sin, ctx_out)
        x = x + mod_x[2] * o_x
        h_x = modulate(rmsnorm(x, norm_g[layer, 1]), mod_x[3], mod_x[4])
        x = x + mod_x[5] * swiglu(h_x, ffn_w1[layer], ffn_w3[layer], ffn_w2[layer])
        if ctx_out:
            ctx = ctx + mod_c[2] * o_c
            h_c = modulate(rmsnorm(ctx, norm_g[layer, 1]), mod_c[3], mod_c[4])
            ctx = ctx + mod_c[5] * swiglu(h_c, ffn_w1[layer], ffn_w3[layer], ffn_w2[layer])
    return rmsnorm(x, final_g)
```

```python
import functools
import math

import jax
import jax.numpy as jnp
import numpy as np
from jax import lax
from jax.experimental import pallas as pl
from jax.experimental.pallas import tpu as pltpu

F32 = jnp.float32
BF16 = jnp.bfloat16

D_MODEL = 1024
GRID_W = 64
HEAD_DIM = 64
ROPE_BASE = 10000.0
NORM_EPS = 1e-6
NEG_INF = -1e30
DIFF_HEADS = 4
DIFF_WIDTH = DIFF_HEADS * 2 * HEAD_DIM
NA_HEADS = 8
NA_WIDTH = NA_HEADS * HEAD_DIM
NA_MAX_ROWS = 8
NA_COLS = 16
GQA_HEADS = 16
GQA_KV_HEADS = 4
GQA_GROUP = GQA_HEADS // GQA_KV_HEADS
SWA_WINDOW = 128
C_Q_COLS = GQA_HEADS * HEAD_DIM
C_KV_COLS = GQA_KV_HEADS * HEAD_DIM
Q_SCALE = HEAD_DIM ** -0.5

LANES = 128
VMEM_LIMIT = 56 * 1024 * 1024

ROW_TILE = 512
DIFF_TQ = 512
DIFF_TK = 512
NA_TILE_ROWS = 4
NA_TQ = NA_TILE_ROWS * GRID_W
SWA_TQ = SWA_WINDOW


def _cparams(*sem):
    return pltpu.CompilerParams(dimension_semantics=sem, vmem_limit_bytes=VMEM_LIMIT)


def _resident(shape):
    nd = len(shape)
    return pl.BlockSpec(shape, lambda *_: (0,) * nd, pipeline_mode=pl.Buffered(1))


def _dot(a, b):
    return jnp.dot(a, b, preferred_element_type=F32)


def _adaln_kernel(cc_ref, w_ref, b_ref, o_ref):
    cc = cc_ref[...]
    h = (cc * jax.nn.sigmoid(cc)).astype(BF16)
    o_ref[0] = _dot(h, w_ref[0].astype(BF16)) + b_ref[0]


def _adaln(cc, ada_w, ada_b):
    depth, d, d6 = ada_w.shape
    nj = d6 // d
    return pl.pallas_call(
        _adaln_kernel,
        grid=(depth, nj),
        in_specs=[
            pl.BlockSpec((8, d), lambda l, j: (0, 0)),
            pl.BlockSpec((1, d, d), lambda l, j: (l, 0, j)),
            pl.BlockSpec((1, 1, d), lambda l, j: (l, 0, j)),
        ],
        out_specs=pl.BlockSpec((1, 8, d), lambda l, j: (l, 0, j)),
        out_shape=jax.ShapeDtypeStruct((depth, 8, d6), F32),
        compiler_params=_cparams("arbitrary", "arbitrary"),
        name="adaln",
    )(cc, ada_w, ada_b.reshape(depth, 1, d6))


def _norm_mod(x, g, shift, scale):
    ms = jnp.mean(x * x, axis=-1, keepdims=True)
    h = x * lax.rsqrt(ms + NORM_EPS) * g
    return h * (1.0 + scale) + shift


def _rope_lanes(t, cos, sin_signed):
    lane = lax.broadcasted_iota(jnp.int32, (t.shape[0], LANES), 1)
    first = (lane % (HEAD_DIM // 2)) < (HEAD_DIM // 4)
    out = []
    for c in range(t.shape[1] // LANES):
        tc = t[:, c * LANES:(c + 1) * LANES]
        partner = jnp.where(first, pltpu.roll(tc, LANES - HEAD_DIM // 4, 1), pltpu.roll(tc, HEAD_DIM // 4, 1))
        out.append(tc * cos + partner * sin_signed)
    return jnp.concatenate(out, axis=1)


def _proj_in_kernel(*refs, n_rope_q, n_q, n_rope_k, n_k, rope):
    if rope:
        x_ref, g_ref, mod_ref, cos_ref, sin_ref, w_ref, qT_ref, k_ref, vT_ref = refs
    else:
        x_ref, g_ref, mod_ref, w_ref, qT_ref, k_ref, vT_ref = refs
    h = _norm_mod(x_ref[...], g_ref[...], mod_ref[0:1, :], mod_ref[1:2, :])
    y = _dot(h.astype(BF16), w_ref[...])
    q = y[:, :n_q]
    k = y[:, n_q:n_q + n_k]
    v = y[:, n_q + n_k:]
    if rope:
        cos, sin = cos_ref[...], sin_ref[...]
        q = jnp.concatenate([_rope_lanes(q[:, :n_rope_q], cos, sin), q[:, n_rope_q:]], axis=1) \
            if n_rope_q < n_q else _rope_lanes(q, cos, sin)
        k = jnp.concatenate([_rope_lanes(k[:, :n_rope_k], cos, sin), k[:, n_rope_k:]], axis=1) \
            if n_rope_k < n_k else _rope_lanes(k, cos, sin)
    qT_ref[...] = (q * Q_SCALE).T.astype(BF16)
    k_ref[...] = k.astype(BF16)
    vT_ref[...] = v.T.astype(BF16)


def _proj_in(x, g, mod, w, tables, *, n_rope_q, n_q, n_rope_k, n_k):
    n, d = x.shape
    cols = w.shape[1]
    n_v = cols - n_q - n_k
    tm = min(ROW_TILE, n)
    rope = tables is not None
    row = lambda i: (i, 0)
    in_specs = [pl.BlockSpec((tm, d), row), _resident((1, d)), _resident(mod.shape)]
    args = [x, g, mod]
    if rope:
        in_specs += [pl.BlockSpec((tm, LANES), row), pl.BlockSpec((tm, LANES), row)]
        args += list(tables)
    in_specs.append(_resident(w.shape))
    args.append(w)
    return pl.pallas_call(
        functools.partial(_proj_in_kernel, n_rope_q=n_rope_q, n_q=n_q, n_rope_k=n_rope_k, n_k=n_k, rope=rope),
        grid=(n // tm,),
        in_specs=in_specs,
        out_specs=[
            pl.BlockSpec((n_q, tm), lambda i: (0, i)),
            pl.BlockSpec((tm, n_k), row),
            pl.BlockSpec((n_v, tm), lambda i: (0, i)),
        ],
        out_shape=[
            jax.ShapeDtypeStruct((n_q, n), BF16),
            jax.ShapeDtypeStruct((n, n_k), BF16),
            jax.ShapeDtypeStruct((n_v, n), BF16),
        ],
        compiler_params=_cparams("arbitrary"),
        name="proj_in",
    )(*args)


def _half_masked(qT_pair, e):
    z = jnp.zeros((HEAD_DIM, qT_pair.shape[1]), qT_pair.dtype)
    if e == 0:
        return jnp.concatenate([qT_pair[:HEAD_DIM], z], axis=0)
    return jnp.concatenate([z, qT_pair[HEAD_DIM:]], axis=0)


def _diff_kernel(*refs, lam_init, has_latent):
    if has_latent:
        qT_ref, k_ref, vT_ref, kc_ref, vcT_ref, lam_ref, subg_ref, o_ref, qm_ref, m_ref, l_ref, acc_ref = refs
        j = pl.program_id(1)
        last = pl.num_programs(1) - 1
    else:
        qT_ref, kc_ref, vcT_ref, lam_ref, subg_ref, o_ref, qm_ref, m_ref, l_ref, acc_ref = refs
    n_maps = 2 * DIFF_HEADS

    def start():
        for h in range(DIFF_HEADS):
            pair = qT_ref[h * LANES:(h + 1) * LANES, :]
            kh = kc_ref[:, h * LANES:(h + 1) * LANES]
            vh = vcT_ref[h * LANES:(h + 1) * LANES, :]
            for e in range(2):
                mp = 2 * h + e
                qm = _half_masked(pair, e)
                qm_ref[mp] = qm
                s = _dot(kh, qm)
                m = jnp.max(s, axis=0, keepdims=True)
                p = jnp.exp(s - m)
                m_ref[mp:mp + 1, :] = m
                l_ref[mp:mp + 1, :] = jnp.sum(p, axis=0, keepdims=True)
                acc_ref[mp] = _dot(vh, p.astype(BF16))

    def update():
        for h in range(DIFF_HEADS):
            kh = k_ref[:, h * LANES:(h + 1) * LANES]
            vh = vT_ref[h * LANES:(h + 1) * LANES, :]
            for e in range(2):
                mp = 2 * h + e
                s = _dot(kh, qm_ref[mp])
                m_old = m_ref[mp:mp + 1, :]
                m_new = jnp.maximum(m_old, jnp.max(s, axis=0, keepdims=True))
                alpha = jnp.exp(m_old - m_new)
                p = jnp.exp(s - m_new)
                m_ref[mp:mp + 1, :] = m_new
                l_ref[mp:mp + 1, :] = alpha * l_ref[mp:mp + 1, :] + jnp.sum(p, axis=0, keepdims=True)
                acc_ref[mp] = alpha * acc_ref[mp] + _dot(vh, p.astype(BF16))

    def finish():
        lam = lam_ref[...]
        lam_val = (jnp.exp(jnp.sum(lam[0:1] * lam[1:2], axis=1, keepdims=True))
                   - jnp.exp(jnp.sum(lam[2:3] * lam[3:4], axis=1, keepdims=True)) + lam_init)
        subg = subg_ref[...]
        for h in range(DIFF_HEADS):
            a0 = acc_ref[2 * h] / l_ref[2 * h:2 * h + 1, :]
            a1 = acc_ref[2 * h + 1] / l_ref[2 * h + 1:2 * h + 2, :]
            a = a0 - lam_val * a1
            ms = jnp.mean(a * a, axis=0, keepdims=True)
            a = a * lax.rsqrt(ms + NORM_EPS) * subg * (1.0 - lam_init)
            o_ref[:, h * LANES:(h + 1) * LANES] = a.T.astype(o_ref.dtype)

    if has_latent:
        pl.when(j == 0)(start)
        update()
        pl.when(j == last)(finish)
    else:
        start()
        finish()
    del n_maps


def _diff_attention(qT, k, vT, kc, vcT, lam, subg, lam_init, *, has_latent):
    n = qT.shape[1]
    nc = kc.shape[0]
    w = DIFF_WIDTH
    small = [_resident((4, HEAD_DIM)), _resident((2 * HEAD_DIM, 1))]
    if has_latent:
        tq, tk = DIFF_TQ, DIFF_TK
        grid = (n // tq, k.shape[0] // tk)
        in_specs = [
            pl.BlockSpec((w, tq), lambda i, j: (0, i)),
            pl.BlockSpec((tk, w), lambda i, j: (j, 0)),
            pl.BlockSpec((w, tk), lambda i, j: (0, j)),
            pl.BlockSpec((nc, w), lambda i, j: (0, 0)),
            pl.BlockSpec((w, nc), lambda i, j: (0, 0)),
        ] + small
        args = (qT, k, vT, kc, vcT, lam, subg)
        out_spec = pl.BlockSpec((tq, w), lambda i, j: (i, 0))
        sem = ("parallel", "arbitrary")
    else:
        tq = n
        grid = (1,)
        in_specs = [
            pl.BlockSpec((w, tq), lambda i: (0, 0)),
            pl.BlockSpec((nc, w), lambda i: (0, 0)),
            pl.BlockSpec((w, nc), lambda i: (0, 0)),
        ] + small
        args = (qT, kc, vcT, lam, subg)
        out_spec = pl.BlockSpec((tq, w), lambda i: (0, 0))
        sem = ("arbitrary",)
    return pl.pallas_call(
        functools.partial(_diff_kernel, lam_init=lam_init, has_latent=has_latent),
        grid=grid,
        in_specs=in_specs,
        out_specs=out_spec,
        out_shape=jax.ShapeDtypeStruct((n, w), BF16),
        scratch_shapes=[
            pltpu.VMEM((2 * DIFF_HEADS, LANES, tq), BF16),
            pltpu.VMEM((2 * DIFF_HEADS, tq), F32),
            pltpu.VMEM((2 * DIFF_HEADS, tq), F32),
            pltpu.VMEM((2 * DIFF_HEADS, LANES, tq), F32),
        ],
        compiler_params=_cparams(*sem),
        name="diff_attn" if has_latent else "diff_attn_ctx",
    )(*args)


def _na_bias_tables(rpb, rows):
    n_tiles = rows // NA_TILE_ROWS
    kk = np.arange(3 * NA_TQ)
    qq = np.arange(NA_TQ)
    k_row_rel, k_col = kk // GRID_W - NA_TILE_ROWS, kk % GRID_W
    q_row_rel, q_col = qq // GRID_W, qq % GRID_W
    cs = np.clip(q_col - NA_COLS // 2, 0, GRID_W - NA_COLS)
    dr_idx, dc_idx, valid = [], [], []
    for tile in (0, 1, n_tiles - 1):
        r0 = tile * NA_TILE_ROWS
        r = r0 + q_row_rel
        rs = np.clip(r - NA_MAX_ROWS // 2, 0, rows - NA_MAX_ROWS)
        kr = r0 + k_row_rel
        ok = ((kr[:, None] >= rs[None, :]) & (kr[:, None] < rs[None, :] + NA_MAX_ROWS)
              & (k_col[:, None] >= cs[None, :]) & (k_col[:, None] < cs[None, :] + NA_COLS))
        dr = np.clip(kr[:, None] - r[None, :] + (NA_MAX_ROWS - 1), 0, 2 * NA_MAX_ROWS - 2)
        dc = np.clip(k_col[:, None] - q_col[None, :] + (NA_COLS - 1), 0, 2 * NA_COLS - 2)
        dr_idx.append(dr), dc_idx.append(dc), valid.append(ok)
    dr_idx, dc_idx, valid = np.stack(dr_idx), np.stack(dc_idx), np.stack(valid)
    vals = rpb.astype(F32)[:, dr_idx, dc_idx]
    return jnp.where(valid[None], vals, NEG_INF).transpose(1, 0, 2, 3)


def _na_kernel(*refs, has_latent):
    if has_latent:
        qT_ref, km_ref, k0_ref, kp_ref, vm_ref, v0_ref, vp_ref, kc_ref, vcT_ref, bias_ref, o_ref = refs
    else:
        qT_ref, kc_ref, vcT_ref, o_ref = refs
    pair = qT_ref[...]
    kc = kc_ref[...]
    if has_latent:
        vT = jnp.concatenate([vm_ref[...], v0_ref[...], vp_ref[...], vcT_ref[...]], axis=1)
    else:
        vT = vcT_ref[...]
    outs = []
    for e in range(2):
        qm = _half_masked(pair, e)
        s = _dot(kc, qm)
        if has_latent:
            s_nb = jnp.concatenate([_dot(km_ref[...], qm), _dot(k0_ref[...], qm), _dot(kp_ref[...], qm)], axis=0)
            s = jnp.concatenate([s_nb + bias_ref[0, e], s], axis=0)
        m = jnp.max(s, axis=0, keepdims=True)
        p = jnp.exp(s - m)
        l = jnp.sum(p, axis=0, keepdims=True)
        outs.append(_dot(vT[e * HEAD_DIM:(e + 1) * HEAD_DIM], p.astype(BF16)) / l)
    o_ref[...] = jnp.concatenate(outs, axis=0).T.astype(o_ref.dtype)


def _na_attention(qT_all, k_all, vT_all, kc_all, vcT_all, bias, *, has_latent):
    n = qT_all.shape[1]
    nc = kc_all.shape[0]
    g0 = DIFF_WIDTH // LANES
    n_pairs = NA_HEADS // 2
    if has_latent:
        tq = NA_TQ
        nt = n // tq
        prev = lambda t: jnp.maximum(t - 1, 0)
        nxt = lambda t: jnp.minimum(t + 1, nt - 1)
        variant = lambda t: jnp.where(t == 0, 0, jnp.where(t == nt - 1, 2, 1))
        in_specs = [
            pl.BlockSpec((LANES, tq), lambda hp, t: (g0 + hp, t)),
            pl.BlockSpec((tq, LANES), lambda hp, t: (prev(t), g0 + hp)),
            pl.BlockSpec((tq, LANES), lambda hp, t: (t, g0 + hp)),
            pl.BlockSpec((tq, LANES), lambda hp, t: (nxt(t), g0 + hp)),
            pl.BlockSpec((LANES, tq), lambda hp, t: (g0 + hp, prev(t))),
            pl.BlockSpec((LANES, tq), lambda hp, t: (g0 + hp, t)),
            pl.BlockSpec((LANES, tq), lambda hp, t: (g0 + hp, nxt(t))),
            pl.BlockSpec((nc, LANES), lambda hp, t: (0, g0 + hp)),
            pl.BlockSpec((LANES, nc), lambda hp, t: (g0 + hp, 0)),
            pl.BlockSpec((1, 2, 3 * tq, tq), lambda hp, t: (variant(t), hp, 0, 0)),
        ]
        args = (qT_all, k_all, k_all, k_all, vT_all, vT_all, vT_all, kc_all, vcT_all, bias)
    else:
        tq = n
        nt = 1
        in_specs = [
            pl.BlockSpec((LANES, tq), lambda hp, t: (g0 + hp, 0)),
            pl.BlockSpec((nc, LANES), lambda hp, t: (0, g0 + hp)),
            pl.BlockSpec((LANES, nc), lambda hp, t: (g0 + hp, 0)),
        ]
        args = (qT_all, kc_all, vcT_all)
    return pl.pallas_call(
        functools.partial(_na_kernel, has_latent=has_latent),
        grid=(n_pairs, nt),
        in_specs=in_specs,
        out_specs=pl.BlockSpec((tq, LANES), lambda hp, t: (t, hp)),
        out_shape=jax.ShapeDtypeStruct((n, NA_WIDTH), BF16),
        compiler_params=_cparams("arbitrary", "arbitrary"),
        name="na_attn" if has_latent else "na_attn_ctx",
    )(*args)


def _swa_kernel(*refs, has_latent):
    if has_latent:
        qT_ref, km_ref, k0_ref, kp_ref, vm_ref, v0_ref, vp_ref, kc_ref, vcT_ref, sink_ref, o_ref = refs
        b = pl.program_id(0)
        nb = pl.num_programs(0)
        tq = qT_ref.shape[1]
        ki = lax.broadcasted_iota(jnp.int32, (tq, tq), 0)
        qi = lax.broadcasted_iota(jnp.int32, (tq, tq), 1)
        band_prev = jnp.concatenate([(ki >= qi) & (b > 0)] * GQA_GROUP, axis=1)
        band_next = jnp.concatenate([(ki <= qi) & (b < nb - 1)] * GQA_GROUP, axis=1)
    else:
        qT_ref, kc_ref, vcT_ref, sink_ref, o_ref = refs
        tq = qT_ref.shape[1]
    zeros = jnp.zeros((HEAD_DIM, tq), BF16)
    outs = []
    for g in range(GQA_KV_HEADS):
        grp, half = g // 2, g % 2
        lanes = slice(grp * LANES, (grp + 1) * LANES)
        rows = slice(g * HEAD_DIM, (g + 1) * HEAD_DIM)
        q_cols = []
        for jh in range(GQA_GROUP):
            hq = g * GQA_GROUP + jh
            qh = qT_ref[hq * HEAD_DIM:(hq + 1) * HEAD_DIM, :]
            q_cols.append(jnp.concatenate([qh, zeros] if half == 0 else [zeros, qh], axis=0))
        qm = jnp.concatenate(q_cols, axis=1)
        s = _dot(kc_ref[:, lanes], qm)
        vT = vcT_ref[rows, :]
        if has_latent:
            s_prev = jnp.where(band_prev, _dot(km_ref[:, lanes], qm), NEG_INF)
            s_next = jnp.where(band_next, _dot(kp_ref[:, lanes], qm), NEG_INF)
            s = jnp.concatenate([s, s_prev, _dot(k0_ref[:, lanes], qm), s_next], axis=0)
            vT = jnp.concatenate([vT, vm_ref[rows, :], v0_ref[rows, :], vp_ref[rows, :]], axis=1)
        sink = sink_ref[g]
        m = jnp.maximum(jnp.max(s, axis=0, keepdims=True), sink)
        p = jnp.exp(s - m)
        l = jnp.sum(p, axis=0, keepdims=True) + jnp.exp(sink - m)
        o = _dot(vT, p.astype(BF16)) / l
        outs += [o[:, jh * tq:(jh + 1) * tq] for jh in range(GQA_GROUP)]
    o_ref[...] = jnp.concatenate(outs, axis=0).T.astype(o_ref.dtype)


def _swa_attention(qT, k, vT, kc, vcT, sinks, *, has_latent):
    n = qT.shape[1]
    nc = kc.shape[0]
    tq = SWA_TQ
    nb = n // tq
    sink_rows = jnp.repeat(sinks.astype(F32).reshape(GQA_KV_HEADS, 1, GQA_GROUP), tq, axis=2)
    ctx_specs = [_resident((nc, C_KV_COLS)), _resident((C_KV_COLS, nc)), _resident(sink_rows.shape)]
    if has_latent:
        prev = lambda b: jnp.maximum(b - 1, 0)
        nxt = lambda b: jnp.minimum(b + 1, nb - 1)
        in_specs = [
            pl.BlockSpec((C_Q_COLS, tq), lambda b: (0, b)),
            pl.BlockSpec((tq, C_KV_COLS), lambda b: (prev(b), 0)),
            pl.BlockSpec((tq, C_KV_COLS), lambda b: (b, 0)),
            pl.BlockSpec((tq, C_KV_COLS), lambda b: (nxt(b), 0)),
            pl.BlockSpec((C_KV_COLS, tq), lambda b: (0, prev(b))),
            pl.BlockSpec((C_KV_COLS, tq), lambda b: (0, b)),
            pl.BlockSpec((C_KV_COLS, tq), lambda b: (0, nxt(b))),
        ] + ctx_specs
        args = (qT, k, k, k, vT, vT, vT, kc, vcT, sink_rows)
    else:
        in_specs = [pl.BlockSpec((C_Q_COLS, tq), lambda b: (0, b))] + ctx_specs
        args = (qT, kc, vcT, sink_rows)
    return pl.pallas_call(
        functools.partial(_swa_kernel, has_latent=has_latent),
        grid=(nb,),
        in_specs=in_specs,
        out_specs=pl.BlockSpec((tq, C_Q_COLS), lambda b: (b, 0)),
        out_shape=jax.ShapeDtypeStruct((n, C_Q_COLS), BF16),
        compiler_params=_cparams("arbitrary"),
        name="swa_attn" if has_latent else "swa_attn_ctx",
    )(*args)


def _out_ffn_kernel(*refs, n_attn, final):
    x_ref = refs[0]
    attn_refs = refs[1:1 + n_attn]
    wout_ref, mod_ref, g_ref, w1_ref, w3_ref, w2_ref = refs[1 + n_attn:7 + n_attn]
    rest = refs[7 + n_attn:]
    if final:
        fg_ref, o_ref = rest
    else:
        (o_ref,) = rest
    proj = None
    r0 = 0
    for a_ref in attn_refs:
        wdt = a_ref.shape[1]
        part = _dot(a_ref[...], wout_ref[r0:r0 + wdt, :])
        proj = part if proj is None else proj + part
        r0 += wdt
    x1 = x_ref[...] + mod_ref[2:3, :] * proj
    h = _norm_mod(x1, g_ref[...], mod_ref[3:4, :], mod_ref[4:5, :]).astype(BF16)
    hidden = w1_ref.shape[1]
    chunk = hidden // 2
    ffn = None
    for c0 in range(0, hidden, chunk):
        a = _dot(h, w1_ref[:, c0:c0 + chunk])
        b = _dot(h, w3_ref[:, c0:c0 + chunk])
        gated = (a * jax.nn.sigmoid(a) * b).astype(BF16)
        part = _dot(gated, w2_ref[c0:c0 + chunk, :])
        ffn = part if ffn is None else ffn + part
    x2 = x1 + mod_ref[5:6, :] * ffn
    if final:
        ms = jnp.mean(x2 * x2, axis=-1, keepdims=True)
        x2 = x2 * lax.rsqrt(ms + NORM_EPS) * fg_ref[...]
    o_ref[...] = x2


def _out_ffn(x, attn, w_out, mod, g, w1, w3, w2, final_g=None):
    n, d = x.shape
    tm = min(ROW_TILE, n)
    row = lambda i: (i, 0)
    final = final_g is not None
    in_specs = [pl.BlockSpec((tm, d), row)]
    in_specs += [pl.BlockSpec((tm, a.shape[1]), row) for a in attn]
    in_specs += [_resident(w_out.shape), _resident(mod.shape), _resident((1, d)),
                 _resident(w1.shape), _resident(w3.shape), _resident(w2.shape)]
    args = [x, *attn, w_out, mod, g, w1, w3, w2]
    if final:
        in_specs.append(_resident((1, d)))
        args.append(final_g)
    return pl.pallas_call(
        functools.partial(_out_ffn_kernel, n_attn=len(attn), final=final),
        grid=(n // tm,),
        in_specs=in_specs,
        out_specs=pl.BlockSpec((tm, d), row),
        out_shape=jax.ShapeDtypeStruct((n, d), F32),
        compiler_params=_cparams("arbitrary"),
        name="out_ffn",
    )(*args)


def _rope_tables(n):
    t = jnp.arange(n)
    row = (t // GRID_W).astype(F32)
    col = (t % GRID_W).astype(F32)
    quarter = HEAD_DIM // 4
    inv = ROPE_BASE ** (-jnp.arange(quarter, dtype=F32) / quarter)
    ar = row[:, None] * inv
    ac = col[:, None] * inv
    ang = jnp.concatenate([ar, ar, ac, ac], axis=-1)
    sign = jnp.tile(jnp.concatenate([-jnp.ones(quarter, F32), jnp.ones(quarter, F32)]), 2)
    cos = jnp.tile(jnp.cos(ang), (1, LANES // HEAD_DIM))
    sin = jnp.tile(jnp.sin(ang) * sign, (1, LANES // HEAD_DIM))
    return cos, sin


def _trunk(x, ctx, mods, norm_g, w_in_ab, w_out_ab, diff_lambda, diff_sub_g, na_rpb, w_in_c, w_out_c,
           attn_sinks, ffn_w1, ffn_w3, ffn_w2, final_g):
    depth = mods.shape[0]
    s_len = x.shape[0]
    tables = _rope_tables(s_len)
    rows = s_len // GRID_W
    for layer in range(depth):
        ctx_out = layer < depth - 1
        mod_x, mod_c = mods[layer, 0], mods[layer, 1]
        g0 = norm_g[layer, 0].reshape(1, -1)
        g1 = norm_g[layer, 1].reshape(1, -1)
        if layer % 2 == 0:
            e = layer // 2
            lam_init = 0.8 - 0.6 * math.exp(-0.3 * layer)
            w = w_in_ab[e]
            q_cols = DIFF_WIDTH + NA_WIDTH
            a_k, a_v, b_k, b_v = (w[:, q_cols + i * DIFF_WIDTH:q_cols + (i + 1) * DIFF_WIDTH] for i in range(4))
            w = jnp.concatenate([w[:, :q_cols], a_k, b_k, a_v, b_v], axis=1).astype(BF16)
            dims = dict(n_rope_q=DIFF_WIDTH, n_q=q_cols, n_rope_k=DIFF_WIDTH, n_k=q_cols)
            qT, k, vT = _proj_in(x, g0, mod_x, w, tables, **dims)
            qcT, kc, vcT = _proj_in(ctx, g0, mod_c, w, None, **dims)
            lam = diff_lambda[e].astype(F32)
            subg = diff_sub_g[e].astype(F32).reshape(-1, 1)
            bias = _na_bias_tables(na_rpb[e], rows)
            a_x = _diff_attention(qT, k, vT, kc, vcT, lam, subg, lam_init, has_latent=True)
            b_x = _na_attention(qT, k, vT, kc, vcT, bias, has_latent=True)
            attn_x = [a_x, b_x]
            if ctx_out:
                a_c = _diff_attention(qcT, None, None, kc, vcT, lam, subg, lam_init, has_latent=False)
                b_c = _na_attention(qcT, None, None, kc, vcT, None, has_latent=False)
                attn_c = [a_c, b_c]
            w_out = w_out_ab[e].astype(BF16)
        else:
            o = layer // 2
            w = w_in_c[o].astype(BF16)
            dims = dict(n_rope_q=C_Q_COLS, n_q=C_Q_COLS, n_rope_k=C_KV_COLS, n_k=C_KV_COLS)
            qT, k, vT = _proj_in(x, g0, mod_x, w, tables, **dims)
            qcT, kc, vcT = _proj_in(ctx, g0, mod_c, w, None, **dims)
            attn_x = [_swa_attention(qT, k, vT, kc, vcT, attn_sinks[o], has_latent=True)]
            if ctx_out:
                attn_c = [_swa_attention(qcT, None, None, kc, vcT, attn_sinks[o], has_latent=False)]
            w_out = w_out_c[o].astype(BF16)
        w1, w3, w2 = ffn_w1[layer].astype(BF16), ffn_w3[layer].astype(BF16), ffn_w2[layer].astype(BF16)
        fg = final_g.reshape(1, -1) if layer == depth - 1 else None
        x = _out_ffn(x, attn_x, w_out, mod_x, g1, w1, w3, w2, fg)
        if ctx_out:
            ctx = _out_ffn(ctx, attn_c, w_out, mod_c, g1, w1, w3, w2)
    return x


def kernel(x, c, ctx, c_ctx, ada_w, ada_b, norm_g, w_in_ab, w_out_ab, diff_lambda, diff_sub_g, na_rpb, w_in_c,
           w_out_c, attn_sinks, ffn_w1, ffn_w3, ffn_w2, final_g):
    batch, _, d = x.shape
    depth = ada_w.shape[0]
    outs = []
    for b in range(batch):
        cc = jnp.zeros((8, d), F32).at[0].set(c[b].astype(F32)).at[1].set(c_ctx.astype(F32))
        mods = _adaln(cc, ada_w, ada_b)[:, :2].reshape(depth, 2, 6, d)
        outs.append(_trunk(x[b], ctx[b], mods, norm_g, w_in_ab, w_out_ab, diff_lambda, diff_sub_g, na_rpb,
                           w_in_c, w_out_c, attn_sinks, ffn_w1, ffn_w3, ffn_w2, final_g))
    return jnp.stack(outs).astype(x.dtype)
```

```python
import functools
import math

import jax
import jax.numpy as jnp
from jax import lax
from jax.experimental import pallas as pl
from jax.experimental.pallas import tpu as pltpu

F32 = jnp.float32
BF16 = jnp.bfloat16

D_MODEL = 1024
GRID_W = 64
HEAD_DIM = 64
ROPE_BASE = 10000.0
NORM_EPS = 1e-6
NEG_INF = -1e30
DIFF_HEADS = 4
DIFF_WIDTH = DIFF_HEADS * 2 * HEAD_DIM
NA_HEADS = 8
NA_WIDTH = NA_HEADS * HEAD_DIM
NA_MAX_ROWS = 8
NA_COLS = 16
GQA_HEADS = 16
GQA_KV_HEADS = 4
GQA_GROUP = GQA_HEADS // GQA_KV_HEADS
SWA_WINDOW = 128
C_Q_COLS = GQA_HEADS * HEAD_DIM
C_KV_COLS = GQA_KV_HEADS * HEAD_DIM
Q_SCALE = HEAD_DIM ** -0.5

LANES = 128
VMEM_LIMIT = 56 * 1024 * 1024

ROW_TILE = 512
DIFF_TQ = 512
DIFF_TK = 512
DIFF_QC = 256
DIFF_RC = 64
NA_TILE_ROWS = 4
NA_TQ = NA_TILE_ROWS * GRID_W
SWA_TQ = SWA_WINDOW


def _cparams(*sem):
    return pltpu.CompilerParams(dimension_semantics=sem, vmem_limit_bytes=VMEM_LIMIT)


def _resident(shape):
    nd = len(shape)
    return pl.BlockSpec(shape, lambda *_: (0,) * nd, pipeline_mode=pl.Buffered(1))


def _dot(a, b):
    return jnp.dot(a, b, preferred_element_type=F32)


def _adaln_kernel(cc_ref, w_ref, b_ref, o_ref):
    cc = cc_ref[...]
    h = (cc * jax.nn.sigmoid(cc)).astype(BF16)
    o_ref[0] = _dot(h, w_ref[0].astype(BF16)) + b_ref[0]


def _adaln(cc, ada_w, ada_b):
    depth, d, d6 = ada_w.shape
    nj = d6 // d
    return pl.pallas_call(
        _adaln_kernel,
        grid=(depth, nj),
        in_specs=[
            pl.BlockSpec((8, d), lambda l, j: (0, 0)),
            pl.BlockSpec((1, d, d), lambda l, j: (l, 0, j)),
            pl.BlockSpec((1, 1, d), lambda l, j: (l, 0, j)),
        ],
        out_specs=pl.BlockSpec((1, 8, d), lambda l, j: (l, 0, j)),
        out_shape=jax.ShapeDtypeStruct((depth, 8, d6), F32),
        compiler_params=_cparams("arbitrary", "arbitrary"),
        name="adaln",
    )(cc, ada_w, ada_b.reshape(depth, 1, d6))


def _norm_mod(x, g, shift, scale):
    ms = jnp.mean(x * x, axis=-1, keepdims=True)
    h = x * lax.rsqrt(ms + NORM_EPS) * g
    return h * (1.0 + scale) + shift


def _rope_lanes(t, cos, sin_signed):
    lane = lax.broadcasted_iota(jnp.int32, (t.shape[0], LANES), 1)
    first = (lane % (HEAD_DIM // 2)) < (HEAD_DIM // 4)
    out = []
    for c in range(t.shape[1] // LANES):
        tc = t[:, c * LANES:(c + 1) * LANES]
        partner = jnp.where(first, pltpu.roll(tc, LANES - HEAD_DIM // 4, 1), pltpu.roll(tc, HEAD_DIM // 4, 1))
        out.append(tc * cos + partner * sin_signed)
    return jnp.concatenate(out, axis=1)


def _proj_in_kernel(*refs, n_rope_q, n_q, n_rope_k, n_k, rope):
    if rope:
        x_ref, g_ref, mod_ref, cos_ref, sin_ref, w_ref, qT_ref, k_ref, vT_ref = refs
    else:
        x_ref, g_ref, mod_ref, w_ref, qT_ref, k_ref, vT_ref = refs
    h = _norm_mod(x_ref[...], g_ref[...], mod_ref[0:1, :], mod_ref[1:2, :])
    y = _dot(h.astype(BF16), w_ref[...])
    q = y[:, :n_q]
    k = y[:, n_q:n_q + n_k]
    v = y[:, n_q + n_k:]
    if rope:
        cos, sin = cos_ref[...], sin_ref[...]
        q = jnp.concatenate([_rope_lanes(q[:, :n_rope_q], cos, sin), q[:, n_rope_q:]], axis=1) \
            if n_rope_q < n_q else _rope_lanes(q, cos, sin)
        k = jnp.concatenate([_rope_lanes(k[:, :n_rope_k], cos, sin), k[:, n_rope_k:]], axis=1) \
            if n_rope_k < n_k else _rope_lanes(k, cos, sin)
    qT_ref[...] = (q * Q_SCALE).T.astype(BF16)
    k_ref[...] = k.astype(BF16)
    vT_ref[...] = v.T.astype(BF16)


def _proj_in(x, g, mod, w, tables, *, n_rope_q, n_q, n_rope_k, n_k):
    n, d = x.shape
    cols = w.shape[1]
    n_v = cols - n_q - n_k
    tm = min(ROW_TILE, n)
    rope = tables is not None
    row = lambda i: (i, 0)
    in_specs = [pl.BlockSpec((tm, d), row), _resident((1, d)), _resident(mod.shape)]
    args = [x, g, mod]
    if rope:
        in_specs += [pl.BlockSpec((tm, LANES), row), pl.BlockSpec((tm, LANES), row)]
        args += list(tables)
    in_specs.append(_resident(w.shape))
    args.append(w)
    return pl.pallas_call(
        functools.partial(_proj_in_kernel, n_rope_q=n_rope_q, n_q=n_q, n_rope_k=n_rope_k, n_k=n_k, rope=rope),
        grid=(n // tm,),
        in_specs=in_specs,
        out_specs=[
            pl.BlockSpec((n_q, tm), lambda i: (0, i)),
            pl.BlockSpec((tm, n_k), row),
            pl.BlockSpec((n_v, tm), lambda i: (0, i)),
        ],
        out_shape=[
            jax.ShapeDtypeStruct((n_q, n), BF16),
            jax.ShapeDtypeStruct((n, n_k), BF16),
            jax.ShapeDtypeStruct((n_v, n), BF16),
        ],
        compiler_params=_cparams("arbitrary"),
        name="proj_in",
    )(*args)


def _half_masked(qT_pair, e):
    z = jnp.zeros((HEAD_DIM, qT_pair.shape[1]), qT_pair.dtype)
    if e == 0:
        return jnp.concatenate([qT_pair[:HEAD_DIM], z], axis=0)
    return jnp.concatenate([z, qT_pair[HEAD_DIM:]], axis=0)


def _diff_kernel(*refs, lam_init, has_latent):
    if has_latent:
        qT_ref, k_ref, vT_ref, kc_ref, vcT_ref, lam_ref, subg_ref, o_ref = refs[:8]
        qm_ref, m_ref, l_ref, acc_ref, s_ref, p_ref = refs[8:]
        j = pl.program_id(1)
        last = pl.num_programs(1) - 1
    else:
        qT_ref, kc_ref, vcT_ref, lam_ref, subg_ref, o_ref = refs[:6]
        qm_ref, m_ref, l_ref, acc_ref, s_ref, p_ref = refs[6:]

    tq = qm_ref.shape[2]
    items = [(h, e, c) for h in range(DIFF_HEADS) for e in range(2) for c in range(0, tq, DIFF_QC)]

    def start():
        for h in range(DIFF_HEADS):
            pair = qT_ref[h * LANES:(h + 1) * LANES, :]
            for e in range(2):
                qm_ref[2 * h + e] = _half_masked(pair, e)
        m_ref[...] = jnp.full(m_ref.shape, NEG_INF, F32)
        l_ref[...] = jnp.zeros(l_ref.shape, F32)
        acc_ref[...] = jnp.zeros(acc_ref.shape, F32)
        fold(kc_ref, vcT_ref)

    def fold(keys_ref, valsT_ref):
        nk = keys_ref.shape[0]
        sub = 8

        def scores(i):
            h, e, c = items[i]
            mp = 2 * h + e
            s = _dot(keys_ref[:, h * LANES:(h + 1) * LANES], qm_ref[mp, :, c:c + DIFF_QC])
            s_ref[i % 2, :nk, :] = s
            m_part = jnp.max(s.reshape(nk // sub, sub, DIFF_QC), axis=0)
            m_old = m_ref[mp, :, c:c + DIFF_QC]
            m_new = jnp.maximum(m_old, jnp.max(m_part, axis=0, keepdims=True))
            m_ref[mp, :, c:c + DIFF_QC] = m_new
            return m_new, jnp.exp(m_old - m_new)

        def absorb(i, m_new, alpha):
            h, e, c = items[i]
            mp = 2 * h + e
            cols = slice(c, c + DIFF_QC)
            l_part = jnp.zeros((sub, DIFF_QC), F32)
            for r in range(0, nk, DIFF_RC):
                p = jnp.exp(s_ref[i % 2, r:r + DIFF_RC, :] - m_new)
                l_part = l_part + jnp.sum(p.reshape(DIFF_RC // sub, sub, DIFF_QC), axis=0)
                p_ref[i % 2, r:r + DIFF_RC, :] = p.astype(BF16)
            l_ref[mp, :, cols] = alpha * l_ref[mp, :, cols] + jnp.sum(l_part, axis=0, keepdims=True)
            pv = _dot(valsT_ref[h * LANES:(h + 1) * LANES, :], p_ref[i % 2, :nk, :])
            acc_ref[mp, :, cols] = alpha * acc_ref[mp, :, cols] + pv

        nxt = scores(0)
        for i in range(len(items)):
            cur = nxt
            if i + 1 < len(items):
                nxt = scores(i + 1)
            absorb(i, *cur)

    def finish():
        lam = lam_ref[...]
        lam_val = (jnp.exp(jnp.sum(lam[0:1] * lam[1:2], axis=1, keepdims=True))
                   - jnp.exp(jnp.sum(lam[2:3] * lam[3:4], axis=1, keepdims=True)) + lam_init)
        subg = subg_ref[...]
        for h in range(DIFF_HEADS):
            a0 = acc_ref[2 * h] / l_ref[2 * h]
            a1 = acc_ref[2 * h + 1] / l_ref[2 * h + 1]
            a = a0 - lam_val * a1
            ms = jnp.mean(a * a, axis=0, keepdims=True)
            a = a * lax.rsqrt(ms + NORM_EPS) * subg * (1.0 - lam_init)
            o_ref[:, h * LANES:(h + 1) * LANES] = a.T.astype(o_ref.dtype)

    if has_latent:
        pl.when(j == 0)(start)
        fold(k_ref, vT_ref)
        pl.when(j == last)(finish)
    else:
        start()
        finish()


def _diff_attention(qT, k, vT, kc, vcT, lam, subg, lam_init, *, has_latent):
    n = qT.shape[1]
    nc = kc.shape[0]
    w = DIFF_WIDTH
    small = [_resident((4, HEAD_DIM)), _resident((2 * HEAD_DIM, 1))]
    if has_latent:
        tq, tk = DIFF_TQ, DIFF_TK
        grid = (n // tq, k.shape[0] // tk)
        in_specs = [
            pl.BlockSpec((w, tq), lambda i, j: (0, i)),
            pl.BlockSpec((tk, w), lambda i, j: (j, 0)),
            pl.BlockSpec((w, tk), lambda i, j: (0, j)),
            pl.BlockSpec((nc, w), lambda i, j: (0, 0)),
            pl.BlockSpec((w, nc), lambda i, j: (0, 0)),
        ] + small
        args = (qT, k, vT, kc, vcT, lam, subg)
        out_spec = pl.BlockSpec((tq, w), lambda i, j: (i, 0))
        sem = ("parallel", "arbitrary")
    else:
        tq, tk = n, 0
        grid = (1,)
        in_specs = [
            pl.BlockSpec((w, tq), lambda i: (0, 0)),
            pl.BlockSpec((nc, w), lambda i: (0, 0)),
            pl.BlockSpec((w, nc), lambda i: (0, 0)),
        ] + small
        args = (qT, kc, vcT, lam, subg)
        out_spec = pl.BlockSpec((tq, w), lambda i: (0, 0))
        sem = ("arbitrary",)
    return pl.pallas_call(
        functools.partial(_diff_kernel, lam_init=lam_init, has_latent=has_latent),
        grid=grid,
        in_specs=in_specs,
        out_specs=out_spec,
        out_shape=jax.ShapeDtypeStruct((n, w), BF16),
        scratch_shapes=[
            pltpu.VMEM((2 * DIFF_HEADS, LANES, tq), BF16),
            pltpu.VMEM((2 * DIFF_HEADS, 1, tq), F32),
            pltpu.VMEM((2 * DIFF_HEADS, 1, tq), F32),
            pltpu.VMEM((2 * DIFF_HEADS, LANES, tq), F32),
            pltpu.VMEM((2, max(tk, nc), DIFF_QC), F32),
            pltpu.VMEM((2, max(tk, nc), DIFF_QC), BF16),
        ],
        compiler_params=_cparams(*sem),
        name="diff_attn" if has_latent else "diff_attn_ctx",
    )(*args)


def _na_bias_kernel(rpb_ref, o_ref):
    h = pl.program_id(0)
    n_dr, n_dc = 2 * NA_MAX_ROWS - 1, 2 * NA_COLS - 1
    kc = lax.broadcasted_iota(jnp.int32, (GRID_W, LANES), 0)
    lane = lax.broadcasted_iota(jnp.int32, (GRID_W, LANES), 1)
    c = lane % GRID_W
    upper = lane >= GRID_W
    dc = kc - c + (NA_COLS - 1)
    cs = jnp.clip(c - NA_COLS // 2, 0, GRID_W - NA_COLS)
    col_ok = (kc >= cs) & (kc < cs + NA_COLS)
    for i in range(o_ref.shape[1]):
        dr_idx = (i - 1, i - 2)
        val = jnp.full((GRID_W, LANES), NEG_INF, F32)
        for d in range(n_dc):
            lo, up = (rpb_ref[(h * n_dr + r) * n_dc + d] if 0 <= r < n_dr else NEG_INF for r in dr_idx)
            val = jnp.where(dc == d, jnp.where(upper, up, lo), val)
        o_ref[0, i] = jnp.where(col_ok, val, NEG_INF)


def _na_bias_blocks(rpb):
    n_heads = rpb.shape[0]
    n_blocks = 2 * NA_MAX_ROWS
    return pl.pallas_call(
        _na_bias_kernel,
        grid=(n_heads,),
        in_specs=[pl.BlockSpec(memory_space=pltpu.SMEM)],
        out_specs=pl.BlockSpec((1, n_blocks, GRID_W, LANES), lambda h: (h, 0, 0, 0)),
        out_shape=jax.ShapeDtypeStruct((n_heads, n_blocks, GRID_W, LANES), F32),
        compiler_params=_cparams("arbitrary"),
        name="na_bias",
    )(rpb.astype(F32).reshape(-1))


def _na_window_bias(bias_ref, e, t, grid_rows):
    lane = lax.broadcasted_iota(jnp.int32, (GRID_W, LANES), 1)
    r0 = t * NA_TILE_ROWS
    rows = []
    for rk in range(3 * NA_TILE_ROWS):
        kr = r0 - NA_TILE_ROWS + rk
        cols = []
        for p in range(NA_TILE_ROWS // 2):
            blk = bias_ref[e, rk - NA_TILE_ROWS - 2 * p + NA_MAX_ROWS]
            r = r0 + 2 * p + lane // GRID_W
            rs = jnp.clip(r - NA_MAX_ROWS // 2, 0, grid_rows - NA_MAX_ROWS)
            ok = (kr >= rs) & (kr < rs + NA_MAX_ROWS)
            cols.append(jnp.where(ok, blk, NEG_INF))
        rows.append(jnp.concatenate(cols, axis=1))
    return jnp.concatenate(rows, axis=0)


def _na_kernel(*refs, has_latent, grid_rows):
    if has_latent:
        qT_ref, km_ref, k0_ref, kp_ref, vm_ref, v0_ref, vp_ref, kc_ref, vcT_ref, bias_ref, o_ref = refs
    else:
        qT_ref, kc_ref, vcT_ref, o_ref = refs
    pair = qT_ref[...]
    kc = kc_ref[...]
    if has_latent:
        vT = jnp.concatenate([vm_ref[...], v0_ref[...], vp_ref[...], vcT_ref[...]], axis=1)
    else:
        vT = vcT_ref[...]
    outs = []
    for e in range(2):
        qm = _half_masked(pair, e)
        s = _dot(kc, qm)
        if has_latent:
            s_nb = jnp.concatenate([_dot(km_ref[...], qm), _dot(k0_ref[...], qm), _dot(kp_ref[...], qm)], axis=0)
            s = jnp.concatenate([s_nb + _na_window_bias(bias_ref, e, pl.program_id(1), grid_rows), s], axis=0)
        m = jnp.max(s, axis=0, keepdims=True)
        p = jnp.exp(s - m)
        l = jnp.sum(p, axis=0, keepdims=True)
        outs.append(_dot(vT[e * HEAD_DIM:(e + 1) * HEAD_DIM], p.astype(BF16)) / l)
    o_ref[...] = jnp.concatenate(outs, axis=0).T.astype(o_ref.dtype)


def _na_attention(qT_all, k_all, vT_all, kc_all, vcT_all, bias, *, has_latent):
    n = qT_all.shape[1]
    nc = kc_all.shape[0]
    g0 = DIFF_WIDTH // LANES
    n_pairs = NA_HEADS // 2
    if has_latent:
        tq = NA_TQ
        nt = n // tq
        prev = lambda t: jnp.maximum(t - 1, 0)
        nxt = lambda t: jnp.minimum(t + 1, nt - 1)
        in_specs = [
            pl.BlockSpec((LANES, tq), lambda hp, t: (g0 + hp, t)),
            pl.BlockSpec((tq, LANES), lambda hp, t: (prev(t), g0 + hp)),
            pl.BlockSpec((tq, LANES), lambda hp, t: (t, g0 + hp)),
            pl.BlockSpec((tq, LANES), lambda hp, t: (nxt(t), g0 + hp)),
            pl.BlockSpec((LANES, tq), lambda hp, t: (g0 + hp, prev(t))),
            pl.BlockSpec((LANES, tq), lambda hp, t: (g0 + hp, t)),
            pl.BlockSpec((LANES, tq), lambda hp, t: (g0 + hp, nxt(t))),
            pl.BlockSpec((nc, LANES), lambda hp, t: (0, g0 + hp)),
            pl.BlockSpec((LANES, nc), lambda hp, t: (g0 + hp, 0)),
            pl.BlockSpec((2,) + bias.shape[1:], lambda hp, t: (hp, 0, 0, 0)),
        ]
        args = (qT_all, k_all, k_all, k_all, vT_all, vT_all, vT_all, kc_all, vcT_all, bias)
    else:
        tq = n
        nt = 1
        in_specs = [
            pl.BlockSpec((LANES, tq), lambda hp, t: (g0 + hp, 0)),
            pl.BlockSpec((nc, LANES), lambda hp, t: (0, g0 + hp)),
            pl.BlockSpec((LANES, nc), lambda hp, t: (g0 + hp, 0)),
        ]
        args = (qT_all, kc_all, vcT_all)
    return pl.pallas_call(
        functools.partial(_na_kernel, has_latent=has_latent, grid_rows=n // GRID_W),
        grid=(n_pairs, nt),
        in_specs=in_specs,
        out_specs=pl.BlockSpec((tq, LANES), lambda hp, t: (t, hp)),
        out_shape=jax.ShapeDtypeStruct((n, NA_WIDTH), BF16),
        compiler_params=_cparams("arbitrary", "arbitrary"),
        name="na_attn" if has_latent else "na_attn_ctx",
    )(*args)


def _swa_kernel(*refs, has_latent):
    if has_latent:
        qT_ref, km_ref, k0_ref, kp_ref, vm_ref, v0_ref, vp_ref, kc_ref, vcT_ref, sink_ref, o_ref = refs
        b = pl.program_id(0)
        nb = pl.num_programs(0)
        tq = qT_ref.shape[1]
        ki = lax.broadcasted_iota(jnp.int32, (tq, tq), 0)
        qi = lax.broadcasted_iota(jnp.int32, (tq, tq), 1)
        band_prev = jnp.concatenate([(ki >= qi) & (b > 0)] * GQA_GROUP, axis=1)
        band_next = jnp.concatenate([(ki <= qi) & (b < nb - 1)] * GQA_GROUP, axis=1)
    else:
        qT_ref, kc_ref, vcT_ref, sink_ref, o_ref = refs
        tq = qT_ref.shape[1]
    zeros = jnp.zeros((HEAD_DIM, tq), BF16)
    outs = []
    for g in range(GQA_KV_HEADS):
        grp, half = g // 2, g % 2
        lanes = slice(grp * LANES, (grp + 1) * LANES)
        rows = slice(g * HEAD_DIM, (g + 1) * HEAD_DIM)
        q_cols = []
        for jh in range(GQA_GROUP):
            hq = g * GQA_GROUP + jh
            qh = qT_ref[hq * HEAD_DIM:(hq + 1) * HEAD_DIM, :]
            q_cols.append(jnp.concatenate([qh, zeros] if half == 0 else [zeros, qh], axis=0))
        qm = jnp.concatenate(q_cols, axis=1)
        s = _dot(kc_ref[:, lanes], qm)
        vT = vcT_ref[rows, :]
        if has_latent:
            s_prev = jnp.where(band_prev, _dot(km_ref[:, lanes], qm), NEG_INF)
            s_next = jnp.where(band_next, _dot(kp_ref[:, lanes], qm), NEG_INF)
            s = jnp.concatenate([s, s_prev, _dot(k0_ref[:, lanes], qm), s_next], axis=0)
            vT = jnp.concatenate([vT, vm_ref[rows, :], v0_ref[rows, :], vp_ref[rows, :]], axis=1)
        sink = sink_ref[g]
        m = jnp.maximum(jnp.max(s, axis=0, keepdims=True), sink)
        p = jnp.exp(s - m)
        l = jnp.sum(p, axis=0, keepdims=True) + jnp.exp(sink - m)
        o = _dot(vT, p.astype(BF16)) / l
        outs += [o[:, jh * tq:(jh + 1) * tq] for jh in range(GQA_GROUP)]
    o_ref[...] = jnp.concatenate(outs, axis=0).T.astype(o_ref.dtype)


def _swa_attention(qT, k, vT, kc, vcT, sinks, *, has_latent):
    n = qT.shape[1]
    nc = kc.shape[0]
    tq = SWA_TQ
    nb = n // tq
    sink_rows = jnp.repeat(sinks.astype(F32).reshape(GQA_KV_HEADS, 1, GQA_GROUP), tq, axis=2)
    ctx_specs = [_resident((nc, C_KV_COLS)), _resident((C_KV_COLS, nc)), _resident(sink_rows.shape)]
    if has_latent:
        prev = lambda b: jnp.maximum(b - 1, 0)
        nxt = lambda b: jnp.minimum(b + 1, nb - 1)
        in_specs = [
            pl.BlockSpec((C_Q_COLS, tq), lambda b: (0, b)),
            pl.BlockSpec((tq, C_KV_COLS), lambda b: (prev(b), 0)),
            pl.BlockSpec((tq, C_KV_COLS), lambda b: (b, 0)),
            pl.BlockSpec((tq, C_KV_COLS), lambda b: (nxt(b), 0)),
            pl.BlockSpec((C_KV_COLS, tq), lambda b: (0, prev(b))),
            pl.BlockSpec((C_KV_COLS, tq), lambda b: (0, b)),
            pl.BlockSpec((C_KV_COLS, tq), lambda b: (0, nxt(b))),
        ] + ctx_specs
        args = (qT, k, k, k, vT, vT, vT, kc, vcT, sink_rows)
    else:
        in_specs = [pl.BlockSpec((C_Q_COLS, tq), lambda b: (0, b))] + ctx_specs
        args = (qT, kc, vcT, sink_rows)
    return pl.pallas_call(
        functools.partial(_swa_kernel, has_latent=has_latent),
        grid=(nb,),
        in_specs=in_specs,
        out_specs=pl.BlockSpec((tq, C_Q_COLS), lambda b: (b, 0)),
        out_shape=jax.ShapeDtypeStruct((n, C_Q_COLS), BF16),
        compiler_params=_cparams("arbitrary"),
        name="swa_attn" if has_latent else "swa_attn_ctx",
    )(*args)


def _out_ffn_kernel(*refs, n_attn, final):
    x_ref = refs[0]
    attn_refs = refs[1:1 + n_attn]
    wout_ref, mod_ref, g_ref, w1_ref, w3_ref, w2_ref = refs[1 + n_attn:7 + n_attn]
    rest = refs[7 + n_attn:]
    if final:
        fg_ref, o_ref = rest
    else:
        (o_ref,) = rest
    proj = None
    r0 = 0
    for a_ref in attn_refs:
        wdt = a_ref.shape[1]
        part = _dot(a_ref[...], wout_ref[r0:r0 + wdt, :])
        proj = part if proj is None else proj + part
        r0 += wdt
    x1 = x_ref[...] + mod_ref[2:3, :] * proj
    h = _norm_mod(x1, g_ref[...], mod_ref[3:4, :], mod_ref[4:5, :]).astype(BF16)
    hidden = w1_ref.shape[1]
    chunk = hidden // 2
    ffn = None
    for c0 in range(0, hidden, chunk):
        a = _dot(h, w1_ref[:, c0:c0 + chunk])
        b = _dot(h, w3_ref[:, c0:c0 + chunk])
        gated = (a * jax.nn.sigmoid(a) * b).astype(BF16)
        part = _dot(gated, w2_ref[c0:c0 + chunk, :])
        ffn = part if ffn is None else ffn + part
    x2 = x1 + mod_ref[5:6, :] * ffn
    if final:
        ms = jnp.mean(x2 * x2, axis=-1, keepdims=True)
        x2 = x2 * lax.rsqrt(ms + NORM_EPS) * fg_ref[...]
    o_ref[...] = x2


def _out_ffn(x, attn, w_out, mod, g, w1, w3, w2, final_g=None):
    n, d = x.shape
    tm = min(ROW_TILE, n)
    row = lambda i: (i, 0)
    final = final_g is not None
    in_specs = [pl.BlockSpec((tm, d), row)]
    in_specs += [pl.BlockSpec((tm, a.shape[1]), row) for a in attn]
    in_specs += [_resident(w_out.shape), _resident(mod.shape), _resident((1, d)),
                 _resident(w1.shape), _resident(w3.shape), _resident(w2.shape)]
    args = [x, *attn, w_out, mod, g, w1, w3, w2]
    if final:
        in_specs.append(_resident((1, d)))
        args.append(final_g)
    return pl.pallas_call(
        functools.partial(_out_ffn_kernel, n_attn=len(attn), final=final),
        grid=(n // tm,),
        in_specs=in_specs,
        out_specs=pl.BlockSpec((tm, d), row),
        out_shape=jax.ShapeDtypeStruct((n, d), F32),
        compiler_params=_cparams("arbitrary"),
        name="out_ffn",
    )(*args)


def _rope_tables(n):
    t = jnp.arange(n)
    row = (t // GRID_W).astype(F32)
    col = (t % GRID_W).astype(F32)
    quarter = HEAD_DIM // 4
    inv = ROPE_BASE ** (-jnp.arange(quarter, dtype=F32) / quarter)
    ar = row[:, None] * inv
    ac = col[:, None] * inv
    ang = jnp.concatenate([ar, ar, ac, ac], axis=-1)
    sign = jnp.tile(jnp.concatenate([-jnp.ones(quarter, F32), jnp.ones(quarter, F32)]), 2)
    cos = jnp.tile(jnp.cos(ang), (1, LANES // HEAD_DIM))
    sin = jnp.tile(jnp.sin(ang) * sign, (1, LANES // HEAD_DIM))
    return cos, sin


def _trunk(x, ctx, mods, norm_g, w_in_ab, w_out_ab, diff_lambda, diff_sub_g, na_rpb, w_in_c, w_out_c,
           attn_sinks, ffn_w1, ffn_w3, ffn_w2, final_g):
    depth = mods.shape[0]
    s_len = x.shape[0]
    tables = _rope_tables(s_len)
    for layer in range(depth):
        ctx_out = layer < depth - 1
        mod_x, mod_c = mods[layer, 0], mods[layer, 1]
        g0 = norm_g[layer, 0].reshape(1, -1)
        g1 = norm_g[layer, 1].reshape(1, -1)
        if layer % 2 == 0:
            e = layer // 2
            lam_init = 0.8 - 0.6 * math.exp(-0.3 * layer)
            w = w_in_ab[e]
            q_cols = DIFF_WIDTH + NA_WIDTH
            a_k, a_v, b_k, b_v = (w[:, q_cols + i * DIFF_WIDTH:q_cols + (i + 1) * DIFF_WIDTH] for i in range(4))
            w = jnp.concatenate([w[:, :q_cols], a_k, b_k, a_v, b_v], axis=1).astype(BF16)
            dims = dict(n_rope_q=DIFF_WIDTH, n_q=q_cols, n_rope_k=DIFF_WIDTH, n_k=q_cols)
            qT, k, vT = _proj_in(x, g0, mod_x, w, tables, **dims)
            qcT, kc, vcT = _proj_in(ctx, g0, mod_c, w, None, **dims)
            lam = diff_lambda[e].astype(F32)
            subg = diff_sub_g[e].astype(F32).reshape(-1, 1)
            bias = _na_bias_blocks(na_rpb[e])
            a_x = _diff_attention(qT, k, vT, kc, vcT, lam, subg, lam_init, has_latent=True)
            b_x = _na_attention(qT, k, vT, kc, vcT, bias, has_latent=True)
            attn_x = [a_x, b_x]
            if ctx_out:
                a_c = _diff_attention(qcT, None, None, kc, vcT, lam, subg, lam_init, has_latent=False)
                b_c = _na_attention(qcT, None, None, kc, vcT, None, has_latent=False)
                attn_c = [a_c, b_c]
            w_out = w_out_ab[e].astype(BF16)
        else:
            o = layer // 2
            w = w_in_c[o].astype(BF16)
            dims = dict(n_rope_q=C_Q_COLS, n_q=C_Q_COLS, n_rope_k=C_KV_COLS, n_k=C_KV_COLS)
            qT, k, vT = _proj_in(x, g0, mod_x, w, tables, **dims)
            qcT, kc, vcT = _proj_in(ctx, g0, mod_c, w, None, **dims)
            attn_x = [_swa_attention(qT, k, vT, kc, vcT, attn_sinks[o], has_latent=True)]
            if ctx_out:
                attn_c = [_swa_attention(qcT, None, None, kc, vcT, attn_sinks[o], has_latent=False)]
            w_out = w_out_c[o].astype(BF16)
        w1, w3, w2 = ffn_w1[layer].astype(BF16), ffn_w3[layer].astype(BF16), ffn_w2[layer].astype(BF16)
        fg = final_g.reshape(1, -1) if layer == depth - 1 else None
        x = _out_ffn(x, attn_x, w_out, mod_x, g1, w1, w3, w2, fg)
        if ctx_out:
            ctx = _out_ffn(ctx, attn_c, w_out, mod_c, g1, w1, w3, w2)
    return x


def kernel(x, c, ctx, c_ctx, ada_w, ada_b, norm_g, w_in_ab, w_out_ab, diff_lambda, diff_sub_g, na_rpb, w_in_c,
           w_out_c, attn_sinks, ffn_w1, ffn_w3, ffn_w2, final_g):
    batch, _, d = x.shape
    depth = ada_w.shape[0]
    outs = []
    for b in range(batch):
        cc = jnp.zeros((8, d), F32).at[0].set(c[b].astype(F32)).at[1].set(c_ctx.astype(F32))
        mods = _adaln(cc, ada_w, ada_b)[:, :2].reshape(depth, 2, 6, d)
        outs.append(_trunk(x[b], ctx[b], mods, norm_g, w_in_ab, w_out_ab, diff_lambda, diff_sub_g, na_rpb,
                           w_in_c, w_out_c, attn_sinks, ffn_w1, ffn_w3, ffn_w2, final_g))
    return jnp.stack(outs).astype(x.dtype)
```

```python
import functools
import math

import jax
import jax.numpy as jnp
from jax import lax
from jax.experimental import pallas as pl
from jax.experimental.pallas import tpu as pltpu

F32 = jnp.float32
BF16 = jnp.bfloat16

D_MODEL = 1024
GRID_W = 64
HEAD_DIM = 64
ROPE_BASE = 10000.0
NORM_EPS = 1e-6
NEG_INF = -1e30
DIFF_HEADS = 4
DIFF_WIDTH = DIFF_HEADS * 2 * HEAD_DIM
NA_HEADS = 8
NA_WIDTH = NA_HEADS * HEAD_DIM
NA_MAX_ROWS = 8
NA_COLS = 16
GQA_HEADS = 16
GQA_KV_HEADS = 4
GQA_GROUP = GQA_HEADS // GQA_KV_HEADS
SWA_WINDOW = 128
C_Q_COLS = GQA_HEADS * HEAD_DIM
C_KV_COLS = GQA_KV_HEADS * HEAD_DIM
Q_SCALE = HEAD_DIM ** -0.5

LANES = 128
VMEM_LIMIT = 56 * 1024 * 1024

ROW_TILE = 512
DIFF_TQ = 512
DIFF_TK = 512
DIFF_QC = 256
DIFF_RC = 64
DIFF_DEPTH = 4
NA_TILE_ROWS = 4
NA_TQ = NA_TILE_ROWS * GRID_W
SWA_TQ = SWA_WINDOW


def _cparams(*sem):
    return pltpu.CompilerParams(dimension_semantics=sem, vmem_limit_bytes=VMEM_LIMIT)


def _resident(shape):
    nd = len(shape)
    return pl.BlockSpec(shape, lambda *_: (0,) * nd, pipeline_mode=pl.Buffered(1))


def _dot(a, b):
    return jnp.dot(a, b, preferred_element_type=F32)


def _adaln_kernel(cc_ref, w_ref, b_ref, o_ref):
    cc = cc_ref[...]
    h = (cc * jax.nn.sigmoid(cc)).astype(BF16)
    o_ref[0] = _dot(h, w_ref[0].astype(BF16)) + b_ref[0]


def _adaln(cc, ada_w, ada_b):
    depth, d, d6 = ada_w.shape
    nj = d6 // d
    return pl.pallas_call(
        _adaln_kernel,
        grid=(depth, nj),
        in_specs=[
            pl.BlockSpec((8, d), lambda l, j: (0, 0)),
            pl.BlockSpec((1, d, d), lambda l, j: (l, 0, j)),
            pl.BlockSpec((1, 1, d), lambda l, j: (l, 0, j)),
        ],
        out_specs=pl.BlockSpec((1, 8, d), lambda l, j: (l, 0, j)),
        out_shape=jax.ShapeDtypeStruct((depth, 8, d6), F32),
        compiler_params=_cparams("arbitrary", "arbitrary"),
        name="adaln",
    )(cc, ada_w, ada_b.reshape(depth, 1, d6))


def _norm_mod(x, g, shift, scale):
    ms = jnp.mean(x * x, axis=-1, keepdims=True)
    h = x * lax.rsqrt(ms + NORM_EPS) * g
    return h * (1.0 + scale) + shift


def _rope_lanes(t, cos, sin_signed):
    lane = lax.broadcasted_iota(jnp.int32, (t.shape[0], LANES), 1)
    first = (lane % (HEAD_DIM // 2)) < (HEAD_DIM // 4)
    out = []
    for c in range(t.shape[1] // LANES):
        tc = t[:, c * LANES:(c + 1) * LANES]
        partner = jnp.where(first, pltpu.roll(tc, LANES - HEAD_DIM // 4, 1), pltpu.roll(tc, HEAD_DIM // 4, 1))
        out.append(tc * cos + partner * sin_signed)
    return jnp.concatenate(out, axis=1)


def _proj_in_kernel(*refs, n_rope_q, n_q, n_rope_k, n_k, rope):
    if rope:
        x_ref, g_ref, mod_ref, cos_ref, sin_ref, w_ref, qT_ref, k_ref, vT_ref = refs
    else:
        x_ref, g_ref, mod_ref, w_ref, qT_ref, k_ref, vT_ref = refs
    h = _norm_mod(x_ref[...], g_ref[...], mod_ref[0:1, :], mod_ref[1:2, :])
    y = _dot(h.astype(BF16), w_ref[...])
    q = y[:, :n_q]
    k = y[:, n_q:n_q + n_k]
    v = y[:, n_q + n_k:]
    if rope:
        cos, sin = cos_ref[...], sin_ref[...]
        q = jnp.concatenate([_rope_lanes(q[:, :n_rope_q], cos, sin), q[:, n_rope_q:]], axis=1) \
            if n_rope_q < n_q else _rope_lanes(q, cos, sin)
        k = jnp.concatenate([_rope_lanes(k[:, :n_rope_k], cos, sin), k[:, n_rope_k:]], axis=1) \
            if n_rope_k < n_k else _rope_lanes(k, cos, sin)
    qT_ref[...] = (q * Q_SCALE).T.astype(BF16)
    k_ref[...] = k.astype(BF16)
    vT_ref[...] = v.T.astype(BF16)


def _proj_in(x, g, mod, w, tables, *, n_rope_q, n_q, n_rope_k, n_k):
    n, d = x.shape
    cols = w.shape[1]
    n_v = cols - n_q - n_k
    tm = min(ROW_TILE, n)
    rope = tables is not None
    row = lambda i: (i, 0)
    in_specs = [pl.BlockSpec((tm, d), row), _resident((1, d)), _resident(mod.shape)]
    args = [x, g, mod]
    if rope:
        in_specs += [pl.BlockSpec((tm, LANES), row), pl.BlockSpec((tm, LANES), row)]
        args += list(tables)
    in_specs.append(_resident(w.shape))
    args.append(w)
    return pl.pallas_call(
        functools.partial(_proj_in_kernel, n_rope_q=n_rope_q, n_q=n_q, n_rope_k=n_rope_k, n_k=n_k, rope=rope),
        grid=(n // tm,),
        in_specs=in_specs,
        out_specs=[
            pl.BlockSpec((n_q, tm), lambda i: (0, i)),
            pl.BlockSpec((tm, n_k), row),
            pl.BlockSpec((n_v, tm), lambda i: (0, i)),
        ],
        out_shape=[
            jax.ShapeDtypeStruct((n_q, n), BF16),
            jax.ShapeDtypeStruct((n, n_k), BF16),
            jax.ShapeDtypeStruct((n_v, n), BF16),
        ],
        compiler_params=_cparams("arbitrary"),
        name="proj_in",
    )(*args)


def _half_masked(qT_pair, e):
    z = jnp.zeros((HEAD_DIM, qT_pair.shape[1]), qT_pair.dtype)
    if e == 0:
        return jnp.concatenate([qT_pair[:HEAD_DIM], z], axis=0)
    return jnp.concatenate([z, qT_pair[HEAD_DIM:]], axis=0)


def _diff_kernel(*refs, lam_init, has_latent):
    if has_latent:
        qT_ref, k_ref, vT_ref, kc_ref, vcT_ref, lam_ref, subg_ref, o_ref = refs[:8]
        qm_ref, m_ref, alpha_ref, l_ref, acc_ref, s_ref, p_ref = refs[8:]
        j = pl.program_id(1)
        last = pl.num_programs(1) - 1
    else:
        qT_ref, kc_ref, vcT_ref, lam_ref, subg_ref, o_ref = refs[:6]
        qm_ref, m_ref, alpha_ref, l_ref, acc_ref, s_ref, p_ref = refs[6:]

    tq = qm_ref.shape[2]
    items = [(h, e, c) for h in range(DIFF_HEADS) for e in range(2) for c in range(0, tq, DIFF_QC)]

    def start():
        for h in range(DIFF_HEADS):
            pair = qT_ref[h * LANES:(h + 1) * LANES, :]
            for e in range(2):
                qm_ref[2 * h + e] = _half_masked(pair, e)
        m_ref[...] = jnp.full(m_ref.shape, NEG_INF, F32)
        l_ref[...] = jnp.zeros(l_ref.shape, F32)
        acc_ref[...] = jnp.zeros(acc_ref.shape, F32)
        fold(kc_ref, vcT_ref)

    def fold(keys_ref, valsT_ref):
        nk = keys_ref.shape[0]
        sub = 8

        def scores(i):
            h, e, c = items[i]
            mp = 2 * h + e
            s = _dot(keys_ref[:, h * LANES:(h + 1) * LANES], qm_ref[mp, :, c:c + DIFF_QC])
            s_ref[i % (DIFF_DEPTH + 1), :nk, :] = s
            m_part = jnp.max(s.reshape(nk // sub, sub, DIFF_QC), axis=0)
            m_old = m_ref[mp, :, c:c + DIFF_QC]
            m_new = jnp.maximum(m_old, jnp.max(m_part, axis=0, keepdims=True))
            m_ref[mp, :, c:c + DIFF_QC] = m_new
            alpha_ref[mp, :, c:c + DIFF_QC] = jnp.exp(m_old - m_new)

        def absorb(i):
            h, e, c = items[i]
            mp = 2 * h + e
            cols = slice(c, c + DIFF_QC)
            m_new = m_ref[mp, :, cols]
            alpha = alpha_ref[mp, :, cols]
            l_part = jnp.zeros((sub, DIFF_QC), F32)
            for r in range(0, nk, DIFF_RC):
                p = jnp.exp(s_ref[i % (DIFF_DEPTH + 1), r:r + DIFF_RC, :] - m_new)
                l_part = l_part + jnp.sum(p.reshape(DIFF_RC // sub, sub, DIFF_QC), axis=0)
                p_ref[i % 2, r:r + DIFF_RC, :] = p.astype(BF16)
            l_ref[mp, :, cols] = alpha * l_ref[mp, :, cols] + jnp.sum(l_part, axis=0, keepdims=True)
            pv = _dot(valsT_ref[h * LANES:(h + 1) * LANES, :], p_ref[i % 2, :nk, :])
            acc_ref[mp, :, cols] = alpha * acc_ref[mp, :, cols] + pv

        for i in range(min(DIFF_DEPTH, len(items))):
            scores(i)
        for i in range(len(items)):
            if i + DIFF_DEPTH < len(items):
                scores(i + DIFF_DEPTH)
            absorb(i)

    def finish():
        lam = lam_ref[...]
        lam_val = (jnp.exp(jnp.sum(lam[0:1] * lam[1:2], axis=1, keepdims=True))
                   - jnp.exp(jnp.sum(lam[2:3] * lam[3:4], axis=1, keepdims=True)) + lam_init)
        subg = subg_ref[...]
        for h in range(DIFF_HEADS):
            a0 = acc_ref[2 * h] / l_ref[2 * h]
            a1 = acc_ref[2 * h + 1] / l_ref[2 * h + 1]
            a = a0 - lam_val * a1
            ms = jnp.mean(a * a, axis=0, keepdims=True)
            a = a * lax.rsqrt(ms + NORM_EPS) * subg * (1.0 - lam_init)
            o_ref[:, h * LANES:(h + 1) * LANES] = a.T.astype(o_ref.dtype)

    if has_latent:
        pl.when(j == 0)(start)
        fold(k_ref, vT_ref)
        pl.when(j == last)(finish)
    else:
        start()
        finish()


def _diff_attention(qT, k, vT, kc, vcT, lam, subg, lam_init, *, has_latent):
    n = qT.shape[1]
    nc = kc.shape[0]
    w = DIFF_WIDTH
    small = [_resident((4, HEAD_DIM)), _resident((2 * HEAD_DIM, 1))]
    if has_latent:
        tq, tk = DIFF_TQ, DIFF_TK
        grid = (n // tq, k.shape[0] // tk)
        in_specs = [
            pl.BlockSpec((w, tq), lambda i, j: (0, i)),
            pl.BlockSpec((tk, w), lambda i, j: (j, 0)),
            pl.BlockSpec((w, tk), lambda i, j: (0, j)),
            pl.BlockSpec((nc, w), lambda i, j: (0, 0)),
            pl.BlockSpec((w, nc), lambda i, j: (0, 0)),
        ] + small
        args = (qT, k, vT, kc, vcT, lam, subg)
        out_spec = pl.BlockSpec((tq, w), lambda i, j: (i, 0))
        sem = ("parallel", "arbitrary")
    else:
        tq, tk = n, 0
        grid = (1,)
        in_specs = [
            pl.BlockSpec((w, tq), lambda i: (0, 0)),
            pl.BlockSpec((nc, w), lambda i: (0, 0)),
            pl.BlockSpec((w, nc), lambda i: (0, 0)),
        ] + small
        args = (qT, kc, vcT, lam, subg)
        out_spec = pl.BlockSpec((tq, w), lambda i: (0, 0))
        sem = ("arbitrary",)
    return pl.pallas_call(
        functools.partial(_diff_kernel, lam_init=lam_init, has_latent=has_latent),
        grid=grid,
        in_specs=in_specs,
        out_specs=out_spec,
        out_shape=jax.ShapeDtypeStruct((n, w), BF16),
        scratch_shapes=[
            pltpu.VMEM((2 * DIFF_HEADS, LANES, tq), BF16),
            pltpu.VMEM((2 * DIFF_HEADS, 1, tq), F32),
            pltpu.VMEM((2 * DIFF_HEADS, 1, tq), F32),
            pltpu.VMEM((2 * DIFF_HEADS, 1, tq), F32),
            pltpu.VMEM((2 * DIFF_HEADS, LANES, tq), F32),
            pltpu.VMEM((DIFF_DEPTH + 1, max(tk, nc), DIFF_QC), F32),
            pltpu.VMEM((2, max(tk, nc), DIFF_QC), BF16),
        ],
        compiler_params=_cparams(*sem),
        name="diff_attn" if has_latent else "diff_attn_ctx",
    )(*args)


def _na_bias_kernel(rpb_ref, o_ref):
    h = pl.program_id(0)
    n_dr, n_dc = 2 * NA_MAX_ROWS - 1, 2 * NA_COLS - 1
    kc = lax.broadcasted_iota(jnp.int32, (GRID_W, LANES), 0)
    lane = lax.broadcasted_iota(jnp.int32, (GRID_W, LANES), 1)
    c = lane % GRID_W
    upper = lane >= GRID_W
    dc = kc - c + (NA_COLS - 1)
    cs = jnp.clip(c - NA_COLS // 2, 0, GRID_W - NA_COLS)
    col_ok = (kc >= cs) & (kc < cs + NA_COLS)
    for i in range(o_ref.shape[1]):
        dr_idx = (i - 1, i - 2)
        val = jnp.full((GRID_W, LANES), NEG_INF, F32)
        for d in range(n_dc):
            lo, up = (rpb_ref[(h * n_dr + r) * n_dc + d] if 0 <= r < n_dr else NEG_INF for r in dr_idx)
            val = jnp.where(dc == d, jnp.where(upper, up, lo), val)
        o_ref[0, i] = jnp.where(col_ok, val, NEG_INF)


def _na_bias_blocks(rpb):
    n_heads = rpb.shape[0]
    n_blocks = 2 * NA_MAX_ROWS
    return pl.pallas_call(
        _na_bias_kernel,
        grid=(n_heads,),
        in_specs=[pl.BlockSpec(memory_space=pltpu.SMEM)],
        out_specs=pl.BlockSpec((1, n_blocks, GRID_W, LANES), lambda h: (h, 0, 0, 0)),
        out_shape=jax.ShapeDtypeStruct((n_heads, n_blocks, GRID_W, LANES), F32),
        compiler_params=_cparams("arbitrary"),
        name="na_bias",
    )(rpb.astype(F32).reshape(-1))


def _na_window_bias(bias_ref, e, t, grid_rows):
    lane = lax.broadcasted_iota(jnp.int32, (GRID_W, LANES), 1)
    r0 = t * NA_TILE_ROWS
    rows = []
    for rk in range(3 * NA_TILE_ROWS):
        kr = r0 - NA_TILE_ROWS + rk
        cols = []
        for p in range(NA_TILE_ROWS // 2):
            blk = bias_ref[e, rk - NA_TILE_ROWS - 2 * p + NA_MAX_ROWS]
            r = r0 + 2 * p + lane // GRID_W
            rs = jnp.clip(r - NA_MAX_ROWS // 2, 0, grid_rows - NA_MAX_ROWS)
            ok = (kr >= rs) & (kr < rs + NA_MAX_ROWS)
            cols.append(jnp.where(ok, blk, NEG_INF))
        rows.append(jnp.concatenate(cols, axis=1))
    return jnp.concatenate(rows, axis=0)


def _na_kernel(*refs, has_latent, grid_rows):
    if has_latent:
        qT_ref, km_ref, k0_ref, kp_ref, vm_ref, v0_ref, vp_ref, kc_ref, vcT_ref, bias_ref, o_ref = refs
    else:
        qT_ref, kc_ref, vcT_ref, o_ref = refs
    pair = qT_ref[...]
    kc = kc_ref[...]
    if has_latent:
        vT = jnp.concatenate([vm_ref[...], v0_ref[...], vp_ref[...], vcT_ref[...]], axis=1)
    else:
        vT = vcT_ref[...]
    outs = []
    for e in range(2):
        qm = _half_masked(pair, e)
        s = _dot(kc, qm)
        if has_latent:
            s_nb = jnp.concatenate([_dot(km_ref[...], qm), _dot(k0_ref[...], qm), _dot(kp_ref[...], qm)], axis=0)
            s = jnp.concatenate([s_nb + _na_window_bias(bias_ref, e, pl.program_id(1), grid_rows), s], axis=0)
        m = jnp.max(s, axis=0, keepdims=True)
        p = jnp.exp(s - m)
        l = jnp.sum(p, axis=0, keepdims=True)
        outs.append(_dot(vT[e * HEAD_DIM:(e + 1) * HEAD_DIM], p.astype(BF16)) / l)
    o_ref[...] = jnp.concatenate(outs, axis=0).T.astype(o_ref.dtype)


def _na_attention(qT_all, k_all, vT_all, kc_all, vcT_all, bias, *, has_latent):
    n = qT_all.shape[1]
    nc = kc_all.shape[0]
    g0 = DIFF_WIDTH // LANES
    n_pairs = NA_HEADS // 2
    if has_latent:
        tq = NA_TQ
        nt = n // tq
        prev = lambda t: jnp.maximum(t - 1, 0)
        nxt = lambda t: jnp.minimum(t + 1, nt - 1)
        in_specs = [
            pl.BlockSpec((LANES, tq), lambda hp, t: (g0 + hp, t)),
            pl.BlockSpec((tq, LANES), lambda hp, t: (prev(t), g0 + hp)),
            pl.BlockSpec((tq, LANES), lambda hp, t: (t, g0 + hp)),
            pl.BlockSpec((tq, LANES), lambda hp, t: (nxt(t), g0 + hp)),
            pl.BlockSpec((LANES, tq), lambda hp, t: (g0 + hp, prev(t))),
            pl.BlockSpec((LANES, tq), lambda hp, t: (g0 + hp, t)),
            pl.BlockSpec((LANES, tq), lambda hp, t: (g0 + hp, nxt(t))),
            pl.BlockSpec((nc, LANES), lambda hp, t: (0, g0 + hp)),
            pl.BlockSpec((LANES, nc), lambda hp, t: (g0 + hp, 0)),
            pl.BlockSpec((2,) + bias.shape[1:], lambda hp, t: (hp, 0, 0, 0)),
        ]
        args = (qT_all, k_all, k_all, k_all, vT_all, vT_all, vT_all, kc_all, vcT_all, bias)
    else:
        tq = n
        nt = 1
        in_specs = [
            pl.BlockSpec((LANES, tq), lambda hp, t: (g0 + hp, 0)),
            pl.BlockSpec((nc, LANES), lambda hp, t: (0, g0 + hp)),
            pl.BlockSpec((LANES, nc), lambda hp, t: (g0 + hp, 0)),
        ]
        args = (qT_all, kc_all, vcT_all)
    return pl.pallas_call(
        functools.partial(_na_kernel, has_latent=has_latent, grid_rows=n // GRID_W),
        grid=(n_pairs, nt),
        in_specs=in_specs,
        out_specs=pl.BlockSpec((tq, LANES), lambda hp, t: (t, hp)),
        out_shape=jax.ShapeDtypeStruct((n, NA_WIDTH), BF16),
        compiler_params=_cparams("arbitrary", "arbitrary"),
        name="na_attn" if has_latent else "na_attn_ctx",
    )(*args)


def _swa_kernel(*refs, has_latent):
    if has_latent:
        qT_ref, km_ref, k0_ref, kp_ref, vm_ref, v0_ref, vp_ref, kc_ref, vcT_ref, sink_ref, o_ref = refs
        b = pl.program_id(0)
        nb = pl.num_programs(0)
        tq = qT_ref.shape[1]
        ki = lax.broadcasted_iota(jnp.int32, (tq, tq), 0)
        qi = lax.broadcasted_iota(jnp.int32, (tq, tq), 1)
        band_prev = jnp.concatenate([(ki >= qi) & (b > 0)] * GQA_GROUP, axis=1)
        band_next = jnp.concatenate([(ki <= qi) & (b < nb - 1)] * GQA_GROUP, axis=1)
    else:
        qT_ref, kc_ref, vcT_ref, sink_ref, o_ref = refs
        tq = qT_ref.shape[1]
    zeros = jnp.zeros((HEAD_DIM, tq), BF16)
    outs = []
    for g in range(GQA_KV_HEADS):
        grp, half = g // 2, g % 2
        lanes = slice(grp * LANES, (grp + 1) * LANES)
        rows = slice(g * HEAD_DIM, (g + 1) * HEAD_DIM)
        q_cols = []
        for jh in range(GQA_GROUP):
            hq = g * GQA_GROUP + jh
            qh = qT_ref[hq * HEAD_DIM:(hq + 1) * HEAD_DIM, :]
            q_cols.append(jnp.concatenate([qh, zeros] if half == 0 else [zeros, qh], axis=0))
        qm = jnp.concatenate(q_cols, axis=1)
        s = _dot(kc_ref[:, lanes], qm)
        vT = vcT_ref[rows, :]
        if has_latent:
            s_prev = jnp.where(band_prev, _dot(km_ref[:, lanes], qm), NEG_INF)
            s_next = jnp.where(band_next, _dot(kp_ref[:, lanes], qm), NEG_INF)
            s = jnp.concatenate([s, s_prev, _dot(k0_ref[:, lanes], qm), s_next], axis=0)
            vT = jnp.concatenate([vT, vm_ref[rows, :], v0_ref[rows, :], vp_ref[rows, :]], axis=1)
        sink = sink_ref[g]
        m = jnp.maximum(jnp.max(s, axis=0, keepdims=True), sink)
        p = jnp.exp(s - m)
        l = jnp.sum(p, axis=0, keepdims=True) + jnp.exp(sink - m)
        o = _dot(vT, p.astype(BF16)) / l
        outs += [o[:, jh * tq:(jh + 1) * tq] for jh in range(GQA_GROUP)]
    o_ref[...] = jnp.concatenate(outs, axis=0).T.astype(o_ref.dtype)


def _swa_attention(qT, k, vT, kc, vcT, sinks, *, has_latent):
    n = qT.shape[1]
    nc = kc.shape[0]
    tq = SWA_TQ
    nb = n // tq
    sink_rows = jnp.repeat(sinks.astype(F32).reshape(GQA_KV_HEADS, 1, GQA_GROUP), tq, axis=2)
    ctx_specs = [_resident((nc, C_KV_COLS)), _resident((C_KV_COLS, nc)), _resident(sink_rows.shape)]
    if has_latent:
        prev = lambda b: jnp.maximum(b - 1, 0)
        nxt = lambda b: jnp.minimum(b + 1, nb - 1)
        in_specs = [
            pl.BlockSpec((C_Q_COLS, tq), lambda b: (0, b)),
            pl.BlockSpec((tq, C_KV_COLS), lambda b: (prev(b), 0)),
            pl.BlockSpec((tq, C_KV_COLS), lambda b: (b, 0)),
            pl.BlockSpec((tq, C_KV_COLS), lambda b: (nxt(b), 0)),
            pl.BlockSpec((C_KV_COLS, tq), lambda b: (0, prev(b))),
            pl.BlockSpec((C_KV_COLS, tq), lambda b: (0, b)),
            pl.BlockSpec((C_KV_COLS, tq), lambda b: (0, nxt(b))),
        ] + ctx_specs
        args = (qT, k, k, k, vT, vT, vT, kc, vcT, sink_rows)
    else:
        in_specs = [pl.BlockSpec((C_Q_COLS, tq), lambda b: (0, b))] + ctx_specs
        args = (qT, kc, vcT, sink_rows)
    return pl.pallas_call(
        functools.partial(_swa_kernel, has_latent=has_latent),
        grid=(nb,),
        in_specs=in_specs,
        out_specs=pl.BlockSpec((tq, C_Q_COLS), lambda b: (b, 0)),
        out_shape=jax.ShapeDtypeStruct((n, C_Q_COLS), BF16),
        compiler_params=_cparams("arbitrary"),
        name="swa_attn" if has_latent else "swa_attn_ctx",
    )(*args)


def _out_ffn_kernel(*refs, n_attn, final):
    x_ref = refs[0]
    attn_refs = refs[1:1 + n_attn]
    wout_ref, mod_ref, g_ref, w1_ref, w3_ref, w2_ref = refs[1 + n_attn:7 + n_attn]
    rest = refs[7 + n_attn:]
    if final:
        fg_ref, o_ref = rest
    else:
        (o_ref,) = rest
    proj = None
    r0 = 0
    for a_ref in attn_refs:
        wdt = a_ref.shape[1]
        part = _dot(a_ref[...], wout_ref[r0:r0 + wdt, :])
        proj = part if proj is None else proj + part
        r0 += wdt
    x1 = x_ref[...] + mod_ref[2:3, :] * proj
    h = _norm_mod(x1, g_ref[...], mod_ref[3:4, :], mod_ref[4:5, :]).astype(BF16)
    hidden = w1_ref.shape[1]
    chunk = hidden // 2
    ffn = None
    for c0 in range(0, hidden, chunk):
        a = _dot(h, w1_ref[:, c0:c0 + chunk])
        b = _dot(h, w3_ref[:, c0:c0 + chunk])
        gated = (a * jax.nn.sigmoid(a) * b).astype(BF16)
        part = _dot(gated, w2_ref[c0:c0 + chunk, :])
        ffn = part if ffn is None else ffn + part
    x2 = x1 + mod_ref[5:6, :] * ffn
    if final:
        ms = jnp.mean(x2 * x2, axis=-1, keepdims=True)
        x2 = x2 * lax.rsqrt(ms + NORM_EPS) * fg_ref[...]
    o_ref[...] = x2


def _out_ffn(x, attn, w_out, mod, g, w1, w3, w2, final_g=None):
    n, d = x.shape
    tm = min(ROW_TILE, n)
    row = lambda i: (i, 0)
    final = final_g is not None
    in_specs = [pl.BlockSpec((tm, d), row)]
    in_specs += [pl.BlockSpec((tm, a.shape[1]), row) for a in attn]
    in_specs += [_resident(w_out.shape), _resident(mod.shape), _resident((1, d)),
                 _resident(w1.shape), _resident(w3.shape), _resident(w2.shape)]
    args = [x, *attn, w_out, mod, g, w1, w3, w2]
    if final:
        in_specs.append(_resident((1, d)))
        args.append(final_g)
    return pl.pallas_call(
        functools.partial(_out_ffn_kernel, n_attn=len(attn), final=final),
        grid=(n // tm,),
        in_specs=in_specs,
        out_specs=pl.BlockSpec((tm, d), row),
        out_shape=jax.ShapeDtypeStruct((n, d), F32),
        compiler_params=_cparams("arbitrary"),
        name="out_ffn",
    )(*args)


def _rope_tables(n):
    t = jnp.arange(n)
    row = (t // GRID_W).astype(F32)
    col = (t % GRID_W).astype(F32)
    quarter = HEAD_DIM // 4
    inv = ROPE_BASE ** (-jnp.arange(quarter, dtype=F32) / quarter)
    ar = row[:, None] * inv
    ac = col[:, None] * inv
    ang = jnp.concatenate([ar, ar, ac, ac], axis=-1)
    sign = jnp.tile(jnp.concatenate([-jnp.ones(quarter, F32), jnp.ones(quarter, F32)]), 2)
    cos = jnp.tile(jnp.cos(ang), (1, LANES // HEAD_DIM))
    sin = jnp.tile(jnp.sin(ang) * sign, (1, LANES // HEAD_DIM))
    return cos, sin


def _trunk(x, ctx, mods, norm_g, w_in_ab, w_out_ab, diff_lambda, diff_sub_g, na_rpb, w_in_c, w_out_c,
           attn_sinks, ffn_w1, ffn_w3, ffn_w2, final_g):
    depth = mods.shape[0]
    s_len = x.shape[0]
    tables = _rope_tables(s_len)
    for layer in range(depth):
        ctx_out = layer < depth - 1
        mod_x, mod_c = mods[layer, 0], mods[layer, 1]
        g0 = norm_g[layer, 0].reshape(1, -1)
        g1 = norm_g[layer, 1].reshape(1, -1)
        if layer % 2 == 0:
            e = layer // 2
            lam_init = 0.8 - 0.6 * math.exp(-0.3 * layer)
            w = w_in_ab[e]
            q_cols = DIFF_WIDTH + NA_WIDTH
            a_k, a_v, b_k, b_v = (w[:, q_cols + i * DIFF_WIDTH:q_cols + (i + 1) * DIFF_WIDTH] for i in range(4))
            w = jnp.concatenate([w[:, :q_cols], a_k, b_k, a_v, b_v], axis=1).astype(BF16)
            dims = dict(n_rope_q=DIFF_WIDTH, n_q=q_cols, n_rope_k=DIFF_WIDTH, n_k=q_cols)
            qT, k, vT = _proj_in(x, g0, mod_x, w, tables, **dims)
            qcT, kc, vcT = _proj_in(ctx, g0, mod_c, w, None, **dims)
            lam = diff_lambda[e].astype(F32)
            subg = diff_sub_g[e].astype(F32).reshape(-1, 1)
            bias = _na_bias_blocks(na_rpb[e])
            a_x = _diff_attention(qT, k, vT, kc, vcT, lam, subg, lam_init, has_latent=True)
            b_x = _na_attention(qT, k, vT, kc, vcT, bias, has_latent=True)
            attn_x = [a_x, b_x]
            if ctx_out:
                a_c = _diff_attention(qcT, None, None, kc, vcT, lam, subg, lam_init, has_latent=False)
                b_c = _na_attention(qcT, None, None, kc, vcT, None, has_latent=False)
                attn_c = [a_c, b_c]
            w_out = w_out_ab[e].astype(BF16)
        else:
            o = layer // 2
            w = w_in_c[o].astype(BF16)
            dims = dict(n_rope_q=C_Q_COLS, n_q=C_Q_COLS, n_rope_k=C_KV_COLS, n_k=C_KV_COLS)
            qT, k, vT = _proj_in(x, g0, mod_x, w, tables, **dims)
            qcT, kc, vcT = _proj_in(ctx, g0, mod_c, w, None, **dims)
            attn_x = [_swa_attention(qT, k, vT, kc, vcT, attn_sinks[o], has_latent=True)]
            if ctx_out:
                attn_c = [_swa_attention(qcT, None, None, kc, vcT, attn_sinks[o], has_latent=False)]
            w_out = w_out_c[o].astype(BF16)
        w1, w3, w2 = ffn_w1[layer].astype(BF16), ffn_w3[layer].astype(BF16), ffn_w2[layer].astype(BF16)
        fg = final_g.reshape(1, -1) if layer == depth - 1 else None
        x = _out_ffn(x, attn_x, w_out, mod_x, g1, w1, w3, w2, fg)
        if ctx_out:
            ctx = _out_ffn(ctx, attn_c, w_out, mod_c, g1, w1, w3, w2)
    return x


def kernel(x, c, ctx, c_ctx, ada_w, ada_b, norm_g, w_in_ab, w_out_ab, diff_lambda, diff_sub_g, na_rpb, w_in_c,
           w_out_c, attn_sinks, ffn_w1, ffn_w3, ffn_w2, final_g):
    batch, _, d = x.shape
    depth = ada_w.shape[0]
    outs = []
    for b in range(batch):
        cc = jnp.zeros((8, d), F32).at[0].set(c[b].astype(F32)).at[1].set(c_ctx.astype(F32))
        mods = _adaln(cc, ada_w, ada_b)[:, :2].reshape(depth, 2, 6, d)
        outs.append(_trunk(x[b], ctx[b], mods, norm_g, w_in_ab, w_out_ab, diff_lambda, diff_sub_g, na_rpb,
                           w_in_c, w_out_c, attn_sinks, ffn_w1, ffn_w3, ffn_w2, final_g))
    return jnp.stack(outs).astype(x.dtype)
```

```python
import functools
import math

import jax
import jax.numpy as jnp
from jax import lax
from jax.experimental import pallas as pl
from jax.experimental.pallas import tpu as pltpu

F32 = jnp.float32
BF16 = jnp.bfloat16

D_MODEL = 1024
GRID_W = 64
HEAD_DIM = 64
ROPE_BASE = 10000.0
NORM_EPS = 1e-6
NEG_INF = -1e30
DIFF_HEADS = 4
DIFF_WIDTH = DIFF_HEADS * 2 * HEAD_DIM
NA_HEADS = 8
NA_WIDTH = NA_HEADS * HEAD_DIM
NA_MAX_ROWS = 8
NA_COLS = 16
GQA_HEADS = 16
GQA_KV_HEADS = 4
GQA_GROUP = GQA_HEADS // GQA_KV_HEADS
SWA_WINDOW = 128
C_Q_COLS = GQA_HEADS * HEAD_DIM
C_KV_COLS = GQA_KV_HEADS * HEAD_DIM
Q_SCALE = HEAD_DIM ** -0.5

LANES = 128
ONES_ROWS = 16
VMEM_LIMIT = 56 * 1024 * 1024

ROW_TILE = 512
DIFF_TQ = 512
DIFF_TK = 512
DIFF_QC = 256
DIFF_RC = 64
DIFF_DEPTH = 4
NA_TILE_ROWS = 4
NA_TQ = NA_TILE_ROWS * GRID_W
SWA_TQ = SWA_WINDOW
SWA_ITEM_HEADS = 2
SWA_DEPTH = 2


def _cparams(*sem):
    return pltpu.CompilerParams(dimension_semantics=sem, vmem_limit_bytes=VMEM_LIMIT)


def _resident(shape):
    nd = len(shape)
    return pl.BlockSpec(shape, lambda *_: (0,) * nd, pipeline_mode=pl.Buffered(1))


def _dot(a, b):
    return jnp.dot(a, b, preferred_element_type=F32)


def _adaln_kernel(cc_ref, w_ref, b_ref, o_ref):
    cc = cc_ref[...]
    h = (cc * jax.nn.sigmoid(cc)).astype(BF16)
    o_ref[0] = _dot(h, w_ref[0].astype(BF16)) + b_ref[0]


def _adaln(cc, ada_w, ada_b):
    depth, d, d6 = ada_w.shape
    nj = d6 // d
    return pl.pallas_call(
        _adaln_kernel,
        grid=(depth, nj),
        in_specs=[
            pl.BlockSpec((8, d), lambda l, j: (0, 0)),
            pl.BlockSpec((1, d, d), lambda l, j: (l, 0, j)),
            pl.BlockSpec((1, 1, d), lambda l, j: (l, 0, j)),
        ],
        out_specs=pl.BlockSpec((1, 8, d), lambda l, j: (l, 0, j)),
        out_shape=jax.ShapeDtypeStruct((depth, 8, d6), F32),
        compiler_params=_cparams("arbitrary", "arbitrary"),
        name="adaln",
    )(cc, ada_w, ada_b.reshape(depth, 1, d6))


def _norm_mod(x, g, shift, scale):
    ms = jnp.mean(x * x, axis=-1, keepdims=True)
    h = x * lax.rsqrt(ms + NORM_EPS) * g
    return h * (1.0 + scale) + shift


def _rope_lanes(t, cos, sin_signed):
    lane = lax.broadcasted_iota(jnp.int32, (t.shape[0], LANES), 1)
    first = (lane % (HEAD_DIM // 2)) < (HEAD_DIM // 4)
    out = []
    for c in range(t.shape[1] // LANES):
        tc = t[:, c * LANES:(c + 1) * LANES]
        partner = jnp.where(first, pltpu.roll(tc, LANES - HEAD_DIM // 4, 1), pltpu.roll(tc, HEAD_DIM // 4, 1))
        out.append(tc * cos + partner * sin_signed)
    return jnp.concatenate(out, axis=1)


def _proj_in_kernel(*refs, n_rope_q, n_q, n_rope_k, n_k, rope):
    if rope:
        x_ref, g_ref, mod_ref, cos_ref, sin_ref, w_ref, qT_ref, k_ref, vT_ref = refs
    else:
        x_ref, g_ref, mod_ref, w_ref, qT_ref, k_ref, vT_ref = refs
    h = _norm_mod(x_ref[...], g_ref[...], mod_ref[0:1, :], mod_ref[1:2, :])
    y = _dot(h.astype(BF16), w_ref[...])
    q = y[:, :n_q]
    k = y[:, n_q:n_q + n_k]
    v = y[:, n_q + n_k:]
    if rope:
        cos, sin = cos_ref[...], sin_ref[...]
        q = jnp.concatenate([_rope_lanes(q[:, :n_rope_q], cos, sin), q[:, n_rope_q:]], axis=1) \
            if n_rope_q < n_q else _rope_lanes(q, cos, sin)
        k = jnp.concatenate([_rope_lanes(k[:, :n_rope_k], cos, sin), k[:, n_rope_k:]], axis=1) \
            if n_rope_k < n_k else _rope_lanes(k, cos, sin)
    qT_ref[...] = (q * Q_SCALE).T.astype(BF16)
    k_ref[...] = k.astype(BF16)
    vT_ref[...] = v.T.astype(BF16)


def _proj_in(x, g, mod, w, tables, *, n_rope_q, n_q, n_rope_k, n_k):
    n, d = x.shape
    cols = w.shape[1]
    n_v = cols - n_q - n_k
    tm = min(ROW_TILE, n)
    rope = tables is not None
    row = lambda i: (i, 0)
    in_specs = [pl.BlockSpec((tm, d), row), _resident((1, d)), _resident(mod.shape)]
    args = [x, g, mod]
    if rope:
        in_specs += [pl.BlockSpec((tm, LANES), row), pl.BlockSpec((tm, LANES), row)]
        args += list(tables)
    in_specs.append(_resident(w.shape))
    args.append(w)
    return pl.pallas_call(
        functools.partial(_proj_in_kernel, n_rope_q=n_rope_q, n_q=n_q, n_rope_k=n_rope_k, n_k=n_k, rope=rope),
        grid=(n // tm,),
        in_specs=in_specs,
        out_specs=[
            pl.BlockSpec((n_q, tm), lambda i: (0, i)),
            pl.BlockSpec((tm, n_k), row),
            pl.BlockSpec((n_v, tm), lambda i: (0, i)),
        ],
        out_shape=[
            jax.ShapeDtypeStruct((n_q, n), BF16),
            jax.ShapeDtypeStruct((n, n_k), BF16),
            jax.ShapeDtypeStruct((n_v, n), BF16),
        ],
        compiler_params=_cparams("arbitrary"),
        name="proj_in",
    )(*args)


def _half_masked(qT_pair, e):
    z = jnp.zeros((HEAD_DIM, qT_pair.shape[1]), qT_pair.dtype)
    if e == 0:
        return jnp.concatenate([qT_pair[:HEAD_DIM], z], axis=0)
    return jnp.concatenate([z, qT_pair[HEAD_DIM:]], axis=0)


def _diff_kernel(*refs, lam_init, has_latent):
    if has_latent:
        qT_ref, k_ref, vT_ref, kc_ref, vcT_ref, lam_ref, subg_ref, o_ref = refs[:8]
        qm_ref, m_ref, alpha_ref, l_ref, acc_ref, s_ref, p_ref = refs[8:]
        j = pl.program_id(1)
        last = pl.num_programs(1) - 1
    else:
        qT_ref, kc_ref, vcT_ref, lam_ref, subg_ref, o_ref = refs[:6]
        qm_ref, m_ref, alpha_ref, l_ref, acc_ref, s_ref, p_ref = refs[6:]

    tq = qm_ref.shape[2]
    items = [(h, e, c) for h in range(DIFF_HEADS) for e in range(2) for c in range(0, tq, DIFF_QC)]

    def start():
        for h in range(DIFF_HEADS):
            pair = qT_ref[h * LANES:(h + 1) * LANES, :]
            for e in range(2):
                qm_ref[2 * h + e] = _half_masked(pair, e)
        m_ref[...] = jnp.full(m_ref.shape, NEG_INF, F32)
        l_ref[...] = jnp.zeros(l_ref.shape, F32)
        acc_ref[...] = jnp.zeros(acc_ref.shape, F32)
        fold(kc_ref, vcT_ref)

    def fold(keys_ref, valsT_ref):
        nk = keys_ref.shape[0]
        sub = 8

        def scores(i):
            h, e, c = items[i]
            mp = 2 * h + e
            s = _dot(keys_ref[:, h * LANES:(h + 1) * LANES], qm_ref[mp, :, c:c + DIFF_QC])
            s_ref[i % (DIFF_DEPTH + 1), :nk, :] = s
            m_part = jnp.max(s.reshape(nk // sub, sub, DIFF_QC), axis=0)
            m_old = m_ref[mp, :, c:c + DIFF_QC]
            m_new = jnp.maximum(m_old, jnp.max(m_part, axis=0, keepdims=True))
            m_ref[mp, :, c:c + DIFF_QC] = m_new
            alpha_ref[mp, :, c:c + DIFF_QC] = jnp.exp(m_old - m_new)

        def absorb(i):
            h, e, c = items[i]
            mp = 2 * h + e
            cols = slice(c, c + DIFF_QC)
            m_new = m_ref[mp, :, cols]
            alpha = alpha_ref[mp, :, cols]
            for r in range(0, nk, DIFF_RC):
                p = jnp.exp(s_ref[i % (DIFF_DEPTH + 1), r:r + DIFF_RC, :] - m_new)
                p_ref[i % 2, r:r + DIFF_RC, :] = p.astype(BF16)
            lhs = jnp.concatenate([valsT_ref[h * LANES:(h + 1) * LANES, :], jnp.ones((ONES_ROWS, nk), BF16)], axis=0)
            pv = _dot(lhs, p_ref[i % 2, :nk, :])
            l_ref[mp, :, cols] = alpha * l_ref[mp, :, cols] + pv[LANES:LANES + 1]
            acc_ref[mp, :, cols] = alpha * acc_ref[mp, :, cols] + pv[:LANES]

        for i in range(min(DIFF_DEPTH, len(items))):
            scores(i)
        for i in range(len(items)):
            if i + DIFF_DEPTH < len(items):
                scores(i + DIFF_DEPTH)
            absorb(i)

    def finish():
        lam = lam_ref[...]
        lam_val = (jnp.exp(jnp.sum(lam[0:1] * lam[1:2], axis=1, keepdims=True))
                   - jnp.exp(jnp.sum(lam[2:3] * lam[3:4], axis=1, keepdims=True)) + lam_init)
        subg = subg_ref[...]
        for h in range(DIFF_HEADS):
            a0 = acc_ref[2 * h] / l_ref[2 * h]
            a1 = acc_ref[2 * h + 1] / l_ref[2 * h + 1]
            a = a0 - lam_val * a1
            ms = jnp.mean(a * a, axis=0, keepdims=True)
            a = a * lax.rsqrt(ms + NORM_EPS) * subg * (1.0 - lam_init)
            o_ref[:, h * LANES:(h + 1) * LANES] = a.T.astype(o_ref.dtype)

    if has_latent:
        pl.when(j == 0)(start)
        fold(k_ref, vT_ref)
        pl.when(j == last)(finish)
    else:
        start()
        finish()


def _diff_attention(qT, k, vT, kc, vcT, lam, subg, lam_init, *, has_latent):
    n = qT.shape[1]
    nc = kc.shape[0]
    w = DIFF_WIDTH
    small = [_resident((4, HEAD_DIM)), _resident((2 * HEAD_DIM, 1))]
    if has_latent:
        tq, tk = DIFF_TQ, DIFF_TK
        grid = (n // tq, k.shape[0] // tk)
        in_specs = [
            pl.BlockSpec((w, tq), lambda i, j: (0, i)),
            pl.BlockSpec((tk, w), lambda i, j: (j, 0)),
            pl.BlockSpec((w, tk), lambda i, j: (0, j)),
            pl.BlockSpec((nc, w), lambda i, j: (0, 0)),
            pl.BlockSpec((w, nc), lambda i, j: (0, 0)),
        ] + small
        args = (qT, k, vT, kc, vcT, lam, subg)
        out_spec = pl.BlockSpec((tq, w), lambda i, j: (i, 0))
        sem = ("parallel", "arbitrary")
    else:
        tq, tk = n, 0
        grid = (1,)
        in_specs = [
            pl.BlockSpec((w, tq), lambda i: (0, 0)),
            pl.BlockSpec((nc, w), lambda i: (0, 0)),
            pl.BlockSpec((w, nc), lambda i: (0, 0)),
        ] + small
        args = (qT, kc, vcT, lam, subg)
        out_spec = pl.BlockSpec((tq, w), lambda i: (0, 0))
        sem = ("arbitrary",)
    return pl.pallas_call(
        functools.partial(_diff_kernel, lam_init=lam_init, has_latent=has_latent),
        grid=grid,
        in_specs=in_specs,
        out_specs=out_spec,
        out_shape=jax.ShapeDtypeStruct((n, w), BF16),
        scratch_shapes=[
            pltpu.VMEM((2 * DIFF_HEADS, LANES, tq), BF16),
            pltpu.VMEM((2 * DIFF_HEADS, 1, tq), F32),
            pltpu.VMEM((2 * DIFF_HEADS, 1, tq), F32),
            pltpu.VMEM((2 * DIFF_HEADS, 1, tq), F32),
            pltpu.VMEM((2 * DIFF_HEADS, LANES, tq), F32),
            pltpu.VMEM((DIFF_DEPTH + 1, max(tk, nc), DIFF_QC), F32),
            pltpu.VMEM((2, max(tk, nc), DIFF_QC), BF16),
        ],
        compiler_params=_cparams(*sem),
        name="diff_attn" if has_latent else "diff_attn_ctx",
    )(*args)


def _na_bias_kernel(rpb_ref, o_ref):
    h = pl.program_id(0)
    n_dr, n_dc = 2 * NA_MAX_ROWS - 1, 2 * NA_COLS - 1
    kc = lax.broadcasted_iota(jnp.int32, (GRID_W, LANES), 0)
    lane = lax.broadcasted_iota(jnp.int32, (GRID_W, LANES), 1)
    c = lane % GRID_W
    upper = lane >= GRID_W
    dc = kc - c + (NA_COLS - 1)
    cs = jnp.clip(c - NA_COLS // 2, 0, GRID_W - NA_COLS)
    col_ok = (kc >= cs) & (kc < cs + NA_COLS)
    for i in range(o_ref.shape[1]):
        dr_idx = (i - 1, i - 2)
        val = jnp.full((GRID_W, LANES), NEG_INF, F32)
        for d in range(n_dc):
            lo, up = (rpb_ref[(h * n_dr + r) * n_dc + d] if 0 <= r < n_dr else NEG_INF for r in dr_idx)
            val = jnp.where(dc == d, jnp.where(upper, up, lo), val)
        o_ref[0, i] = jnp.where(col_ok, val, NEG_INF)


def _na_bias_blocks(rpb):
    n_heads = rpb.shape[0]
    n_blocks = 2 * NA_MAX_ROWS
    return pl.pallas_call(
        _na_bias_kernel,
        grid=(n_heads,),
        in_specs=[pl.BlockSpec(memory_space=pltpu.SMEM)],
        out_specs=pl.BlockSpec((1, n_blocks, GRID_W, LANES), lambda h: (h, 0, 0, 0)),
        out_shape=jax.ShapeDtypeStruct((n_heads, n_blocks, GRID_W, LANES), F32),
        compiler_params=_cparams("arbitrary"),
        name="na_bias",
    )(rpb.astype(F32).reshape(-1))


def _na_window_bias(bias_ref, e, t, grid_rows):
    lane = lax.broadcasted_iota(jnp.int32, (GRID_W, LANES), 1)
    r0 = t * NA_TILE_ROWS
    rows = []
    for rk in range(3 * NA_TILE_ROWS):
        kr = r0 - NA_TILE_ROWS + rk
        cols = []
        for p in range(NA_TILE_ROWS // 2):
            blk = bias_ref[e, rk - NA_TILE_ROWS - 2 * p + NA_MAX_ROWS]
            r = r0 + 2 * p + lane // GRID_W
            rs = jnp.clip(r - NA_MAX_ROWS // 2, 0, grid_rows - NA_MAX_ROWS)
            ok = (kr >= rs) & (kr < rs + NA_MAX_ROWS)
            cols.append(jnp.where(ok, blk, NEG_INF))
        rows.append(jnp.concatenate(cols, axis=1))
    return jnp.concatenate(rows, axis=0)


def _na_kernel(*refs, has_latent, grid_rows):
    if has_latent:
        qT_ref, km_ref, k0_ref, kp_ref, vm_ref, v0_ref, vp_ref, kc_ref, vcT_ref, bias_ref, o_ref = refs
    else:
        qT_ref, kc_ref, vcT_ref, o_ref = refs
    pair = qT_ref[...]
    kc = kc_ref[...]
    if has_latent:
        vT = jnp.concatenate([vm_ref[...], v0_ref[...], vp_ref[...], vcT_ref[...]], axis=1)
    else:
        vT = vcT_ref[...]
    ones = jnp.ones((ONES_ROWS, vT.shape[1]), BF16)
    scores = []
    for e in range(2):
        qm = _half_masked(pair, e)
        s = _dot(kc, qm)
        if has_latent:
            s_nb = jnp.concatenate([_dot(km_ref[...], qm), _dot(k0_ref[...], qm), _dot(kp_ref[...], qm)], axis=0)
            s = jnp.concatenate([s_nb + _na_window_bias(bias_ref, e, pl.program_id(1), grid_rows), s], axis=0)
        scores.append(s)
    outs = []
    for e in range(2):
        s = scores[e]
        p = jnp.exp(s - jnp.max(s, axis=0, keepdims=True))
        pv = _dot(jnp.concatenate([vT[e * HEAD_DIM:(e + 1) * HEAD_DIM], ones], axis=0), p.astype(BF16))
        outs.append(pv[:HEAD_DIM] / pv[HEAD_DIM:HEAD_DIM + 1])
    o_ref[...] = jnp.concatenate(outs, axis=0).T.astype(o_ref.dtype)


def _na_attention(qT_all, k_all, vT_all, kc_all, vcT_all, bias, *, has_latent):
    n = qT_all.shape[1]
    nc = kc_all.shape[0]
    g0 = DIFF_WIDTH // LANES
    n_pairs = NA_HEADS // 2
    if has_latent:
        tq = NA_TQ
        nt = n // tq
        prev = lambda t: jnp.maximum(t - 1, 0)
        nxt = lambda t: jnp.minimum(t + 1, nt - 1)
        in_specs = [
            pl.BlockSpec((LANES, tq), lambda hp, t: (g0 + hp, t)),
            pl.BlockSpec((tq, LANES), lambda hp, t: (prev(t), g0 + hp)),
            pl.BlockSpec((tq, LANES), lambda hp, t: (t, g0 + hp)),
            pl.BlockSpec((tq, LANES), lambda hp, t: (nxt(t), g0 + hp)),
            pl.BlockSpec((LANES, tq), lambda hp, t: (g0 + hp, prev(t))),
            pl.BlockSpec((LANES, tq), lambda hp, t: (g0 + hp, t)),
            pl.BlockSpec((LANES, tq), lambda hp, t: (g0 + hp, nxt(t))),
            pl.BlockSpec((nc, LANES), lambda hp, t: (0, g0 + hp)),
            pl.BlockSpec((LANES, nc), lambda hp, t: (g0 + hp, 0)),
            pl.BlockSpec((2,) + bias.shape[1:], lambda hp, t: (hp, 0, 0, 0)),
        ]
        args = (qT_all, k_all, k_all, k_all, vT_all, vT_all, vT_all, kc_all, vcT_all, bias)
    else:
        tq = n
        nt = 1
        in_specs = [
            pl.BlockSpec((LANES, tq), lambda hp, t: (g0 + hp, 0)),
            pl.BlockSpec((nc, LANES), lambda hp, t: (0, g0 + hp)),
            pl.BlockSpec((LANES, nc), lambda hp, t: (g0 + hp, 0)),
        ]
        args = (qT_all, kc_all, vcT_all)
    return pl.pallas_call(
        functools.partial(_na_kernel, has_latent=has_latent, grid_rows=n // GRID_W),
        grid=(n_pairs, nt),
        in_specs=in_specs,
        out_specs=pl.BlockSpec((tq, LANES), lambda hp, t: (t, hp)),
        out_shape=jax.ShapeDtypeStruct((n, NA_WIDTH), BF16),
        compiler_params=_cparams("arbitrary", "arbitrary"),
        name="na_attn" if has_latent else "na_attn_ctx",
    )(*args)


def _swa_kernel(*refs, has_latent):
    if has_latent:
        qT_ref, km_ref, k0_ref, kp_ref, vm_ref, v0_ref, vp_ref, kc_ref, vcT_ref, sink_ref, o_ref = refs
        b = pl.program_id(0)
        nb = pl.num_programs(0)
        tq = qT_ref.shape[1]
        ki = lax.broadcasted_iota(jnp.int32, (tq, tq), 0)
        qi = lax.broadcasted_iota(jnp.int32, (tq, tq), 1)
        band_prev = jnp.concatenate([(ki >= qi) & (b > 0)] * SWA_ITEM_HEADS, axis=1)
        band_next = jnp.concatenate([(ki <= qi) & (b < nb - 1)] * SWA_ITEM_HEADS, axis=1)
    else:
        qT_ref, kc_ref, vcT_ref, sink_ref, o_ref = refs
        tq = qT_ref.shape[1]
    zeros = jnp.zeros((HEAD_DIM, tq), BF16)
    width = SWA_ITEM_HEADS * tq
    items = [(g, j0) for g in range(GQA_KV_HEADS) for j0 in range(0, GQA_GROUP, SWA_ITEM_HEADS)]

    def scores(item):
        g, j0 = item
        lanes = slice((g // 2) * LANES, (g // 2 + 1) * LANES)
        q_cols = []
        for jh in range(j0, j0 + SWA_ITEM_HEADS):
            hq = g * GQA_GROUP + jh
            qh = qT_ref[hq * HEAD_DIM:(hq + 1) * HEAD_DIM, :]
            q_cols.append(jnp.concatenate([qh, zeros] if g % 2 == 0 else [zeros, qh], axis=0))
        qm = jnp.concatenate(q_cols, axis=1)
        s = _dot(kc_ref[:, lanes], qm)
        if has_latent:
            s_prev = jnp.where(band_prev, _dot(km_ref[:, lanes], qm), NEG_INF)
            s_next = jnp.where(band_next, _dot(kp_ref[:, lanes], qm), NEG_INF)
            s = jnp.concatenate([s, s_prev, _dot(k0_ref[:, lanes], qm), s_next], axis=0)
        return s

    def attend(item, s):
        g, j0 = item
        rows = slice(g * HEAD_DIM, (g + 1) * HEAD_DIM)
        vT = vcT_ref[rows, :]
        if has_latent:
            vT = jnp.concatenate([vT, vm_ref[rows, :], v0_ref[rows, :], vp_ref[rows, :]], axis=1)
        sink = sink_ref[g][:, j0 * tq:j0 * tq + width]
        m = jnp.maximum(jnp.max(s, axis=0, keepdims=True), sink)
        p = jnp.exp(s - m)
        pv = _dot(jnp.concatenate([vT, jnp.ones((ONES_ROWS, vT.shape[1]), BF16)], axis=0), p.astype(BF16))
        o = pv[:HEAD_DIM] / (pv[HEAD_DIM:HEAD_DIM + 1] + jnp.exp(sink - m))
        return [o[:, jh * tq:(jh + 1) * tq] for jh in range(SWA_ITEM_HEADS)]

    pending = [scores(item) for item in items[:SWA_DEPTH]]
    outs = []
    for i, item in enumerate(items):
        if i + SWA_DEPTH < len(items):
            pending.append(scores(items[i + SWA_DEPTH]))
        outs += attend(item, pending.pop(0))
    o_ref[...] = jnp.concatenate(outs, axis=0).T.astype(o_ref.dtype)


def _swa_attention(qT, k, vT, kc, vcT, sinks, *, has_latent):
    n = qT.shape[1]
    nc = kc.shape[0]
    tq = SWA_TQ
    nb = n // tq
    sink_rows = jnp.repeat(sinks.astype(F32).reshape(GQA_KV_HEADS, 1, GQA_GROUP), tq, axis=2)
    ctx_specs = [_resident((nc, C_KV_COLS)), _resident((C_KV_COLS, nc)), _resident(sink_rows.shape)]
    if has_latent:
        prev = lambda b: jnp.maximum(b - 1, 0)
        nxt = lambda b: jnp.minimum(b + 1, nb - 1)
        in_specs = [
            pl.BlockSpec((C_Q_COLS, tq), lambda b: (0, b)),
            pl.BlockSpec((tq, C_KV_COLS), lambda b: (prev(b), 0)),
            pl.BlockSpec((tq, C_KV_COLS), lambda b: (b, 0)),
            pl.BlockSpec((tq, C_KV_COLS), lambda b: (nxt(b), 0)),
            pl.BlockSpec((C_KV_COLS, tq), lambda b: (0, prev(b))),
            pl.BlockSpec((C_KV_COLS, tq), lambda b: (0, b)),
            pl.BlockSpec((C_KV_COLS, tq), lambda b: (0, nxt(b))),
        ] + ctx_specs
        args = (qT, k, k, k, vT, vT, vT, kc, vcT, sink_rows)
    else:
        in_specs = [pl.BlockSpec((C_Q_COLS, tq), lambda b: (0, b))] + ctx_specs
        args = (qT, kc, vcT, sink_rows)
    return pl.pallas_call(
        functools.partial(_swa_kernel, has_latent=has_latent),
        grid=(nb,),
        in_specs=in_specs,
        out_specs=pl.BlockSpec((tq, C_Q_COLS), lambda b: (b, 0)),
        out_shape=jax.ShapeDtypeStruct((n, C_Q_COLS), BF16),
        compiler_params=_cparams("arbitrary"),
        name="swa_attn" if has_latent else "swa_attn_ctx",
    )(*args)


def _out_ffn_kernel(*refs, n_attn, final):
    x_ref = refs[0]
    attn_refs = refs[1:1 + n_attn]
    wout_ref, mod_ref, g_ref, w1_ref, w3_ref, w2_ref = refs[1 + n_attn:7 + n_attn]
    rest = refs[7 + n_attn:]
    if final:
        fg_ref, o_ref = rest
    else:
        (o_ref,) = rest
    proj = None
    r0 = 0
    for a_ref in attn_refs:
        wdt = a_ref.shape[1]
        part = _dot(a_ref[...], wout_ref[r0:r0 + wdt, :])
        proj = part if proj is None else proj + part
        r0 += wdt
    x1 = x_ref[...] + mod_ref[2:3, :] * proj
    h = _norm_mod(x1, g_ref[...], mod_ref[3:4, :], mod_ref[4:5, :]).astype(BF16)
    hidden = w1_ref.shape[1]
    chunk = hidden // 2
    ffn = None
    for c0 in range(0, hidden, chunk):
        a = _dot(h, w1_ref[:, c0:c0 + chunk])
        b = _dot(h, w3_ref[:, c0:c0 + chunk])
        gated = (a * jax.nn.sigmoid(a) * b).astype(BF16)
        part = _dot(gated, w2_ref[c0:c0 + chunk, :])
        ffn = part if ffn is None else ffn + part
    x2 = x1 + mod_ref[5:6, :] * ffn
    if final:
        ms = jnp.mean(x2 * x2, axis=-1, keepdims=True)
        x2 = x2 * lax.rsqrt(ms + NORM_EPS) * fg_ref[...]
    o_ref[...] = x2


def _out_ffn(x, attn, w_out, mod, g, w1, w3, w2, final_g=None):
    n, d = x.shape
    tm = min(ROW_TILE, n)
    row = lambda i: (i, 0)
    final = final_g is not None
    in_specs = [pl.BlockSpec((tm, d), row)]
    in_specs += [pl.BlockSpec((tm, a.shape[1]), row) for a in attn]
    in_specs += [_resident(w_out.shape), _resident(mod.shape), _resident((1, d)),
                 _resident(w1.shape), _resident(w3.shape), _resident(w2.shape)]
    args = [x, *attn, w_out, mod, g, w1, w3, w2]
    if final:
        in_specs.append(_resident((1, d)))
        args.append(final_g)
    return pl.pallas_call(
        functools.partial(_out_ffn_kernel, n_attn=len(attn), final=final),
        grid=(n // tm,),
        in_specs=in_specs,
        out_specs=pl.BlockSpec((tm, d), row),
        out_shape=jax.ShapeDtypeStruct((n, d), F32),
        compiler_params=_cparams("arbitrary"),
        name="out_ffn",
    )(*args)


def _rope_tables(n):
    t = jnp.arange(n)
    row = (t // GRID_W).astype(F32)
    col = (t % GRID_W).astype(F32)
    quarter = HEAD_DIM // 4
    inv = ROPE_BASE ** (-jnp.arange(quarter, dtype=F32) / quarter)
    ar = row[:, None] * inv
    ac = col[:, None] * inv
    ang = jnp.concatenate([ar, ar, ac, ac], axis=-1)
    sign = jnp.tile(jnp.concatenate([-jnp.ones(quarter, F32), jnp.ones(quarter, F32)]), 2)
    cos = jnp.tile(jnp.cos(ang), (1, LANES // HEAD_DIM))
    sin = jnp.tile(jnp.sin(ang) * sign, (1, LANES // HEAD_DIM))
    return cos, sin


def _trunk(x, ctx, mods, norm_g, w_in_ab, w_out_ab, diff_lambda, diff_sub_g, na_rpb, w_in_c, w_out_c,
           attn_sinks, ffn_w1, ffn_w3, ffn_w2, final_g):
    depth = mods.shape[0]
    s_len = x.shape[0]
    tables = _rope_tables(s_len)
    for layer in range(depth):
        ctx_out = layer < depth - 1
        mod_x, mod_c = mods[layer, 0], mods[layer, 1]
        g0 = norm_g[layer, 0].reshape(1, -1)
        g1 = norm_g[layer, 1].reshape(1, -1)
        if layer % 2 == 0:
            e = layer // 2
            lam_init = 0.8 - 0.6 * math.exp(-0.3 * layer)
            w = w_in_ab[e]
            q_cols = DIFF_WIDTH + NA_WIDTH
            a_k, a_v, b_k, b_v = (w[:, q_cols + i * DIFF_WIDTH:q_cols + (i + 1) * DIFF_WIDTH] for i in range(4))
            w = jnp.concatenate([w[:, :q_cols], a_k, b_k, a_v, b_v], axis=1).astype(BF16)
            dims = dict(n_rope_q=DIFF_WIDTH, n_q=q_cols, n_rope_k=DIFF_WIDTH, n_k=q_cols)
            qT, k, vT = _proj_in(x, g0, mod_x, w, tables, **dims)
            qcT, kc, vcT = _proj_in(ctx, g0, mod_c, w, None, **dims)
            lam = diff_lambda[e].astype(F32)
            subg = diff_sub_g[e].astype(F32).reshape(-1, 1)
            bias = _na_bias_blocks(na_rpb[e])
            a_x = _diff_attention(qT, k, vT, kc, vcT, lam, subg, lam_init, has_latent=True)
            b_x = _na_attention(qT, k, vT, kc, vcT, bias, has_latent=True)
            attn_x = [a_x, b_x]
            if ctx_out:
                a_c = _diff_attention(qcT, None, None, kc, vcT, lam, subg, lam_init, has_latent=False)
                b_c = _na_attention(qcT, None, None, kc, vcT, None, has_latent=False)
                attn_c = [a_c, b_c]
            w_out = w_out_ab[e].astype(BF16)
        else:
            o = layer // 2
            w = w_in_c[o].astype(BF16)
            dims = dict(n_rope_q=C_Q_COLS, n_q=C_Q_COLS, n_rope_k=C_KV_COLS, n_k=C_KV_COLS)
            qT, k, vT = _proj_in(x, g0, mod_x, w, tables, **dims)
            qcT, kc, vcT = _proj_in(ctx, g0, mod_c, w, None, **dims)
            attn_x = [_swa_attention(qT, k, vT, kc, vcT, attn_sinks[o], has_latent=True)]
            if ctx_out:
                attn_c = [_swa_attention(qcT, None, None, kc, vcT, attn_sinks[o], has_latent=False)]
            w_out = w_out_c[o].astype(BF16)
        w1, w3, w2 = ffn_w1[layer].astype(BF16), ffn_w3[layer].astype(BF16), ffn_w2[layer].astype(BF16)
        fg = final_g.reshape(1, -1) if layer == depth - 1 else None
        x = _out_ffn(x, attn_x, w_out, mod_x, g1, w1, w3, w2, fg)
        if ctx_out:
            ctx = _out_ffn(ctx, attn_c, w_out, mod_c, g1, w1, w3, w2)
    return x


def kernel(x, c, ctx, c_ctx, ada_w, ada_b, norm_g, w_in_ab, w_out_ab, diff_lambda, diff_sub_g, na_rpb, w_in_c,
           w_out_c, attn_sinks, ffn_w1, ffn_w3, ffn_w2, final_g):
    batch, _, d = x.shape
    depth = ada_w.shape[0]
    outs = []
    for b in range(batch):
        cc = jnp.zeros((8, d), F32).at[0].set(c[b].astype(F32)).at[1].set(c_ctx.astype(F32))
        mods = _adaln(cc, ada_w, ada_b)[:, :2].reshape(depth, 2, 6, d)
        outs.append(_trunk(x[b], ctx[b], mods, norm_g, w_in_ab, w_out_ab, diff_lambda, diff_sub_g, na_rpb,
                           w_in_c, w_out_c, attn_sinks, ffn_w1, ffn_w3, ffn_w2, final_g))
    return jnp.stack(outs).astype(x.dtype)
```

```python
import functools
import math

import jax
import jax.numpy as jnp
from jax import lax
from jax.experimental import pallas as pl
from jax.experimental.pallas import tpu as pltpu

F32 = jnp.float32
BF16 = jnp.bfloat16

D_MODEL = 1024
GRID_W = 64
HEAD_DIM = 64
ROPE_BASE = 10000.0
NORM_EPS = 1e-6
NEG_INF = -1e30
DIFF_HEADS = 4
DIFF_WIDTH = DIFF_HEADS * 2 * HEAD_DIM
NA_HEADS = 8
NA_WIDTH = NA_HEADS * HEAD_DIM
NA_MAX_ROWS = 8
NA_COLS = 16
GQA_HEADS = 16
GQA_KV_HEADS = 4
GQA_GROUP = GQA_HEADS // GQA_KV_HEADS
SWA_WINDOW = 128
C_Q_COLS = GQA_HEADS * HEAD_DIM
C_KV_COLS = GQA_KV_HEADS * HEAD_DIM
LOG2E = math.log2(math.e)
Q_SCALE = HEAD_DIM ** -0.5 * LOG2E

LANES = 128
ONES_ROWS = 16
VMEM_LIMIT = 56 * 1024 * 1024

ROW_TILE = 512
DIFF_TQ = 1024
DIFF_TK = 512
DIFF_QC = 256
DIFF_RC = 64
DIFF_DEPTH = 4
NA_TILE_ROWS = 4
NA_TQ = NA_TILE_ROWS * GRID_W
SWA_TQ = SWA_WINDOW
SWA_ITEM_HEADS = 2
SWA_DEPTH = 2


def _cparams(*sem):
    return pltpu.CompilerParams(dimension_semantics=sem, vmem_limit_bytes=VMEM_LIMIT)


def _resident(shape):
    nd = len(shape)
    return pl.BlockSpec(shape, lambda *_: (0,) * nd, pipeline_mode=pl.Buffered(1))


def _dot(a, b):
    return jnp.dot(a, b, preferred_element_type=F32)


def _adaln_kernel(cc_ref, w_ref, b_ref, o_ref):
    cc = cc_ref[...]
    h = (cc * jax.nn.sigmoid(cc)).astype(BF16)
    o_ref[0] = _dot(h, w_ref[0].astype(BF16)) + b_ref[0]


def _adaln(cc, ada_w, ada_b):
    depth, d, d6 = ada_w.shape
    nj = d6 // d
    return pl.pallas_call(
        _adaln_kernel,
        grid=(depth, nj),
        in_specs=[
            pl.BlockSpec((8, d), lambda l, j: (0, 0)),
            pl.BlockSpec((1, d, d), lambda l, j: (l, 0, j)),
            pl.BlockSpec((1, 1, d), lambda l, j: (l, 0, j)),
        ],
        out_specs=pl.BlockSpec((1, 8, d), lambda l, j: (l, 0, j)),
        out_shape=jax.ShapeDtypeStruct((depth, 8, d6), F32),
        compiler_params=_cparams("arbitrary", "arbitrary"),
        name="adaln",
    )(cc, ada_w, ada_b.reshape(depth, 1, d6))


def _norm_mod(x, g, shift, scale):
    ms = jnp.mean(x * x, axis=-1, keepdims=True)
    h = x * lax.rsqrt(ms + NORM_EPS) * g
    return h * (1.0 + scale) + shift


def _rope_lanes(t, cos, sin_signed):
    lane = lax.broadcasted_iota(jnp.int32, (t.shape[0], LANES), 1)
    first = (lane % (HEAD_DIM // 2)) < (HEAD_DIM // 4)
    out = []
    for c in range(t.shape[1] // LANES):
        tc = t[:, c * LANES:(c + 1) * LANES]
        partner = jnp.where(first, pltpu.roll(tc, LANES - HEAD_DIM // 4, 1), pltpu.roll(tc, HEAD_DIM // 4, 1))
        out.append(tc * cos + partner * sin_signed)
    return jnp.concatenate(out, axis=1)


def _proj_in_kernel(*refs, n_rope_q, n_q, n_rope_k, n_k, rope):
    if rope:
        x_ref, g_ref, mod_ref, cos_ref, sin_ref, w_ref, qT_ref, k_ref, vT_ref = refs
    else:
        x_ref, g_ref, mod_ref, w_ref, qT_ref, k_ref, vT_ref = refs
    h = _norm_mod(x_ref[...], g_ref[...], mod_ref[0:1, :], mod_ref[1:2, :])
    y = _dot(h.astype(BF16), w_ref[...])
    q = y[:, :n_q]
    k = y[:, n_q:n_q + n_k]
    v = y[:, n_q + n_k:]
    if rope:
        cos, sin = cos_ref[...], sin_ref[...]
        q = jnp.concatenate([_rope_lanes(q[:, :n_rope_q], cos, sin), q[:, n_rope_q:]], axis=1) \
            if n_rope_q < n_q else _rope_lanes(q, cos, sin)
        k = jnp.concatenate([_rope_lanes(k[:, :n_rope_k], cos, sin), k[:, n_rope_k:]], axis=1) \
            if n_rope_k < n_k else _rope_lanes(k, cos, sin)
    qT_ref[...] = (q * Q_SCALE).T.astype(BF16)
    k_ref[...] = k.astype(BF16)
    vT_ref[...] = v.T.astype(BF16)


def _proj_in(x, g, mod, w, tables, *, n_rope_q, n_q, n_rope_k, n_k):
    n, d = x.shape
    cols = w.shape[1]
    n_v = cols - n_q - n_k
    tm = min(ROW_TILE, n)
    rope = tables is not None
    row = lambda i: (i, 0)
    in_specs = [pl.BlockSpec((tm, d), row), _resident((1, d)), _resident(mod.shape)]
    args = [x, g, mod]
    if rope:
        in_specs += [pl.BlockSpec((tm, LANES), row), pl.BlockSpec((tm, LANES), row)]
        args += list(tables)
    in_specs.append(_resident(w.shape))
    args.append(w)
    return pl.pallas_call(
        functools.partial(_proj_in_kernel, n_rope_q=n_rope_q, n_q=n_q, n_rope_k=n_rope_k, n_k=n_k, rope=rope),
        grid=(n // tm,),
        in_specs=in_specs,
        out_specs=[
            pl.BlockSpec((n_q, tm), lambda i: (0, i)),
            pl.BlockSpec((tm, n_k), row),
            pl.BlockSpec((n_v, tm), lambda i: (0, i)),
        ],
        out_shape=[
            jax.ShapeDtypeStruct((n_q, n), BF16),
            jax.ShapeDtypeStruct((n, n_k), BF16),
            jax.ShapeDtypeStruct((n_v, n), BF16),
        ],
        compiler_params=_cparams("arbitrary"),
        name="proj_in",
    )(*args)


def _half_masked(qT_pair, e):
    z = jnp.zeros((HEAD_DIM, qT_pair.shape[1]), qT_pair.dtype)
    if e == 0:
        return jnp.concatenate([qT_pair[:HEAD_DIM], z], axis=0)
    return jnp.concatenate([z, qT_pair[HEAD_DIM:]], axis=0)


def _diff_kernel(*refs, lam_init, has_latent):
    if has_latent:
        qT_ref, k_ref, vT_ref, kc_ref, vcT_ref, lam_ref, subg_ref, o_ref = refs[:8]
        qm_ref, m_ref, alpha_ref, l_ref, acc_ref, s_ref, p_ref = refs[8:]
        j = pl.program_id(1)
        last = pl.num_programs(1) - 1
    else:
        qT_ref, kc_ref, vcT_ref, lam_ref, subg_ref, o_ref = refs[:6]
        qm_ref, m_ref, alpha_ref, l_ref, acc_ref, s_ref, p_ref = refs[6:]

    tq = qm_ref.shape[2]
    items = [(h, e, c) for h in range(DIFF_HEADS) for e in range(2) for c in range(0, tq, DIFF_QC)]

    def start():
        for h in range(DIFF_HEADS):
            pair = qT_ref[h * LANES:(h + 1) * LANES, :]
            for e in range(2):
                qm_ref[2 * h + e] = _half_masked(pair, e)
        m_ref[...] = jnp.full(m_ref.shape, NEG_INF, F32)
        l_ref[...] = jnp.zeros(l_ref.shape, F32)
        acc_ref[...] = jnp.zeros(acc_ref.shape, F32)
        fold(kc_ref, vcT_ref)

    def fold(keys_ref, valsT_ref):
        nk = keys_ref.shape[0]
        sub = 8

        def scores(i):
            h, e, c = items[i]
            mp = 2 * h + e
            s = _dot(keys_ref[:, h * LANES:(h + 1) * LANES], qm_ref[mp, :, c:c + DIFF_QC])
            s_ref[i % (DIFF_DEPTH + 1), :nk, :] = s
            m_part = jnp.max(s.reshape(nk // sub, sub, DIFF_QC), axis=0)
            m_old = m_ref[mp, :, c:c + DIFF_QC]
            m_new = jnp.maximum(m_old, jnp.max(m_part, axis=0, keepdims=True))
            m_ref[mp, :, c:c + DIFF_QC] = m_new
            alpha_ref[mp, :, c:c + DIFF_QC] = jnp.exp2(m_old - m_new)

        def absorb(i):
            h, e, c = items[i]
            mp = 2 * h + e
            cols = slice(c, c + DIFF_QC)
            m_new = m_ref[mp, :, cols]
            alpha = alpha_ref[mp, :, cols]
            for r in range(0, nk, DIFF_RC):
                p = jnp.exp2(s_ref[i % (DIFF_DEPTH + 1), r:r + DIFF_RC, :] - m_new)
                p_ref[i % 2, r:r + DIFF_RC, :] = p.astype(BF16)
            lhs = jnp.concatenate([valsT_ref[h * LANES:(h + 1) * LANES, :], jnp.ones((ONES_ROWS, nk), BF16)], axis=0)
            pv = _dot(lhs, p_ref[i % 2, :nk, :])
            l_ref[mp, :, cols] = alpha * l_ref[mp, :, cols] + pv[LANES:LANES + 1]
            acc_ref[mp, :, cols] = alpha * acc_ref[mp, :, cols] + pv[:LANES]

        for i in range(min(DIFF_DEPTH, len(items))):
            scores(i)
        for i in range(len(items)):
            if i + DIFF_DEPTH < len(items):
                scores(i + DIFF_DEPTH)
            absorb(i)

    def finish():
        lam = lam_ref[...]
        lam_val = (jnp.exp(jnp.sum(lam[0:1] * lam[1:2], axis=1, keepdims=True))
                   - jnp.exp(jnp.sum(lam[2:3] * lam[3:4], axis=1, keepdims=True)) + lam_init)
        subg = subg_ref[...]
        for h in range(DIFF_HEADS):
            a0 = acc_ref[2 * h] / l_ref[2 * h]
            a1 = acc_ref[2 * h + 1] / l_ref[2 * h + 1]
            a = a0 - lam_val * a1
            ms = jnp.mean(a * a, axis=0, keepdims=True)
            a = a * lax.rsqrt(ms + NORM_EPS) * subg * (1.0 - lam_init)
            o_ref[:, h * LANES:(h + 1) * LANES] = a.T.astype(o_ref.dtype)

    if has_latent:
        pl.when(j == 0)(start)
        fold(k_ref, vT_ref)
        pl.when(j == last)(finish)
    else:
        start()
        finish()


def _diff_attention(qT, k, vT, kc, vcT, lam, subg, lam_init, *, has_latent):
    n = qT.shape[1]
    nc = kc.shape[0]
    w = DIFF_WIDTH
    small = [_resident((4, HEAD_DIM)), _resident((2 * HEAD_DIM, 1))]
    if has_latent:
        tq, tk = DIFF_TQ, DIFF_TK
        grid = (n // tq, k.shape[0] // tk)
        in_specs = [
            pl.BlockSpec((w, tq), lambda i, j: (0, i)),
            pl.BlockSpec((tk, w), lambda i, j: (j, 0)),
            pl.BlockSpec((w, tk), lambda i, j: (0, j)),
            pl.BlockSpec((nc, w), lambda i, j: (0, 0)),
            pl.BlockSpec((w, nc), lambda i, j: (0, 0)),
        ] + small
        args = (qT, k, vT, kc, vcT, lam, subg)
        out_spec = pl.BlockSpec((tq, w), lambda i, j: (i, 0))
        sem = ("parallel", "arbitrary")
    else:
        tq, tk = n, 0
        grid = (1,)
        in_specs = [
            pl.BlockSpec((w, tq), lambda i: (0, 0)),
            pl.BlockSpec((nc, w), lambda i: (0, 0)),
            pl.BlockSpec((w, nc), lambda i: (0, 0)),
        ] + small
        args = (qT, kc, vcT, lam, subg)
        out_spec = pl.BlockSpec((tq, w), lambda i: (0, 0))
        sem = ("arbitrary",)
    return pl.pallas_call(
        functools.partial(_diff_kernel, lam_init=lam_init, has_latent=has_latent),
        grid=grid,
        in_specs=in_specs,
        out_specs=out_spec,
        out_shape=jax.ShapeDtypeStruct((n, w), BF16),
        scratch_shapes=[
            pltpu.VMEM((2 * DIFF_HEADS, LANES, tq), BF16),
            pltpu.VMEM((2 * DIFF_HEADS, 1, tq), F32),
            pltpu.VMEM((2 * DIFF_HEADS, 1, tq), F32),
            pltpu.VMEM((2 * DIFF_HEADS, 1, tq), F32),
            pltpu.VMEM((2 * DIFF_HEADS, LANES, tq), F32),
            pltpu.VMEM((DIFF_DEPTH + 1, max(tk, nc), DIFF_QC), F32),
            pltpu.VMEM((2, max(tk, nc), DIFF_QC), BF16),
        ],
        compiler_params=_cparams(*sem),
        name="diff_attn" if has_latent else "diff_attn_ctx",
    )(*args)


def _na_bias_kernel(rpb_ref, o_ref):
    h = pl.program_id(0)
    n_dr, n_dc = 2 * NA_MAX_ROWS - 1, 2 * NA_COLS - 1
    kc = lax.broadcasted_iota(jnp.int32, (GRID_W, LANES), 0)
    lane = lax.broadcasted_iota(jnp.int32, (GRID_W, LANES), 1)
    c = lane % GRID_W
    upper = lane >= GRID_W
    dc = kc - c + (NA_COLS - 1)
    cs = jnp.clip(c - NA_COLS // 2, 0, GRID_W - NA_COLS)
    col_ok = (kc >= cs) & (kc < cs + NA_COLS)
    for i in range(o_ref.shape[1]):
        dr_idx = (i - 1, i - 2)
        val = jnp.full((GRID_W, LANES), NEG_INF, F32)
        for d in range(n_dc):
            lo, up = (rpb_ref[(h * n_dr + r) * n_dc + d] * LOG2E if 0 <= r < n_dr else NEG_INF for r in dr_idx)
            val = jnp.where(dc == d, jnp.where(upper, up, lo), val)
        o_ref[0, i] = jnp.where(col_ok, val, NEG_INF)


def _na_bias_blocks(rpb):
    n_heads = rpb.shape[0]
    n_blocks = 2 * NA_MAX_ROWS
    return pl.pallas_call(
        _na_bias_kernel,
        grid=(n_heads,),
        in_specs=[pl.BlockSpec(memory_space=pltpu.SMEM)],
        out_specs=pl.BlockSpec((1, n_blocks, GRID_W, LANES), lambda h: (h, 0, 0, 0)),
        out_shape=jax.ShapeDtypeStruct((n_heads, n_blocks, GRID_W, LANES), F32),
        compiler_params=_cparams("arbitrary"),
        name="na_bias",
    )(rpb.astype(F32).reshape(-1))


def _na_window_bias(bias_ref, e, t, grid_rows):
    lane = lax.broadcasted_iota(jnp.int32, (GRID_W, LANES), 1)
    r0 = t * NA_TILE_ROWS
    rows = []
    for rk in range(3 * NA_TILE_ROWS):
        kr = r0 - NA_TILE_ROWS + rk
        cols = []
        for p in range(NA_TILE_ROWS // 2):
            blk = bias_ref[e, rk - NA_TILE_ROWS - 2 * p + NA_MAX_ROWS]
            r = r0 + 2 * p + lane // GRID_W
            rs = jnp.clip(r - NA_MAX_ROWS // 2, 0, grid_rows - NA_MAX_ROWS)
            ok = (kr >= rs) & (kr < rs + NA_MAX_ROWS)
            cols.append(jnp.where(ok, blk, NEG_INF))
        rows.append(jnp.concatenate(cols, axis=1))
    return jnp.concatenate(rows, axis=0)


def _na_kernel(*refs, has_latent, grid_rows):
    if has_latent:
        qT_ref, km_ref, k0_ref, kp_ref, vm_ref, v0_ref, vp_ref, kc_ref, vcT_ref, bias_ref, o_ref = refs
    else:
        qT_ref, kc_ref, vcT_ref, o_ref = refs
    pair = qT_ref[...]
    kc = kc_ref[...]
    if has_latent:
        vT = jnp.concatenate([vm_ref[...], v0_ref[...], vp_ref[...], vcT_ref[...]], axis=1)
    else:
        vT = vcT_ref[...]
    ones = jnp.ones((ONES_ROWS, vT.shape[1]), BF16)
    scores = []
    for e in range(2):
        qm = _half_masked(pair, e)
        s = _dot(kc, qm)
        if has_latent:
            s_nb = jnp.concatenate([_dot(km_ref[...], qm), _dot(k0_ref[...], qm), _dot(kp_ref[...], qm)], axis=0)
            s = jnp.concatenate([s_nb + _na_window_bias(bias_ref, e, pl.program_id(1), grid_rows), s], axis=0)
        scores.append(s)
    outs = []
    for e in range(2):
        s = scores[e]
        p = jnp.exp2(s - jnp.max(s, axis=0, keepdims=True))
        pv = _dot(jnp.concatenate([vT[e * HEAD_DIM:(e + 1) * HEAD_DIM], ones], axis=0), p.astype(BF16))
        outs.append(pv[:HEAD_DIM] / pv[HEAD_DIM:HEAD_DIM + 1])
    o_ref[...] = jnp.concatenate(outs, axis=0).T.astype(o_ref.dtype)


def _na_attention(qT_all, k_all, vT_all, kc_all, vcT_all, bias, *, has_latent):
    n = qT_all.shape[1]
    nc = kc_all.shape[0]
    g0 = DIFF_WIDTH // LANES
    n_pairs = NA_HEADS // 2
    if has_latent:
        tq = NA_TQ
        nt = n // tq
        prev = lambda t: jnp.maximum(t - 1, 0)
        nxt = lambda t: jnp.minimum(t + 1, nt - 1)
        in_specs = [
            pl.BlockSpec((LANES, tq), lambda hp, t: (g0 + hp, t)),
            pl.BlockSpec((tq, LANES), lambda hp, t: (prev(t), g0 + hp)),
            pl.BlockSpec((tq, LANES), lambda hp, t: (t, g0 + hp)),
            pl.BlockSpec((tq, LANES), lambda hp, t: (nxt(t), g0 + hp)),
            pl.BlockSpec((LANES, tq), lambda hp, t: (g0 + hp, prev(t))),
            pl.BlockSpec((LANES, tq), lambda hp, t: (g0 + hp, t)),
            pl.BlockSpec((LANES, tq), lambda hp, t: (g0 + hp, nxt(t))),
            pl.BlockSpec((nc, LANES), lambda hp, t: (0, g0 + hp)),
            pl.BlockSpec((LANES, nc), lambda hp, t: (g0 + hp, 0)),
            pl.BlockSpec((2,) + bias.shape[1:], lambda hp, t: (hp, 0, 0, 0)),
        ]
        args = (qT_all, k_all, k_all, k_all, vT_all, vT_all, vT_all, kc_all, vcT_all, bias)
    else:
        tq = n
        nt = 1
        in_specs = [
            pl.BlockSpec((LANES, tq), lambda hp, t: (g0 + hp, 0)),
            pl.BlockSpec((nc, LANES), lambda hp, t: (0, g0 + hp)),
            pl.BlockSpec((LANES, nc), lambda hp, t: (g0 + hp, 0)),
        ]
        args = (qT_all, kc_all, vcT_all)
    return pl.pallas_call(
        functools.partial(_na_kernel, has_latent=has_latent, grid_rows=n // GRID_W),
        grid=(n_pairs, nt),
        in_specs=in_specs,
        out_specs=pl.BlockSpec((tq, LANES), lambda hp, t: (t, hp)),
        out_shape=jax.ShapeDtypeStruct((n, NA_WIDTH), BF16),
        compiler_params=_cparams("arbitrary", "arbitrary"),
        name="na_attn" if has_latent else "na_attn_ctx",
    )(*args)


def _swa_kernel(*refs, has_latent):
    if has_latent:
        qT_ref, km_ref, k0_ref, kp_ref, vm_ref, v0_ref, vp_ref, kc_ref, vcT_ref, sink_ref, o_ref = refs
        b = pl.program_id(0)
        nb = pl.num_programs(0)
        tq = qT_ref.shape[1]
        ki = lax.broadcasted_iota(jnp.int32, (tq, tq), 0)
        qi = lax.broadcasted_iota(jnp.int32, (tq, tq), 1)
        band_prev = jnp.concatenate([(ki >= qi) & (b > 0)] * SWA_ITEM_HEADS, axis=1)
        band_next = jnp.concatenate([(ki <= qi) & (b < nb - 1)] * SWA_ITEM_HEADS, axis=1)
    else:
        qT_ref, kc_ref, vcT_ref, sink_ref, o_ref = refs
        tq = qT_ref.shape[1]
    zeros = jnp.zeros((HEAD_DIM, tq), BF16)
    width = SWA_ITEM_HEADS * tq
    items = [(g, j0) for g in range(GQA_KV_HEADS) for j0 in range(0, GQA_GROUP, SWA_ITEM_HEADS)]

    def scores(item):
        g, j0 = item
        lanes = slice((g // 2) * LANES, (g // 2 + 1) * LANES)
        q_cols = []
        for jh in range(j0, j0 + SWA_ITEM_HEADS):
            hq = g * GQA_GROUP + jh
            qh = qT_ref[hq * HEAD_DIM:(hq + 1) * HEAD_DIM, :]
            q_cols.append(jnp.concatenate([qh, zeros] if g % 2 == 0 else [zeros, qh], axis=0))
        qm = jnp.concatenate(q_cols, axis=1)
        s = _dot(kc_ref[:, lanes], qm)
        if has_latent:
            s_prev = jnp.where(band_prev, _dot(km_ref[:, lanes], qm), NEG_INF)
            s_next = jnp.where(band_next, _dot(kp_ref[:, lanes], qm), NEG_INF)
            s = jnp.concatenate([s, s_prev, _dot(k0_ref[:, lanes], qm), s_next], axis=0)
        return s

    def attend(item, s):
        g, j0 = item
        rows = slice(g * HEAD_DIM, (g + 1) * HEAD_DIM)
        vT = vcT_ref[rows, :]
        if has_latent:
            vT = jnp.concatenate([vT, vm_ref[rows, :], v0_ref[rows, :], vp_ref[rows, :]], axis=1)
        sink = sink_ref[g][:, j0 * tq:j0 * tq + width] * LOG2E
        m = jnp.maximum(jnp.max(s, axis=0, keepdims=True), sink)
        p = jnp.exp2(s - m)
        pv = _dot(jnp.concatenate([vT, jnp.ones((ONES_ROWS, vT.shape[1]), BF16)], axis=0), p.astype(BF16))
        o = pv[:HEAD_DIM] / (pv[HEAD_DIM:HEAD_DIM + 1] + jnp.exp2(sink - m))
        return [o[:, jh * tq:(jh + 1) * tq] for jh in range(SWA_ITEM_HEADS)]

    pending = [scores(item) for item in items[:SWA_DEPTH]]
    outs = []
    for i, item in enumerate(items):
        if i + SWA_DEPTH < len(items):
            pending.append(scores(items[i + SWA_DEPTH]))
        outs += attend(item, pending.pop(0))
    o_ref[...] = jnp.concatenate(outs, axis=0).T.astype(o_ref.dtype)


def _swa_attention(qT, k, vT, kc, vcT, sinks, *, has_latent):
    n = qT.shape[1]
    nc = kc.shape[0]
    tq = SWA_TQ
    nb = n // tq
    sink_rows = jnp.repeat(sinks.astype(F32).reshape(GQA_KV_HEADS, 1, GQA_GROUP), tq, axis=2)
    ctx_specs = [_resident((nc, C_KV_COLS)), _resident((C_KV_COLS, nc)), _resident(sink_rows.shape)]
    if has_latent:
        prev = lambda b: jnp.maximum(b - 1, 0)
        nxt = lambda b: jnp.minimum(b + 1, nb - 1)
        in_specs = [
            pl.BlockSpec((C_Q_COLS, tq), lambda b: (0, b)),
            pl.BlockSpec((tq, C_KV_COLS), lambda b: (prev(b), 0)),
            pl.BlockSpec((tq, C_KV_COLS), lambda b: (b, 0)),
            pl.BlockSpec((tq, C_KV_COLS), lambda b: (nxt(b), 0)),
            pl.BlockSpec((C_KV_COLS, tq), lambda b: (0, prev(b))),
            pl.BlockSpec((C_KV_COLS, tq), lambda b: (0, b)),
            pl.BlockSpec((C_KV_COLS, tq), lambda b: (0, nxt(b))),
        ] + ctx_specs
        args = (qT, k, k, k, vT, vT, vT, kc, vcT, sink_rows)
    else:
        in_specs = [pl.BlockSpec((C_Q_COLS, tq), lambda b: (0, b))] + ctx_specs
        args = (qT, kc, vcT, sink_rows)
    return pl.pallas_call(
        functools.partial(_swa_kernel, has_latent=has_latent),
        grid=(nb,),
        in_specs=in_specs,
        out_specs=pl.BlockSpec((tq, C_Q_COLS), lambda b: (b, 0)),
        out_shape=jax.ShapeDtypeStruct((n, C_Q_COLS), BF16),
        compiler_params=_cparams("arbitrary"),
        name="swa_attn" if has_latent else "swa_attn_ctx",
    )(*args)


def _out_ffn_kernel(*refs, n_attn, final):
    x_ref = refs[0]
    attn_refs = refs[1:1 + n_attn]
    wout_ref, mod_ref, g_ref, w1_ref, w3_ref, w2_ref = refs[1 + n_attn:7 + n_attn]
    rest = refs[7 + n_attn:]
    if final:
        fg_ref, o_ref = rest
    else:
        (o_ref,) = rest
    proj = None
    r0 = 0
    for a_ref in attn_refs:
        wdt = a_ref.shape[1]
        part = _dot(a_ref[...], wout_ref[r0:r0 + wdt, :])
        proj = part if proj is None else proj + part
        r0 += wdt
    x1 = x_ref[...] + mod_ref[2:3, :] * proj
    h = _norm_mod(x1, g_ref[...], mod_ref[3:4, :], mod_ref[4:5, :]).astype(BF16)
    hidden = w1_ref.shape[1]
    chunk = hidden // 2
    ffn = None
    for c0 in range(0, hidden, chunk):
        a = _dot(h, w1_ref[:, c0:c0 + chunk])
        b = _dot(h, w3_ref[:, c0:c0 + chunk])
        gated = (a * jax.nn.sigmoid(a) * b).astype(BF16)
        part = _dot(gated, w2_ref[c0:c0 + chunk, :])
        ffn = part if ffn is None else ffn + part
    x2 = x1 + mod_ref[5:6, :] * ffn
    if final:
        ms = jnp.mean(x2 * x2, axis=-1, keepdims=True)
        x2 = x2 * lax.rsqrt(ms + NORM_EPS) * fg_ref[...]
    o_ref[...] = x2


def _out_ffn(x, attn, w_out, mod, g, w1, w3, w2, final_g=None):
    n, d = x.shape
    tm = min(ROW_TILE, n)
    row = lambda i: (i, 0)
    final = final_g is not None
    in_specs = [pl.BlockSpec((tm, d), row)]
    in_specs += [pl.BlockSpec((tm, a.shape[1]), row) for a in attn]
    in_specs += [_resident(w_out.shape), _resident(mod.shape), _resident((1, d)),
                 _resident(w1.shape), _resident(w3.shape), _resident(w2.shape)]
    args = [x, *attn, w_out, mod, g, w1, w3, w2]
    if final:
        in_specs.append(_resident((1, d)))
        args.append(final_g)
    return pl.pallas_call(
        functools.partial(_out_ffn_kernel, n_attn=len(attn), final=final),
        grid=(n // tm,),
        in_specs=in_specs,
        out_specs=pl.BlockSpec((tm, d), row),
        out_shape=jax.ShapeDtypeStruct((n, d), F32),
        compiler_params=_cparams("arbitrary"),
        name="out_ffn",
    )(*args)


def _rope_tables(n):
    t = jnp.arange(n)
    row = (t // GRID_W).astype(F32)
    col = (t % GRID_W).astype(F32)
    quarter = HEAD_DIM // 4
    inv = ROPE_BASE ** (-jnp.arange(quarter, dtype=F32) / quarter)
    ar = row[:, None] * inv
    ac = col[:, None] * inv
    ang = jnp.concatenate([ar, ar, ac, ac], axis=-1)
    sign = jnp.tile(jnp.concatenate([-jnp.ones(quarter, F32), jnp.ones(quarter, F32)]), 2)
    cos = jnp.tile(jnp.cos(ang), (1, LANES // HEAD_DIM))
    sin = jnp.tile(jnp.sin(ang) * sign, (1, LANES // HEAD_DIM))
    return cos, sin


def _trunk(x, ctx, mods, norm_g, w_in_ab, w_out_ab, diff_lambda, diff_sub_g, na_rpb, w_in_c, w_out_c,
           attn_sinks, ffn_w1, ffn_w3, ffn_w2, final_g):
    depth = mods.shape[0]
    s_len = x.shape[0]
    tables = _rope_tables(s_len)
    for layer in range(depth):
        ctx_out = layer < depth - 1
        mod_x, mod_c = mods[layer, 0], mods[layer, 1]
        g0 = norm_g[layer, 0].reshape(1, -1)
        g1 = norm_g[layer, 1].reshape(1, -1)
        if layer % 2 == 0:
            e = layer // 2
            lam_init = 0.8 - 0.6 * math.exp(-0.3 * layer)
            w = w_in_ab[e]
            q_cols = DIFF_WIDTH + NA_WIDTH
            a_k, a_v, b_k, b_v = (w[:, q_cols + i * DIFF_WIDTH:q_cols + (i + 1) * DIFF_WIDTH] for i in range(4))
            w = jnp.concatenate([w[:, :q_cols], a_k, b_k, a_v, b_v], axis=1).astype(BF16)
            dims = dict(n_rope_q=DIFF_WIDTH, n_q=q_cols, n_rope_k=DIFF_WIDTH, n_k=q_cols)
            qT, k, vT = _proj_in(x, g0, mod_x, w, tables, **dims)
            qcT, kc, vcT = _proj_in(ctx, g0, mod_c, w, None, **dims)
            lam = diff_lambda[e].astype(F32)
            subg = diff_sub_g[e].astype(F32).reshape(-1, 1)
            bias = _na_bias_blocks(na_rpb[e])
            a_x = _diff_attention(qT, k, vT, kc, vcT, lam, subg, lam_init, has_latent=True)
            b_x = _na_attention(qT, k, vT, kc, vcT, bias, has_latent=True)
            attn_x = [a_x, b_x]
            if ctx_out:
                a_c = _diff_attention(qcT, None, None, kc, vcT, lam, subg, lam_init, has_latent=False)
                b_c = _na_attention(qcT, None, None, kc, vcT, None, has_latent=False)
                attn_c = [a_c, b_c]
            w_out = w_out_ab[e].astype(BF16)
        else:
            o = layer // 2
            w = w_in_c[o].astype(BF16)
            dims = dict(n_rope_q=C_Q_COLS, n_q=C_Q_COLS, n_rope_k=C_KV_COLS, n_k=C_KV_COLS)
            qT, k, vT = _proj_in(x, g0, mod_x, w, tables, **dims)
            qcT, kc, vcT = _proj_in(ctx, g0, mod_c, w, None, **dims)
            attn_x = [_swa_attention(qT, k, vT, kc, vcT, attn_sinks[o], has_latent=True)]
            if ctx_out:
                attn_c = [_swa_attention(qcT, None, None, kc, vcT, attn_sinks[o], has_latent=False)]
            w_out = w_out_c[o].astype(BF16)
        w1, w3, w2 = ffn_w1[layer].astype(BF16), ffn_w3[layer].astype(BF16), ffn_w2[layer].astype(BF16)
        fg = final_g.reshape(1, -1) if layer == depth - 1 else None
        x = _out_ffn(x, attn_x, w_out, mod_x, g1, w1, w3, w2, fg)
        if ctx_out:
            ctx = _out_ffn(ctx, attn_c, w_out, mod_c, g1, w1, w3, w2)
    return x


def kernel(x, c, ctx, c_ctx, ada_w, ada_b, norm_g, w_in_ab, w_out_ab, diff_lambda, diff_sub_g, na_rpb, w_in_c,
           w_out_c, attn_sinks, ffn_w1, ffn_w3, ffn_w2, final_g):
    batch, _, d = x.shape
    depth = ada_w.shape[0]
    outs = []
    for b in range(batch):
        cc = jnp.zeros((8, d), F32).at[0].set(c[b].astype(F32)).at[1].set(c_ctx.astype(F32))
        mods = _adaln(cc, ada_w, ada_b)[:, :2].reshape(depth, 2, 6, d)
        outs.append(_trunk(x[b], ctx[b], mods, norm_g, w_in_ab, w_out_ab, diff_lambda, diff_sub_g, na_rpb,
                           w_in_c, w_out_c, attn_sinks, ffn_w1, ffn_w3, ffn_w2, final_g))
    return jnp.stack(outs).astype(x.dtype)
```

```python
import functools
import math

import jax
import jax.numpy as jnp
from jax import lax
from jax.experimental import pallas as pl
from jax.experimental.pallas import tpu as pltpu

F32 = jnp.float32
BF16 = jnp.bfloat16

D_MODEL = 1024
GRID_W = 64
HEAD_DIM = 64
ROPE_BASE = 10000.0
NORM_EPS = 1e-6
NEG_INF = -1e30
DIFF_HEADS = 4
DIFF_WIDTH = DIFF_HEADS * 2 * HEAD_DIM
NA_HEADS = 8
NA_WIDTH = NA_HEADS * HEAD_DIM
NA_MAX_ROWS = 8
NA_COLS = 16
GQA_HEADS = 16
GQA_KV_HEADS = 4
GQA_GROUP = GQA_HEADS // GQA_KV_HEADS
SWA_WINDOW = 128
C_Q_COLS = GQA_HEADS * HEAD_DIM
C_KV_COLS = GQA_KV_HEADS * HEAD_DIM
LOG2E = math.log2(math.e)
Q_SCALE = HEAD_DIM ** -0.5 * LOG2E

LANES = 128
ONES_ROWS = 16
VMEM_LIMIT = 56 * 1024 * 1024

ROW_TILE = 512
DIFF_TQ = 1024
DIFF_TK = 512
DIFF_QC = 256
DIFF_RC = 64
DIFF_DEPTH = 4
assert DIFF_TK == ROW_TILE
NA_TILE_ROWS = 4
NA_TQ = NA_TILE_ROWS * GRID_W
SWA_TQ = SWA_WINDOW
SWA_ITEM_HEADS = 2
SWA_DEPTH = 2


def _cparams(*sem):
    return pltpu.CompilerParams(dimension_semantics=sem, vmem_limit_bytes=VMEM_LIMIT)


def _resident(shape):
    nd = len(shape)
    return pl.BlockSpec(shape, lambda *_: (0,) * nd, pipeline_mode=pl.Buffered(1))


def _dot(a, b):
    return jnp.dot(a, b, preferred_element_type=F32)


def _adaln_kernel(cc_ref, w_ref, b_ref, o_ref):
    cc = cc_ref[...]
    h = (cc * jax.nn.sigmoid(cc)).astype(BF16)
    o_ref[0] = _dot(h, w_ref[0].astype(BF16)) + b_ref[0]


def _adaln(cc, ada_w, ada_b):
    depth, d, d6 = ada_w.shape
    nj = d6 // d
    return pl.pallas_call(
        _adaln_kernel,
        grid=(depth, nj),
        in_specs=[
            pl.BlockSpec((8, d), lambda l, j: (0, 0)),
            pl.BlockSpec((1, d, d), lambda l, j: (l, 0, j)),
            pl.BlockSpec((1, 1, d), lambda l, j: (l, 0, j)),
        ],
        out_specs=pl.BlockSpec((1, 8, d), lambda l, j: (l, 0, j)),
        out_shape=jax.ShapeDtypeStruct((depth, 8, d6), F32),
        compiler_params=_cparams("arbitrary", "arbitrary"),
        name="adaln",
    )(cc, ada_w, ada_b.reshape(depth, 1, d6))


def _norm_mod(x, g, shift, scale):
    ms = jnp.mean(x * x, axis=-1, keepdims=True)
    h = x * lax.rsqrt(ms + NORM_EPS) * g
    return h * (1.0 + scale) + shift


def _rope_lanes(t, cos, sin_signed):
    lane = lax.broadcasted_iota(jnp.int32, (t.shape[0], LANES), 1)
    first = (lane % (HEAD_DIM // 2)) < (HEAD_DIM // 4)
    out = []
    for c in range(t.shape[1] // LANES):
        tc = t[:, c * LANES:(c + 1) * LANES]
        partner = jnp.where(first, pltpu.roll(tc, LANES - HEAD_DIM // 4, 1), pltpu.roll(tc, HEAD_DIM // 4, 1))
        out.append(tc * cos + partner * sin_signed)
    return jnp.concatenate(out, axis=1)


def _proj_in_kernel(*refs, n_rope_q, n_q, n_rope_k, n_k, rope, n_v_tiled):
    n_in = 6 if rope else 4
    if rope:
        x_ref, g_ref, mod_ref, cos_ref, sin_ref, w_ref = refs[:n_in]
    else:
        x_ref, g_ref, mod_ref, w_ref = refs[:n_in]
    qT_ref, k_ref, vT_ref = refs[n_in:n_in + 3]
    h = _norm_mod(x_ref[...], g_ref[...], mod_ref[0:1, :], mod_ref[1:2, :])
    y = _dot(h.astype(BF16), w_ref[...])
    q = y[:, :n_q]
    k = y[:, n_q:n_q + n_k]
    v = y[:, n_q + n_k:]
    if rope:
        cos, sin = cos_ref[...], sin_ref[...]
        q = jnp.concatenate([_rope_lanes(q[:, :n_rope_q], cos, sin), q[:, n_rope_q:]], axis=1) \
            if n_rope_q < n_q else _rope_lanes(q, cos, sin)
        k = jnp.concatenate([_rope_lanes(k[:, :n_rope_k], cos, sin), k[:, n_rope_k:]], axis=1) \
            if n_rope_k < n_k else _rope_lanes(k, cos, sin)
    qT_ref[...] = (q * Q_SCALE).T.astype(BF16)
    k_ref[...] = k.astype(BF16)
    vT_ref[...] = v[:, n_v_tiled:].T.astype(BF16)
    if n_v_tiled:
        refs[n_in + 3][0] = v[:, :n_v_tiled].T.astype(BF16)


def _proj_in(x, g, mod, w, tables, *, n_rope_q, n_q, n_rope_k, n_k, n_v_tiled=0):
    n, d = x.shape
    cols = w.shape[1]
    n_v = cols - n_q - n_k - n_v_tiled
    tm = min(ROW_TILE, n)
    rope = tables is not None
    row = lambda i: (i, 0)
    in_specs = [pl.BlockSpec((tm, d), row), _resident((1, d)), _resident(mod.shape)]
    args = [x, g, mod]
    if rope:
        in_specs += [pl.BlockSpec((tm, LANES), row), pl.BlockSpec((tm, LANES), row)]
        args += list(tables)
    in_specs.append(_resident(w.shape))
    args.append(w)
    out_specs = [
        pl.BlockSpec((n_q, tm), lambda i: (0, i)),
        pl.BlockSpec((tm, n_k), row),
        pl.BlockSpec((n_v, tm), lambda i: (0, i)),
    ]
    out_shape = [
        jax.ShapeDtypeStruct((n_q, n), BF16),
        jax.ShapeDtypeStruct((n, n_k), BF16),
        jax.ShapeDtypeStruct((n_v, n), BF16),
    ]
    if n_v_tiled:
        out_specs.append(pl.BlockSpec((1, n_v_tiled, tm), lambda i: (i, 0, 0)))
        out_shape.append(jax.ShapeDtypeStruct((n // tm, n_v_tiled, tm), BF16))
    return pl.pallas_call(
        functools.partial(_proj_in_kernel, n_rope_q=n_rope_q, n_q=n_q, n_rope_k=n_rope_k, n_k=n_k, rope=rope,
                          n_v_tiled=n_v_tiled),
        grid=(n // tm,),
        in_specs=in_specs,
        out_specs=out_specs,
        out_shape=out_shape,
        compiler_params=_cparams("arbitrary"),
        name="proj_in",
    )(*args)


def _half_masked(qT_pair, e):
    z = jnp.zeros((HEAD_DIM, qT_pair.shape[1]), qT_pair.dtype)
    if e == 0:
        return jnp.concatenate([qT_pair[:HEAD_DIM], z], axis=0)
    return jnp.concatenate([z, qT_pair[HEAD_DIM:]], axis=0)


def _diff_kernel(*refs, lam_init, n_tiles):
    if n_tiles:
        qT_ref, k_ref, v3_ref, kc_ref, vcT_ref, lam_ref, subg_ref, o_ref = refs[:8]
    else:
        qT_ref, kc_ref, vcT_ref, lam_ref, subg_ref, o_ref = refs[:6]
    qm_ref, m_ref, alpha_ref, l_ref, acc_ref, s_ref, p_ref = refs[-7:]

    tq = qm_ref.shape[2]
    items = [(h, e, c) for h in range(DIFF_HEADS) for e in range(2) for c in range(0, tq, DIFF_QC)]
    n = len(items)
    s_slots = s_ref.shape[0]
    assert n % s_slots == 0 and s_slots > DIFF_DEPTH, (n, s_slots)
    sub = 8

    def keys_of(tile, h):
        lanes = slice(h * LANES, (h + 1) * LANES)
        if tile is None:
            return kc_ref[:, lanes]
        return k_ref[pl.ds(pl.multiple_of(tile * DIFF_TK, DIFF_TK), DIFF_TK), lanes]

    def vals_of(tile, h):
        rows = slice(h * LANES, (h + 1) * LANES)
        return vcT_ref[rows, :] if tile is None else v3_ref[tile, rows, :]

    def scores(i, tile):
        h, e, c = items[i]
        mp = 2 * h + e
        keys = keys_of(tile, h)
        nk = keys.shape[0]
        s = _dot(keys, qm_ref[mp, :, c:c + DIFF_QC])
        s_ref[i % s_slots, :nk, :] = s
        m_part = jnp.max(s.reshape(nk // sub, sub, DIFF_QC), axis=0)
        m_old = m_ref[mp, :, c:c + DIFF_QC]
        m_new = jnp.maximum(m_old, jnp.max(m_part, axis=0, keepdims=True))
        m_ref[mp, :, c:c + DIFF_QC] = m_new
        alpha_ref[mp, :, c:c + DIFF_QC] = jnp.exp2(m_old - m_new)

    def absorb(i, tile):
        h, e, c = items[i]
        mp = 2 * h + e
        cols = slice(c, c + DIFF_QC)
        vals = vals_of(tile, h)
        nk = vals.shape[1]
        m_new = m_ref[mp, :, cols]
        alpha = alpha_ref[mp, :, cols]
        for r in range(0, nk, DIFF_RC):
            p = jnp.exp2(s_ref[i % s_slots, r:r + DIFF_RC, :] - m_new)
            p_ref[i % 2, r:r + DIFF_RC, :] = p.astype(BF16)
        pv = _dot(jnp.concatenate([vals, jnp.ones((ONES_ROWS, nk), BF16)], axis=0), p_ref[i % 2, :nk, :])
        l_ref[mp, :, cols] = alpha * l_ref[mp, :, cols] + pv[LANES:LANES + 1]
        acc_ref[mp, :, cols] = alpha * acc_ref[mp, :, cols] + pv[:LANES]

    def sweep(tile):
        for i in range(min(DIFF_DEPTH, n)):
            scores(i, tile)
        for i in range(n):
            if i + DIFF_DEPTH < n:
                scores(i + DIFF_DEPTH, tile)
            absorb(i, tile)

    for h in range(DIFF_HEADS):
        pair = qT_ref[h * LANES:(h + 1) * LANES, :]
        for e in range(2):
            qm_ref[2 * h + e] = _half_masked(pair, e)
    m_ref[...] = jnp.full(m_ref.shape, NEG_INF, F32)
    l_ref[...] = jnp.zeros(l_ref.shape, F32)
    acc_ref[...] = jnp.zeros(acc_ref.shape, F32)
    sweep(None)
    if n_tiles:
        def body(t, carry):
            sweep(t)
            return carry
        lax.fori_loop(0, n_tiles, body, 0)

    lam = lam_ref[...]
    lam_val = (jnp.exp(jnp.sum(lam[0:1] * lam[1:2], axis=1, keepdims=True))
               - jnp.exp(jnp.sum(lam[2:3] * lam[3:4], axis=1, keepdims=True)) + lam_init)
    subg = subg_ref[...]
    for h in range(DIFF_HEADS):
        a0 = acc_ref[2 * h] / l_ref[2 * h]
        a1 = acc_ref[2 * h + 1] / l_ref[2 * h + 1]
        a = a0 - lam_val * a1
        ms = jnp.mean(a * a, axis=0, keepdims=True)
        a = a * lax.rsqrt(ms + NORM_EPS) * subg * (1.0 - lam_init)
        o_ref[:, h * LANES:(h + 1) * LANES] = a.T.astype(o_ref.dtype)


def _diff_attention(qT, k, v3, kc, vcT, lam, subg, lam_init):
    n = qT.shape[1]
    nc = kc.shape[0]
    w = DIFF_WIDTH
    small = [_resident((4, HEAD_DIM)), _resident((2 * HEAD_DIM, 1))]
    if k is not None:
        tq, tk = DIFF_TQ, DIFF_TK
        n_tiles = v3.shape[0]
        assert v3.shape == (k.shape[0] // tk, w, tk), (v3.shape, k.shape)
        in_specs = [
            pl.BlockSpec((w, tq), lambda i: (0, i)),
            _resident((k.shape[0], w)),
            _resident(v3.shape),
            pl.BlockSpec((nc, w), lambda i: (0, 0)),
            pl.BlockSpec((w, nc), lambda i: (0, 0)),
        ] + small
        args = (qT, k, v3, kc, vcT, lam, subg)
    else:
        tq, tk, n_tiles = n, 0, 0
        in_specs = [
            pl.BlockSpec((w, tq), lambda i: (0, 0)),
            pl.BlockSpec((nc, w), lambda i: (0, 0)),
            pl.BlockSpec((w, nc), lambda i: (0, 0)),
        ] + small
        args = (qT, kc, vcT, lam, subg)
    return pl.pallas_call(
        functools.partial(_diff_kernel, lam_init=lam_init, n_tiles=n_tiles),
        grid=(n // tq,),
        in_specs=in_specs,
        out_specs=pl.BlockSpec((tq, w), lambda i: (i, 0)),
        out_shape=jax.ShapeDtypeStruct((n, w), BF16),
        scratch_shapes=[
            pltpu.VMEM((2 * DIFF_HEADS, LANES, tq), BF16),
            pltpu.VMEM((2 * DIFF_HEADS, 1, tq), F32),
            pltpu.VMEM((2 * DIFF_HEADS, 1, tq), F32),
            pltpu.VMEM((2 * DIFF_HEADS, 1, tq), F32),
            pltpu.VMEM((2 * DIFF_HEADS, LANES, tq), F32),
            pltpu.VMEM((2 * DIFF_DEPTH, max(tk, nc), DIFF_QC), F32),
            pltpu.VMEM((2, max(tk, nc), DIFF_QC), BF16),
        ],
        compiler_params=_cparams("arbitrary"),
        name="diff_attn" if n_tiles else "diff_attn_ctx",
    )(*args)


def _na_bias_kernel(rpb_ref, o_ref):
    h = pl.program_id(0)
    n_dr, n_dc = 2 * NA_MAX_ROWS - 1, 2 * NA_COLS - 1
    kc = lax.broadcasted_iota(jnp.int32, (GRID_W, LANES), 0)
    lane = lax.broadcasted_iota(jnp.int32, (GRID_W, LANES), 1)
    c = lane % GRID_W
    upper = lane >= GRID_W
    dc = kc - c + (NA_COLS - 1)
    cs = jnp.clip(c - NA_COLS // 2, 0, GRID_W - NA_COLS)
    col_ok = (kc >= cs) & (kc < cs + NA_COLS)
    for i in range(o_ref.shape[1]):
        dr_idx = (i - 1, i - 2)
        val = jnp.full((GRID_W, LANES), NEG_INF, F32)
        for d in range(n_dc):
            lo, up = (rpb_ref[(h * n_dr + r) * n_dc + d] * LOG2E if 0 <= r < n_dr else NEG_INF for r in dr_idx)
            val = jnp.where(dc == d, jnp.where(upper, up, lo), val)
        o_ref[0, i] = jnp.where(col_ok, val, NEG_INF)


def _na_bias_blocks(rpb):
    n_heads = rpb.shape[0]
    n_blocks = 2 * NA_MAX_ROWS
    return pl.pallas_call(
        _na_bias_kernel,
        grid=(n_heads,),
        in_specs=[pl.BlockSpec(memory_space=pltpu.SMEM)],
        out_specs=pl.BlockSpec((1, n_blocks, GRID_W, LANES), lambda h: (h, 0, 0, 0)),
        out_shape=jax.ShapeDtypeStruct((n_heads, n_blocks, GRID_W, LANES), F32),
        compiler_params=_cparams("arbitrary"),
        name="na_bias",
    )(rpb.astype(F32).reshape(-1))


def _na_window_bias(bias_ref, e, t, grid_rows):
    lane = lax.broadcasted_iota(jnp.int32, (GRID_W, LANES), 1)
    r0 = t * NA_TILE_ROWS
    rows = []
    for rk in range(3 * NA_TILE_ROWS):
        kr = r0 - NA_TILE_ROWS + rk
        cols = []
        for p in range(NA_TILE_ROWS // 2):
            blk = bias_ref[e, rk - NA_TILE_ROWS - 2 * p + NA_MAX_ROWS]
            r = r0 + 2 * p + lane // GRID_W
            rs = jnp.clip(r - NA_MAX_ROWS // 2, 0, grid_rows - NA_MAX_ROWS)
            ok = (kr >= rs) & (kr < rs + NA_MAX_ROWS)
            cols.append(jnp.where(ok, blk, NEG_INF))
        rows.append(jnp.concatenate(cols, axis=1))
    return jnp.concatenate(rows, axis=0)


def _na_kernel(*refs, has_latent, grid_rows):
    if has_latent:
        qT_ref, km_ref, k0_ref, kp_ref, vm_ref, v0_ref, vp_ref, kc_ref, vcT_ref, bias_ref, o_ref = refs
    else:
        qT_ref, kc_ref, vcT_ref, o_ref = refs
    pair = qT_ref[...]
    kc = kc_ref[...]
    if has_latent:
        vT = jnp.concatenate([vm_ref[...], v0_ref[...], vp_ref[...], vcT_ref[...]], axis=1)
    else:
        vT = vcT_ref[...]
    ones = jnp.ones((ONES_ROWS, vT.shape[1]), BF16)
    scores = []
    for e in range(2):
        qm = _half_masked(pair, e)
        s = _dot(kc, qm)
        if has_latent:
            s_nb = jnp.concatenate([_dot(km_ref[...], qm), _dot(k0_ref[...], qm), _dot(kp_ref[...], qm)], axis=0)
            s = jnp.concatenate([s_nb + _na_window_bias(bias_ref, e, pl.program_id(1), grid_rows), s], axis=0)
        scores.append(s)
    outs = []
    for e in range(2):
        s = scores[e]
        p = jnp.exp2(s - jnp.max(s, axis=0, keepdims=True))
        pv = _dot(jnp.concatenate([vT[e * HEAD_DIM:(e + 1) * HEAD_DIM], ones], axis=0), p.astype(BF16))
        outs.append(pv[:HEAD_DIM] / pv[HEAD_DIM:HEAD_DIM + 1])
    o_ref[...] = jnp.concatenate(outs, axis=0).T.astype(o_ref.dtype)


def _na_attention(qT_all, k_all, vT_all, kc_all, vcT_all, bias, *, has_latent):
    n = qT_all.shape[1]
    nc = kc_all.shape[0]
    g0 = DIFF_WIDTH // LANES
    n_pairs = NA_HEADS // 2
    if has_latent:
        tq = NA_TQ
        nt = n // tq
        prev = lambda t: jnp.maximum(t - 1, 0)
        nxt = lambda t: jnp.minimum(t + 1, nt - 1)
        in_specs = [
            pl.BlockSpec((LANES, tq), lambda hp, t: (g0 + hp, t)),
            pl.BlockSpec((tq, LANES), lambda hp, t: (prev(t), g0 + hp)),
            pl.BlockSpec((tq, LANES), lambda hp, t: (t, g0 + hp)),
            pl.BlockSpec((tq, LANES), lambda hp, t: (nxt(t), g0 + hp)),
            pl.BlockSpec((LANES, tq), lambda hp, t: (hp, prev(t))),
            pl.BlockSpec((LANES, tq), lambda hp, t: (hp, t)),
            pl.BlockSpec((LANES, tq), lambda hp, t: (hp, nxt(t))),
            pl.BlockSpec((nc, LANES), lambda hp, t: (0, g0 + hp)),
            pl.BlockSpec((LANES, nc), lambda hp, t: (g0 + hp, 0)),
            pl.BlockSpec((2,) + bias.shape[1:], lambda hp, t: (hp, 0, 0, 0)),
        ]
        args = (qT_all, k_all, k_all, k_all, vT_all, vT_all, vT_all, kc_all, vcT_all, bias)
    else:
        tq = n
        nt = 1
        in_specs = [
            pl.BlockSpec((LANES, tq), lambda hp, t: (g0 + hp, 0)),
            pl.BlockSpec((nc, LANES), lambda hp, t: (0, g0 + hp)),
            pl.BlockSpec((LANES, nc), lambda hp, t: (g0 + hp, 0)),
        ]
        args = (qT_all, kc_all, vcT_all)
    return pl.pallas_call(
        functools.partial(_na_kernel, has_latent=has_latent, grid_rows=n // GRID_W),
        grid=(n_pairs, nt),
        in_specs=in_specs,
        out_specs=pl.BlockSpec((tq, LANES), lambda hp, t: (t, hp)),
        out_shape=jax.ShapeDtypeStruct((n, NA_WIDTH), BF16),
        compiler_params=_cparams("arbitrary", "arbitrary"),
        name="na_attn" if has_latent else "na_attn_ctx",
    )(*args)


def _swa_kernel(*refs, has_latent):
    if has_latent:
        qT_ref, km_ref, k0_ref, kp_ref, vm_ref, v0_ref, vp_ref, kc_ref, vcT_ref, sink_ref, o_ref = refs
        b = pl.program_id(0)
        nb = pl.num_programs(0)
        tq = qT_ref.shape[1]
        ki = lax.broadcasted_iota(jnp.int32, (tq, tq), 0)
        qi = lax.broadcasted_iota(jnp.int32, (tq, tq), 1)
        band_prev = jnp.concatenate([(ki >= qi) & (b > 0)] * SWA_ITEM_HEADS, axis=1)
        band_next = jnp.concatenate([(ki <= qi) & (b < nb - 1)] * SWA_ITEM_HEADS, axis=1)
    else:
        qT_ref, kc_ref, vcT_ref, sink_ref, o_ref = refs
        tq = qT_ref.shape[1]
    zeros = jnp.zeros((HEAD_DIM, tq), BF16)
    width = SWA_ITEM_HEADS * tq
    items = [(g, j0) for g in range(GQA_KV_HEADS) for j0 in range(0, GQA_GROUP, SWA_ITEM_HEADS)]

    def scores(item):
        g, j0 = item
        lanes = slice((g // 2) * LANES, (g // 2 + 1) * LANES)
        q_cols = []
        for jh in range(j0, j0 + SWA_ITEM_HEADS):
            hq = g * GQA_GROUP + jh
            qh = qT_ref[hq * HEAD_DIM:(hq + 1) * HEAD_DIM, :]
            q_cols.append(jnp.concatenate([qh, zeros] if g % 2 == 0 else [zeros, qh], axis=0))
        qm = jnp.concatenate(q_cols, axis=1)
        s = _dot(kc_ref[:, lanes], qm)
        if has_latent:
            s_prev = jnp.where(band_prev, _dot(km_ref[:, lanes], qm), NEG_INF)
            s_next = jnp.where(band_next, _dot(kp_ref[:, lanes], qm), NEG_INF)
            s = jnp.concatenate([s, s_prev, _dot(k0_ref[:, lanes], qm), s_next], axis=0)
        return s

    def attend(item, s):
        g, j0 = item
        rows = slice(g * HEAD_DIM, (g + 1) * HEAD_DIM)
        vT = vcT_ref[rows, :]
        if has_latent:
            vT = jnp.concatenate([vT, vm_ref[rows, :], v0_ref[rows, :], vp_ref[rows, :]], axis=1)
        sink = sink_ref[g][:, j0 * tq:j0 * tq + width] * LOG2E
        m = jnp.maximum(jnp.max(s, axis=0, keepdims=True), sink)
        p = jnp.exp2(s - m)
        pv = _dot(jnp.concatenate([vT, jnp.ones((ONES_ROWS, vT.shape[1]), BF16)], axis=0), p.astype(BF16))
        o = pv[:HEAD_DIM] / (pv[HEAD_DIM:HEAD_DIM + 1] + jnp.exp2(sink - m))
        return [o[:, jh * tq:(jh + 1) * tq] for jh in range(SWA_ITEM_HEADS)]

    pending = [scores(item) for item in items[:SWA_DEPTH]]
    outs = []
    for i, item in enumerate(items):
        if i + SWA_DEPTH < len(items):
            pending.append(scores(items[i + SWA_DEPTH]))
        outs += attend(item, pending.pop(0))
    o_ref[...] = jnp.concatenate(outs, axis=0).T.astype(o_ref.dtype)


def _swa_attention(qT, k, vT, kc, vcT, sinks, *, has_latent):
    n = qT.shape[1]
    nc = kc.shape[0]
    tq = SWA_TQ
    nb = n // tq
    sink_rows = jnp.repeat(sinks.astype(F32).reshape(GQA_KV_HEADS, 1, GQA_GROUP), tq, axis=2)
    ctx_specs = [_resident((nc, C_KV_COLS)), _resident((C_KV_COLS, nc)), _resident(sink_rows.shape)]
    if has_latent:
        prev = lambda b: jnp.maximum(b - 1, 0)
        nxt = lambda b: jnp.minimum(b + 1, nb - 1)
        in_specs = [
            pl.BlockSpec((C_Q_COLS, tq), lambda b: (0, b)),
            pl.BlockSpec((tq, C_KV_COLS), lambda b: (prev(b), 0)),
            pl.BlockSpec((tq, C_KV_COLS), lambda b: (b, 0)),
            pl.BlockSpec((tq, C_KV_COLS), lambda b: (nxt(b), 0)),
            pl.BlockSpec((C_KV_COLS, tq), lambda b: (0, prev(b))),
            pl.BlockSpec((C_KV_COLS, tq), lambda b: (0, b)),
            pl.BlockSpec((C_KV_COLS, tq), lambda b: (0, nxt(b))),
        ] + ctx_specs
        args = (qT, k, k, k, vT, vT, vT, kc, vcT, sink_rows)
    else:
        in_specs = [pl.BlockSpec((C_Q_COLS, tq), lambda b: (0, b))] + ctx_specs
        args = (qT, kc, vcT, sink_rows)
    return pl.pallas_call(
        functools.partial(_swa_kernel, has_latent=has_latent),
        grid=(nb,),
        in_specs=in_specs,
        out_specs=pl.BlockSpec((tq, C_Q_COLS), lambda b: (b, 0)),
        out_shape=jax.ShapeDtypeStruct((n, C_Q_COLS), BF16),
        compiler_params=_cparams("arbitrary"),
        name="swa_attn" if has_latent else "swa_attn_ctx",
    )(*args)


def _out_ffn_kernel(*refs, n_attn, final):
    x_ref = refs[0]
    attn_refs = refs[1:1 + n_attn]
    wout_ref, mod_ref, g_ref, w1_ref, w3_ref, w2_ref = refs[1 + n_attn:7 + n_attn]
    rest = refs[7 + n_attn:]
    if final:
        fg_ref, o_ref = rest
    else:
        (o_ref,) = rest
    proj = None
    r0 = 0
    for a_ref in attn_refs:
        wdt = a_ref.shape[1]
        part = _dot(a_ref[...], wout_ref[r0:r0 + wdt, :])
        proj = part if proj is None else proj + part
        r0 += wdt
    x1 = x_ref[...] + mod_ref[2:3, :] * proj
    h = _norm_mod(x1, g_ref[...], mod_ref[3:4, :], mod_ref[4:5, :]).astype(BF16)
    hidden = w1_ref.shape[1]
    chunk = hidden // 2
    ffn = None
    for c0 in range(0, hidden, chunk):
        a = _dot(h, w1_ref[:, c0:c0 + chunk])
        b = _dot(h, w3_ref[:, c0:c0 + chunk])
        gated = (a * jax.nn.sigmoid(a) * b).astype(BF16)
        part = _dot(gated, w2_ref[c0:c0 + chunk, :])
        ffn = part if ffn is None else ffn + part
    x2 = x1 + mod_ref[5:6, :] * ffn
    if final:
        ms = jnp.mean(x2 * x2, axis=-1, keepdims=True)
        x2 = x2 * lax.rsqrt(ms + NORM_EPS) * fg_ref[...]
    o_ref[...] = x2


def _out_ffn(x, attn, w_out, mod, g, w1, w3, w2, final_g=None):
    n, d = x.shape
    tm = min(ROW_TILE, n)
    row = lambda i: (i, 0)
    final = final_g is not None
    in_specs = [pl.BlockSpec((tm, d), row)]
    in_specs += [pl.BlockSpec((tm, a.shape[1]), row) for a in attn]
    in_specs += [_resident(w_out.shape), _resident(mod.shape), _resident((1, d)),
                 _resident(w1.shape), _resident(w3.shape), _resident(w2.shape)]
    args = [x, *attn, w_out, mod, g, w1, w3, w2]
    if final:
        in_specs.append(_resident((1, d)))
        args.append(final_g)
    return pl.pallas_call(
        functools.partial(_out_ffn_kernel, n_attn=len(attn), final=final),
        grid=(n // tm,),
        in_specs=in_specs,
        out_specs=pl.BlockSpec((tm, d), row),
        out_shape=jax.ShapeDtypeStruct((n, d), F32),
        compiler_params=_cparams("arbitrary"),
        name="out_ffn",
    )(*args)


def _rope_tables(n):
    t = jnp.arange(n)
    row = (t // GRID_W).astype(F32)
    col = (t % GRID_W).astype(F32)
    quarter = HEAD_DIM // 4
    inv = ROPE_BASE ** (-jnp.arange(quarter, dtype=F32) / quarter)
    ar = row[:, None] * inv
    ac = col[:, None] * inv
    ang = jnp.concatenate([ar, ar, ac, ac], axis=-1)
    sign = jnp.tile(jnp.concatenate([-jnp.ones(quarter, F32), jnp.ones(quarter, F32)]), 2)
    cos = jnp.tile(jnp.cos(ang), (1, LANES // HEAD_DIM))
    sin = jnp.tile(jnp.sin(ang) * sign, (1, LANES // HEAD_DIM))
    return cos, sin


def _trunk(x, ctx, mods, norm_g, w_in_ab, w_out_ab, diff_lambda, diff_sub_g, na_rpb, w_in_c, w_out_c,
           attn_sinks, ffn_w1, ffn_w3, ffn_w2, final_g):
    depth = mods.shape[0]
    s_len = x.shape[0]
    tables = _rope_tables(s_len)
    for layer in range(depth):
        ctx_out = layer < depth - 1
        mod_x, mod_c = mods[layer, 0], mods[layer, 1]
        g0 = norm_g[layer, 0].reshape(1, -1)
        g1 = norm_g[layer, 1].reshape(1, -1)
        if layer % 2 == 0:
            e = layer // 2
            lam_init = 0.8 - 0.6 * math.exp(-0.3 * layer)
            w = w_in_ab[e]
            q_cols = DIFF_WIDTH + NA_WIDTH
            a_k, a_v, b_k, b_v = (w[:, q_cols + i * DIFF_WIDTH:q_cols + (i + 1) * DIFF_WIDTH] for i in range(4))
            w = jnp.concatenate([w[:, :q_cols], a_k, b_k, a_v, b_v], axis=1).astype(BF16)
            dims = dict(n_rope_q=DIFF_WIDTH, n_q=q_cols, n_rope_k=DIFF_WIDTH, n_k=q_cols)
            qT, k, vT, v3 = _proj_in(x, g0, mod_x, w, tables, n_v_tiled=DIFF_WIDTH, **dims)
            qcT, kc, vcT = _proj_in(ctx, g0, mod_c, w, None, **dims)
            lam = diff_lambda[e].astype(F32)
            subg = diff_sub_g[e].astype(F32).reshape(-1, 1)
            bias = _na_bias_blocks(na_rpb[e])
            a_x = _diff_attention(qT, k, v3, kc, vcT, lam, subg, lam_init)
            b_x = _na_attention(qT, k, vT, kc, vcT, bias, has_latent=True)
            attn_x = [a_x, b_x]
            if ctx_out:
                a_c = _diff_attention(qcT, None, None, kc, vcT, lam, subg, lam_init)
                b_c = _na_attention(qcT, None, None, kc, vcT, None, has_latent=False)
                attn_c = [a_c, b_c]
            w_out = w_out_ab[e].astype(BF16)
        else:
            o = layer // 2
            w = w_in_c[o].astype(BF16)
            dims = dict(n_rope_q=C_Q_COLS, n_q=C_Q_COLS, n_rope_k=C_KV_COLS, n_k=C_KV_COLS)
            qT, k, vT = _proj_in(x, g0, mod_x, w, tables, **dims)
            qcT, kc, vcT = _proj_in(ctx, g0, mod_c, w, None, **dims)
            attn_x = [_swa_attention(qT, k, vT, kc, vcT, attn_sinks[o], has_latent=True)]
            if ctx_out:
                attn_c = [_swa_attention(qcT, None, None, kc, vcT, attn_sinks[o], has_latent=False)]
            w_out = w_out_c[o].astype(BF16)
        w1, w3, w2 = ffn_w1[layer].astype(BF16), ffn_w3[layer].astype(BF16), ffn_w2[layer].astype(BF16)
        fg = final_g.reshape(1, -1) if layer == depth - 1 else None
        x = _out_ffn(x, attn_x, w_out, mod_x, g1, w1, w3, w2, fg)
        if ctx_out:
            ctx = _out_ffn(ctx, attn_c, w_out, mod_c, g1, w1, w3, w2)
    return x


def kernel(x, c, ctx, c_ctx, ada_w, ada_b, norm_g, w_in_ab, w_out_ab, diff_lambda, diff_sub_g, na_rpb, w_in_c,
           w_out_c, attn_sinks, ffn_w1, ffn_w3, ffn_w2, final_g):
    batch, _, d = x.shape
    depth = ada_w.shape[0]
    outs = []
    for b in range(batch):
        cc = jnp.zeros((8, d), F32).at[0].set(c[b].astype(F32)).at[1].set(c_ctx.astype(F32))
        mods = _adaln(cc, ada_w, ada_b)[:, :2].reshape(depth, 2, 6, d)
        outs.append(_trunk(x[b], ctx[b], mods, norm_g, w_in_ab, w_out_ab, diff_lambda, diff_sub_g, na_rpb,
                           w_in_c, w_out_c, attn_sinks, ffn_w1, ffn_w3, ffn_w2, final_g))
    return jnp.stack(outs).astype(x.dtype)
```

```python
import functools
import math

import jax
import jax.numpy as jnp
from jax import lax
from jax.experimental import pallas as pl
from jax.experimental.pallas import tpu as pltpu

F32 = jnp.float32
BF16 = jnp.bfloat16

D_MODEL = 1024
GRID_W = 64
HEAD_DIM = 64
ROPE_BASE = 10000.0
NORM_EPS = 1e-6
NEG_INF = -1e30
DIFF_HEADS = 4
DIFF_WIDTH = DIFF_HEADS * 2 * HEAD_DIM
NA_HEADS = 8
NA_WIDTH = NA_HEADS * HEAD_DIM
NA_MAX_ROWS = 8
NA_COLS = 16
GQA_HEADS = 16
GQA_KV_HEADS = 4
GQA_GROUP = GQA_HEADS // GQA_KV_HEADS
SWA_WINDOW = 128
C_Q_COLS = GQA_HEADS * HEAD_DIM
C_KV_COLS = GQA_KV_HEADS * HEAD_DIM
LOG2E = math.log2(math.e)
Q_SCALE = HEAD_DIM ** -0.5 * LOG2E

LANES = 128
ONES_ROWS = 16
VMEM_LIMIT = 56 * 1024 * 1024

ROW_TILE = 512
DIFF_TQ = 1024
DIFF_TK = 512
DIFF_QC = 256
DIFF_RC = 64
DIFF_DEPTH = 4
assert DIFF_TK == ROW_TILE
NA_TILE_ROWS = 4
NA_TQ = NA_TILE_ROWS * GRID_W
NA_STEP_PAIRS = 2
NA_DEPTH = 2
SWA_TQ = SWA_WINDOW
SWA_ITEM_HEADS = 2
SWA_DEPTH = 3


def _cparams(*sem):
    return pltpu.CompilerParams(dimension_semantics=sem, vmem_limit_bytes=VMEM_LIMIT)


def _resident(shape):
    nd = len(shape)
    return pl.BlockSpec(shape, lambda *_: (0,) * nd, pipeline_mode=pl.Buffered(1))


def _dot(a, b):
    return jnp.dot(a, b, preferred_element_type=F32)


def _adaln_kernel(cc_ref, w_ref, b_ref, o_ref):
    cc = cc_ref[...]
    h = (cc * jax.nn.sigmoid(cc)).astype(BF16)
    o_ref[0] = _dot(h, w_ref[0].astype(BF16)) + b_ref[0]


def _adaln(cc, ada_w, ada_b):
    depth, d, d6 = ada_w.shape
    nj = d6 // d
    return pl.pallas_call(
        _adaln_kernel,
        grid=(depth, nj),
        in_specs=[
            pl.BlockSpec((8, d), lambda l, j: (0, 0)),
            pl.BlockSpec((1, d, d), lambda l, j: (l, 0, j)),
            pl.BlockSpec((1, 1, d), lambda l, j: (l, 0, j)),
        ],
        out_specs=pl.BlockSpec((1, 8, d), lambda l, j: (l, 0, j)),
        out_shape=jax.ShapeDtypeStruct((depth, 8, d6), F32),
        compiler_params=_cparams("arbitrary", "arbitrary"),
        name="adaln",
    )(cc, ada_w, ada_b.reshape(depth, 1, d6))


def _norm_mod(x, g, shift, scale):
    ms = jnp.mean(x * x, axis=-1, keepdims=True)
    h = x * lax.rsqrt(ms + NORM_EPS) * g
    return h * (1.0 + scale) + shift


def _rope_lanes(t, cos, sin_signed):
    lane = lax.broadcasted_iota(jnp.int32, (t.shape[0], LANES), 1)
    first = (lane % (HEAD_DIM // 2)) < (HEAD_DIM // 4)
    out = []
    for c in range(t.shape[1] // LANES):
        tc = t[:, c * LANES:(c + 1) * LANES]
        partner = jnp.where(first, pltpu.roll(tc, LANES - HEAD_DIM // 4, 1), pltpu.roll(tc, HEAD_DIM // 4, 1))
        out.append(tc * cos + partner * sin_signed)
    return jnp.concatenate(out, axis=1)


def _proj_in_kernel(*refs, n_rope_q, n_q, n_rope_k, n_k, rope, n_v_tiled):
    n_in = 6 if rope else 4
    if rope:
        x_ref, g_ref, mod_ref, cos_ref, sin_ref, w_ref = refs[:n_in]
    else:
        x_ref, g_ref, mod_ref, w_ref = refs[:n_in]
    qT_ref, k_ref, vT_ref = refs[n_in:n_in + 3]
    h = _norm_mod(x_ref[...], g_ref[...], mod_ref[0:1, :], mod_ref[1:2, :])
    y = _dot(h.astype(BF16), w_ref[...])
    q = y[:, :n_q]
    k = y[:, n_q:n_q + n_k]
    v = y[:, n_q + n_k:]
    if rope:
        cos, sin = cos_ref[...], sin_ref[...]
        q = jnp.concatenate([_rope_lanes(q[:, :n_rope_q], cos, sin), q[:, n_rope_q:]], axis=1) \
            if n_rope_q < n_q else _rope_lanes(q, cos, sin)
        k = jnp.concatenate([_rope_lanes(k[:, :n_rope_k], cos, sin), k[:, n_rope_k:]], axis=1) \
            if n_rope_k < n_k else _rope_lanes(k, cos, sin)
    qT_ref[...] = (q * Q_SCALE).T.astype(BF16)
    k_ref[...] = k.astype(BF16)
    vT_ref[...] = v[:, n_v_tiled:].T.astype(BF16)
    if n_v_tiled:
        refs[n_in + 3][0] = v[:, :n_v_tiled].T.astype(BF16)


def _proj_in(x, g, mod, w, tables, *, n_rope_q, n_q, n_rope_k, n_k, n_v_tiled=0):
    n, d = x.shape
    cols = w.shape[1]
    n_v = cols - n_q - n_k - n_v_tiled
    tm = min(ROW_TILE, n)
    rope = tables is not None
    row = lambda i: (i, 0)
    in_specs = [pl.BlockSpec((tm, d), row), _resident((1, d)), _resident(mod.shape)]
    args = [x, g, mod]
    if rope:
        in_specs += [pl.BlockSpec((tm, LANES), row), pl.BlockSpec((tm, LANES), row)]
        args += list(tables)
    in_specs.append(_resident(w.shape))
    args.append(w)
    out_specs = [
        pl.BlockSpec((n_q, tm), lambda i: (0, i)),
        pl.BlockSpec((tm, n_k), row),
        pl.BlockSpec((n_v, tm), lambda i: (0, i)),
    ]
    out_shape = [
        jax.ShapeDtypeStruct((n_q, n), BF16),
        jax.ShapeDtypeStruct((n, n_k), BF16),
        jax.ShapeDtypeStruct((n_v, n), BF16),
    ]
    if n_v_tiled:
        out_specs.append(pl.BlockSpec((1, n_v_tiled, tm), lambda i: (i, 0, 0)))
        out_shape.append(jax.ShapeDtypeStruct((n // tm, n_v_tiled, tm), BF16))
    return pl.pallas_call(
        functools.partial(_proj_in_kernel, n_rope_q=n_rope_q, n_q=n_q, n_rope_k=n_rope_k, n_k=n_k, rope=rope,
                          n_v_tiled=n_v_tiled),
        grid=(n // tm,),
        in_specs=in_specs,
        out_specs=out_specs,
        out_shape=out_shape,
        compiler_params=_cparams("arbitrary"),
        name="proj_in",
    )(*args)


def _half_masked(qT_pair, e):
    z = jnp.zeros((HEAD_DIM, qT_pair.shape[1]), qT_pair.dtype)
    if e == 0:
        return jnp.concatenate([qT_pair[:HEAD_DIM], z], axis=0)
    return jnp.concatenate([z, qT_pair[HEAD_DIM:]], axis=0)


def _diff_kernel(*refs, lam_init, n_tiles):
    if n_tiles:
        qT_ref, k_ref, v3_ref, kc_ref, vcT_ref, lam_ref, subg_ref, o_ref = refs[:8]
    else:
        qT_ref, kc_ref, vcT_ref, lam_ref, subg_ref, o_ref = refs[:6]
    qm_ref, m_ref, alpha_ref, l_ref, acc_ref, s_ref, p_ref = refs[-7:]

    tq = qm_ref.shape[2]
    items = [(h, e, c) for h in range(DIFF_HEADS) for e in range(2) for c in range(0, tq, DIFF_QC)]
    n = len(items)
    s_slots = s_ref.shape[0]
    assert n % s_slots == 0 and s_slots > DIFF_DEPTH, (n, s_slots)
    sub = 8

    def keys_of(tile, h):
        lanes = slice(h * LANES, (h + 1) * LANES)
        if tile is None:
            return kc_ref[:, lanes]
        return k_ref[pl.ds(pl.multiple_of(tile * DIFF_TK, DIFF_TK), DIFF_TK), lanes]

    def vals_of(tile, h):
        rows = slice(h * LANES, (h + 1) * LANES)
        return vcT_ref[rows, :] if tile is None else v3_ref[tile, rows, :]

    def scores(i, tile):
        h, e, c = items[i]
        mp = 2 * h + e
        keys = keys_of(tile, h)
        nk = keys.shape[0]
        s = _dot(keys, qm_ref[mp, :, c:c + DIFF_QC])
        s_ref[i % s_slots, :nk, :] = s
        m_part = jnp.max(s.reshape(nk // sub, sub, DIFF_QC), axis=0)
        m_old = m_ref[mp, :, c:c + DIFF_QC]
        m_new = jnp.maximum(m_old, jnp.max(m_part, axis=0, keepdims=True))
        m_ref[mp, :, c:c + DIFF_QC] = m_new
        alpha_ref[mp, :, c:c + DIFF_QC] = jnp.exp2(m_old - m_new)

    def absorb(i, tile):
        h, e, c = items[i]
        mp = 2 * h + e
        cols = slice(c, c + DIFF_QC)
        vals = vals_of(tile, h)
        nk = vals.shape[1]
        m_new = m_ref[mp, :, cols]
        alpha = alpha_ref[mp, :, cols]
        for r in range(0, nk, DIFF_RC):
            p = jnp.exp2(s_ref[i % s_slots, r:r + DIFF_RC, :] - m_new)
            p_ref[i % 2, r:r + DIFF_RC, :] = p.astype(BF16)
        pv = _dot(jnp.concatenate([vals, jnp.ones((ONES_ROWS, nk), BF16)], axis=0), p_ref[i % 2, :nk, :])
        l_ref[mp, :, cols] = alpha * l_ref[mp, :, cols] + pv[LANES:LANES + 1]
        acc_ref[mp, :, cols] = alpha * acc_ref[mp, :, cols] + pv[:LANES]

    def sweep(tile):
        for i in range(min(DIFF_DEPTH, n)):
            scores(i, tile)
        for i in range(n):
            if i + DIFF_DEPTH < n:
                scores(i + DIFF_DEPTH, tile)
            absorb(i, tile)

    for h in range(DIFF_HEADS):
        pair = qT_ref[h * LANES:(h + 1) * LANES, :]
        for e in range(2):
            qm_ref[2 * h + e] = _half_masked(pair, e)
    m_ref[...] = jnp.full(m_ref.shape, NEG_INF, F32)
    l_ref[...] = jnp.zeros(l_ref.shape, F32)
    acc_ref[...] = jnp.zeros(acc_ref.shape, F32)
    sweep(None)
    if n_tiles:
        def body(t, carry):
            sweep(t)
            return carry
        lax.fori_loop(0, n_tiles, body, 0)

    lam = lam_ref[...]
    lam_val = (jnp.exp(jnp.sum(lam[0:1] * lam[1:2], axis=1, keepdims=True))
               - jnp.exp(jnp.sum(lam[2:3] * lam[3:4], axis=1, keepdims=True)) + lam_init)
    subg = subg_ref[...]
    for h in range(DIFF_HEADS):
        a0 = acc_ref[2 * h] / l_ref[2 * h]
        a1 = acc_ref[2 * h + 1] / l_ref[2 * h + 1]
        a = a0 - lam_val * a1
        ms = jnp.mean(a * a, axis=0, keepdims=True)
        a = a * lax.rsqrt(ms + NORM_EPS) * subg * (1.0 - lam_init)
        o_ref[:, h * LANES:(h + 1) * LANES] = a.T.astype(o_ref.dtype)


def _diff_attention(qT, k, v3, kc, vcT, lam, subg, lam_init):
    n = qT.shape[1]
    nc = kc.shape[0]
    w = DIFF_WIDTH
    small = [_resident((4, HEAD_DIM)), _resident((2 * HEAD_DIM, 1))]
    if k is not None:
        tq, tk = DIFF_TQ, DIFF_TK
        n_tiles = v3.shape[0]
        assert v3.shape == (k.shape[0] // tk, w, tk), (v3.shape, k.shape)
        in_specs = [
            pl.BlockSpec((w, tq), lambda i: (0, i)),
            _resident((k.shape[0], w)),
            _resident(v3.shape),
            pl.BlockSpec((nc, w), lambda i: (0, 0)),
            pl.BlockSpec((w, nc), lambda i: (0, 0)),
        ] + small
        args = (qT, k, v3, kc, vcT, lam, subg)
    else:
        tq, tk, n_tiles = n, 0, 0
        in_specs = [
            pl.BlockSpec((w, tq), lambda i: (0, 0)),
            pl.BlockSpec((nc, w), lambda i: (0, 0)),
            pl.BlockSpec((w, nc), lambda i: (0, 0)),
        ] + small
        args = (qT, kc, vcT, lam, subg)
    return pl.pallas_call(
        functools.partial(_diff_kernel, lam_init=lam_init, n_tiles=n_tiles),
        grid=(n // tq,),
        in_specs=in_specs,
        out_specs=pl.BlockSpec((tq, w), lambda i: (i, 0)),
        out_shape=jax.ShapeDtypeStruct((n, w), BF16),
        scratch_shapes=[
            pltpu.VMEM((2 * DIFF_HEADS, LANES, tq), BF16),
            pltpu.VMEM((2 * DIFF_HEADS, 1, tq), F32),
            pltpu.VMEM((2 * DIFF_HEADS, 1, tq), F32),
            pltpu.VMEM((2 * DIFF_HEADS, 1, tq), F32),
            pltpu.VMEM((2 * DIFF_HEADS, LANES, tq), F32),
            pltpu.VMEM((2 * DIFF_DEPTH, max(tk, nc), DIFF_QC), F32),
            pltpu.VMEM((2, max(tk, nc), DIFF_QC), BF16),
        ],
        compiler_params=_cparams("arbitrary"),
        name="diff_attn" if n_tiles else "diff_attn_ctx",
    )(*args)


def _na_bias_kernel(rpb_ref, o_ref):
    h = pl.program_id(0)
    n_dr, n_dc = 2 * NA_MAX_ROWS - 1, 2 * NA_COLS - 1
    kc = lax.broadcasted_iota(jnp.int32, (GRID_W, LANES), 0)
    lane = lax.broadcasted_iota(jnp.int32, (GRID_W, LANES), 1)
    c = lane % GRID_W
    upper = lane >= GRID_W
    dc = kc - c + (NA_COLS - 1)
    cs = jnp.clip(c - NA_COLS // 2, 0, GRID_W - NA_COLS)
    col_ok = (kc >= cs) & (kc < cs + NA_COLS)
    for i in range(o_ref.shape[1]):
        dr_idx = (i - 1, i - 2)
        val = jnp.full((GRID_W, LANES), NEG_INF, F32)
        for d in range(n_dc):
            lo, up = (rpb_ref[(h * n_dr + r) * n_dc + d] * LOG2E if 0 <= r < n_dr else NEG_INF for r in dr_idx)
            val = jnp.where(dc == d, jnp.where(upper, up, lo), val)
        o_ref[0, i] = jnp.where(col_ok, val, NEG_INF)


def _na_bias_blocks(rpb):
    n_heads = rpb.shape[0]
    n_blocks = 2 * NA_MAX_ROWS
    return pl.pallas_call(
        _na_bias_kernel,
        grid=(n_heads,),
        in_specs=[pl.BlockSpec(memory_space=pltpu.SMEM)],
        out_specs=pl.BlockSpec((1, n_blocks, GRID_W, LANES), lambda h: (h, 0, 0, 0)),
        out_shape=jax.ShapeDtypeStruct((n_heads, n_blocks, GRID_W, LANES), F32),
        compiler_params=_cparams("arbitrary"),
        name="na_bias",
    )(rpb.astype(F32).reshape(-1))


def _na_window_bias(bias_ref, e, t, grid_rows):
    lane = lax.broadcasted_iota(jnp.int32, (GRID_W, LANES), 1)
    r0 = t * NA_TILE_ROWS
    rows = []
    for rk in range(3 * NA_TILE_ROWS):
        kr = r0 - NA_TILE_ROWS + rk
        cols = []
        for p in range(NA_TILE_ROWS // 2):
            blk = bias_ref[e, rk - NA_TILE_ROWS - 2 * p + NA_MAX_ROWS]
            r = r0 + 2 * p + lane // GRID_W
            rs = jnp.clip(r - NA_MAX_ROWS // 2, 0, grid_rows - NA_MAX_ROWS)
            ok = (kr >= rs) & (kr < rs + NA_MAX_ROWS)
            cols.append(jnp.where(ok, blk, NEG_INF))
        rows.append(jnp.concatenate(cols, axis=1))
    return jnp.concatenate(rows, axis=0)


def _na_kernel(*refs, has_latent, grid_rows):
    if has_latent:
        qT_ref, km_ref, k0_ref, kp_ref, vm_ref, v0_ref, vp_ref, kc_ref, vcT_ref, bias_ref, o_ref = refs
    else:
        qT_ref, kc_ref, vcT_ref, o_ref = refs
    items = [(pp, e) for pp in range(NA_STEP_PAIRS) for e in range(2)]

    def scores(item):
        pp, e = item
        lanes = slice(pp * LANES, (pp + 1) * LANES)
        qm = _half_masked(qT_ref[lanes, :], e)
        s = _dot(kc_ref[:, lanes], qm)
        if has_latent:
            s_nb = jnp.concatenate([_dot(r[:, lanes], qm) for r in (km_ref, k0_ref, kp_ref)], axis=0)
            s = jnp.concatenate([s_nb + _na_window_bias(bias_ref, 2 * pp + e, pl.program_id(1), grid_rows), s], axis=0)
        return s

    def attend(item, s):
        pp, e = item
        rows = slice(pp * LANES + e * HEAD_DIM, pp * LANES + (e + 1) * HEAD_DIM)
        vT = vcT_ref[rows, :]
        if has_latent:
            vT = jnp.concatenate([vm_ref[rows, :], v0_ref[rows, :], vp_ref[rows, :], vT], axis=1)
        p = jnp.exp2(s - jnp.max(s, axis=0, keepdims=True))
        pv = _dot(jnp.concatenate([vT, jnp.ones((ONES_ROWS, vT.shape[1]), BF16)], axis=0), p.astype(BF16))
        return pv[:HEAD_DIM] / pv[HEAD_DIM:HEAD_DIM + 1]

    pending = [scores(item) for item in items[:NA_DEPTH]]
    outs = []
    for i, item in enumerate(items):
        if i + NA_DEPTH < len(items):
            pending.append(scores(items[i + NA_DEPTH]))
        outs.append(attend(item, pending.pop(0)))
    o_ref[...] = jnp.concatenate(outs, axis=0).T.astype(o_ref.dtype)


def _na_attention(qT_all, k_all, vT_all, kc_all, vcT_all, bias, *, has_latent):
    n = qT_all.shape[1]
    nc = kc_all.shape[0]
    w = NA_STEP_PAIRS * LANES
    g0 = DIFF_WIDTH // w
    n_steps = NA_WIDTH // w
    if has_latent:
        tq = NA_TQ
        nt = n // tq
        prev = lambda t: jnp.maximum(t - 1, 0)
        nxt = lambda t: jnp.minimum(t + 1, nt - 1)
        in_specs = [
            pl.BlockSpec((w, tq), lambda hp, t: (g0 + hp, t)),
            pl.BlockSpec((tq, w), lambda hp, t: (prev(t), g0 + hp)),
            pl.BlockSpec((tq, w), lambda hp, t: (t, g0 + hp)),
            pl.BlockSpec((tq, w), lambda hp, t: (nxt(t), g0 + hp)),
            pl.BlockSpec((w, tq), lambda hp, t: (hp, prev(t))),
            pl.BlockSpec((w, tq), lambda hp, t: (hp, t)),
            pl.BlockSpec((w, tq), lambda hp, t: (hp, nxt(t))),
            pl.BlockSpec((nc, w), lambda hp, t: (0, g0 + hp)),
            pl.BlockSpec((w, nc), lambda hp, t: (g0 + hp, 0)),
            pl.BlockSpec((2 * NA_STEP_PAIRS,) + bias.shape[1:], lambda hp, t: (hp, 0, 0, 0)),
        ]
        args = (qT_all, k_all, k_all, k_all, vT_all, vT_all, vT_all, kc_all, vcT_all, bias)
    else:
        tq = n
        nt = 1
        in_specs = [
            pl.BlockSpec((w, tq), lambda hp, t: (g0 + hp, 0)),
            pl.BlockSpec((nc, w), lambda hp, t: (0, g0 + hp)),
            pl.BlockSpec((w, nc), lambda hp, t: (g0 + hp, 0)),
        ]
        args = (qT_all, kc_all, vcT_all)
    return pl.pallas_call(
        functools.partial(_na_kernel, has_latent=has_latent, grid_rows=n // GRID_W),
        grid=(n_steps, nt),
        in_specs=in_specs,
        out_specs=pl.BlockSpec((tq, w), lambda hp, t: (t, hp)),
        out_shape=jax.ShapeDtypeStruct((n, NA_WIDTH), BF16),
        compiler_params=_cparams("arbitrary", "arbitrary"),
        name="na_attn" if has_latent else "na_attn_ctx",
    )(*args)


def _swa_kernel(*refs, has_latent):
    if has_latent:
        qT_ref, km_ref, k0_ref, kp_ref, vm_ref, v0_ref, vp_ref, kc_ref, vcT_ref, sink_ref, o_ref = refs
        b = pl.program_id(0)
        nb = pl.num_programs(0)
        tq = qT_ref.shape[1]
        ki = lax.broadcasted_iota(jnp.int32, (tq, tq), 0)
        qi = lax.broadcasted_iota(jnp.int32, (tq, tq), 1)
        band_prev = jnp.concatenate([(ki >= qi) & (b > 0)] * SWA_ITEM_HEADS, axis=1)
        band_next = jnp.concatenate([(ki <= qi) & (b < nb - 1)] * SWA_ITEM_HEADS, axis=1)
    else:
        qT_ref, kc_ref, vcT_ref, sink_ref, o_ref = refs
        tq = qT_ref.shape[1]
    zeros = jnp.zeros((HEAD_DIM, tq), BF16)
    width = SWA_ITEM_HEADS * tq
    items = [(g, j0) for g in range(GQA_KV_HEADS) for j0 in range(0, GQA_GROUP, SWA_ITEM_HEADS)]

    def scores(item):
        g, j0 = item
        lanes = slice((g // 2) * LANES, (g // 2 + 1) * LANES)
        q_cols = []
        for jh in range(j0, j0 + SWA_ITEM_HEADS):
            hq = g * GQA_GROUP + jh
            qh = qT_ref[hq * HEAD_DIM:(hq + 1) * HEAD_DIM, :]
            q_cols.append(jnp.concatenate([qh, zeros] if g % 2 == 0 else [zeros, qh], axis=0))
        qm = jnp.concatenate(q_cols, axis=1)
        s = _dot(kc_ref[:, lanes], qm)
        if has_latent:
            s_prev = jnp.where(band_prev, _dot(km_ref[:, lanes], qm), NEG_INF)
            s_next = jnp.where(band_next, _dot(kp_ref[:, lanes], qm), NEG_INF)
            s = jnp.concatenate([s, s_prev, _dot(k0_ref[:, lanes], qm), s_next], axis=0)
        return s

    def attend(item, s):
        g, j0 = item
        rows = slice(g * HEAD_DIM, (g + 1) * HEAD_DIM)
        vT = vcT_ref[rows, :]
        if has_latent:
            vT = jnp.concatenate([vT, vm_ref[rows, :], v0_ref[rows, :], vp_ref[rows, :]], axis=1)
        sink = sink_ref[g][:, j0 * tq:j0 * tq + width] * LOG2E
        m = jnp.maximum(jnp.max(s, axis=0, keepdims=True), sink)
        p = jnp.exp2(s - m)
        pv = _dot(jnp.concatenate([vT, jnp.ones((ONES_ROWS, vT.shape[1]), BF16)], axis=0), p.astype(BF16))
        o = pv[:HEAD_DIM] / (pv[HEAD_DIM:HEAD_DIM + 1] + jnp.exp2(sink - m))
        return [o[:, jh * tq:(jh + 1) * tq] for jh in range(SWA_ITEM_HEADS)]

    pending = [scores(item) for item in items[:SWA_DEPTH]]
    outs = []
    for i, item in enumerate(items):
        if i + SWA_DEPTH < len(items):
            pending.append(scores(items[i + SWA_DEPTH]))
        outs += attend(item, pending.pop(0))
    o_ref[...] = jnp.concatenate(outs, axis=0).T.astype(o_ref.dtype)


def _swa_attention(qT, k, vT, kc, vcT, sinks, *, has_latent):
    n = qT.shape[1]
    nc = kc.shape[0]
    tq = SWA_TQ
    nb = n // tq
    sink_rows = jnp.repeat(sinks.astype(F32).reshape(GQA_KV_HEADS, 1, GQA_GROUP), tq, axis=2)
    ctx_specs = [_resident((nc, C_KV_COLS)), _resident((C_KV_COLS, nc)), _resident(sink_rows.shape)]
    if has_latent:
        prev = lambda b: jnp.maximum(b - 1, 0)
        nxt = lambda b: jnp.minimum(b + 1, nb - 1)
        in_specs = [
            pl.BlockSpec((C_Q_COLS, tq), lambda b: (0, b)),
            pl.BlockSpec((tq, C_KV_COLS), lambda b: (prev(b), 0)),
            pl.BlockSpec((tq, C_KV_COLS), lambda b: (b, 0)),
            pl.BlockSpec((tq, C_KV_COLS), lambda b: (nxt(b), 0)),
            pl.BlockSpec((C_KV_COLS, tq), lambda b: (0, prev(b))),
            pl.BlockSpec((C_KV_COLS, tq), lambda b: (0, b)),
            pl.BlockSpec((C_KV_COLS, tq), lambda b: (0, nxt(b))),
        ] + ctx_specs
        args = (qT, k, k, k, vT, vT, vT, kc, vcT, sink_rows)
    else:
        in_specs = [pl.BlockSpec((C_Q_COLS, tq), lambda b: (0, b))] + ctx_specs
        args = (qT, kc, vcT, sink_rows)
    return pl.pallas_call(
        functools.partial(_swa_kernel, has_latent=has_latent),
        grid=(nb,),
        in_specs=in_specs,
        out_specs=pl.BlockSpec((tq, C_Q_COLS), lambda b: (b, 0)),
        out_shape=jax.ShapeDtypeStruct((n, C_Q_COLS), BF16),
        compiler_params=_cparams("arbitrary"),
        name="swa_attn" if has_latent else "swa_attn_ctx",
    )(*args)


def _out_ffn_kernel(*refs, n_attn, final):
    x_ref = refs[0]
    attn_refs = refs[1:1 + n_attn]
    wout_ref, mod_ref, g_ref, w1_ref, w3_ref, w2_ref = refs[1 + n_attn:7 + n_attn]
    rest = refs[7 + n_attn:]
    if final:
        fg_ref, o_ref = rest
    else:
        (o_ref,) = rest
    proj = None
    r0 = 0
    for a_ref in attn_refs:
        wdt = a_ref.shape[1]
        part = _dot(a_ref[...], wout_ref[r0:r0 + wdt, :])
        proj = part if proj is None else proj + part
        r0 += wdt
    x1 = x_ref[...] + mod_ref[2:3, :] * proj
    h = _norm_mod(x1, g_ref[...], mod_ref[3:4, :], mod_ref[4:5, :]).astype(BF16)
    hidden = w1_ref.shape[1]
    chunk = hidden // 2
    ffn = None
    for c0 in range(0, hidden, chunk):
        a = _dot(h, w1_ref[:, c0:c0 + chunk])
        b = _dot(h, w3_ref[:, c0:c0 + chunk])
        gated = (a * jax.nn.sigmoid(a) * b).astype(BF16)
        part = _dot(gated, w2_ref[c0:c0 + chunk, :])
        ffn = part if ffn is None else ffn + part
    x2 = x1 + mod_ref[5:6, :] * ffn
    if final:
        ms = jnp.mean(x2 * x2, axis=-1, keepdims=True)
        x2 = x2 * lax.rsqrt(ms + NORM_EPS) * fg_ref[...]
    o_ref[...] = x2


def _out_ffn(x, attn, w_out, mod, g, w1, w3, w2, final_g=None):
    n, d = x.shape
    tm = min(ROW_TILE, n)
    row = lambda i: (i, 0)
    final = final_g is not None
    in_specs = [pl.BlockSpec((tm, d), row)]
    in_specs += [pl.BlockSpec((tm, a.shape[1]), row) for a in attn]
    in_specs += [_resident(w_out.shape), _resident(mod.shape), _resident((1, d)),
                 _resident(w1.shape), _resident(w3.shape), _resident(w2.shape)]
    args = [x, *attn, w_out, mod, g, w1, w3, w2]
    if final:
        in_specs.append(_resident((1, d)))
        args.append(final_g)
    return pl.pallas_call(
        functools.partial(_out_ffn_kernel, n_attn=len(attn), final=final),
        grid=(n // tm,),
        in_specs=in_specs,
        out_specs=pl.BlockSpec((tm, d), row),
        out_shape=jax.ShapeDtypeStruct((n, d), F32),
        compiler_params=_cparams("arbitrary"),
        name="out_ffn",
    )(*args)


def _rope_tables(n):
    t = jnp.arange(n)
    row = (t // GRID_W).astype(F32)
    col = (t % GRID_W).astype(F32)
    quarter = HEAD_DIM // 4
    inv = ROPE_BASE ** (-jnp.arange(quarter, dtype=F32) / quarter)
    ar = row[:, None] * inv
    ac = col[:, None] * inv
    ang = jnp.concatenate([ar, ar, ac, ac], axis=-1)
    sign = jnp.tile(jnp.concatenate([-jnp.ones(quarter, F32), jnp.ones(quarter, F32)]), 2)
    cos = jnp.tile(jnp.cos(ang), (1, LANES // HEAD_DIM))
    sin = jnp.tile(jnp.sin(ang) * sign, (1, LANES // HEAD_DIM))
    return cos, sin


def _trunk(x, ctx, mods, norm_g, w_in_ab, w_out_ab, diff_lambda, diff_sub_g, na_rpb, w_in_c, w_out_c,
           attn_sinks, ffn_w1, ffn_w3, ffn_w2, final_g):
    depth = mods.shape[0]
    s_len = x.shape[0]
    tables = _rope_tables(s_len)
    for layer in range(depth):
        ctx_out = layer < depth - 1
        mod_x, mod_c = mods[layer, 0], mods[layer, 1]
        g0 = norm_g[layer, 0].reshape(1, -1)
        g1 = norm_g[layer, 1].reshape(1, -1)
        if layer % 2 == 0:
            e = layer // 2
            lam_init = 0.8 - 0.6 * math.exp(-0.3 * layer)
            w = w_in_ab[e]
            q_cols = DIFF_WIDTH + NA_WIDTH
            a_k, a_v, b_k, b_v = (w[:, q_cols + i * DIFF_WIDTH:q_cols + (i + 1) * DIFF_WIDTH] for i in range(4))
            w = jnp.concatenate([w[:, :q_cols], a_k, b_k, a_v, b_v], axis=1).astype(BF16)
            dims = dict(n_rope_q=DIFF_WIDTH, n_q=q_cols, n_rope_k=DIFF_WIDTH, n_k=q_cols)
            qT, k, vT, v3 = _proj_in(x, g0, mod_x, w, tables, n_v_tiled=DIFF_WIDTH, **dims)
            qcT, kc, vcT = _proj_in(ctx, g0, mod_c, w, None, **dims)
            lam = diff_lambda[e].astype(F32)
            subg = diff_sub_g[e].astype(F32).reshape(-1, 1)
            bias = _na_bias_blocks(na_rpb[e])
            a_x = _diff_attention(qT, k, v3, kc, vcT, lam, subg, lam_init)
            b_x = _na_attention(qT, k, vT, kc, vcT, bias, has_latent=True)
            attn_x = [a_x, b_x]
            if ctx_out:
                a_c = _diff_attention(qcT, None, None, kc, vcT, lam, subg, lam_init)
                b_c = _na_attention(qcT, None, None, kc, vcT, None, has_latent=False)
                attn_c = [a_c, b_c]
            w_out = w_out_ab[e].astype(BF16)
        else:
            o = layer // 2
            w = w_in_c[o].astype(BF16)
            dims = dict(n_rope_q=C_Q_COLS, n_q=C_Q_COLS, n_rope_k=C_KV_COLS, n_k=C_KV_COLS)
            qT, k, vT = _proj_in(x, g0, mod_x, w, tables, **dims)
            qcT, kc, vcT = _proj_in(ctx, g0, mod_c, w, None, **dims)
            attn_x = [_swa_attention(qT, k, vT, kc, vcT, attn_sinks[o], has_latent=True)]
            if ctx_out:
                attn_c = [_swa_attention(qcT, None, None, kc, vcT, attn_sinks[o], has_latent=False)]
            w_out = w_out_c[o].astype(BF16)
        w1, w3, w2 = ffn_w1[layer].astype(BF16), ffn_w3[layer].astype(BF16), ffn_w2[layer].astype(BF16)
        fg = final_g.reshape(1, -1) if layer == depth - 1 else None
        x = _out_ffn(x, attn_x, w_out, mod_x, g1, w1, w3, w2, fg)
        if ctx_out:
            ctx = _out_ffn(ctx, attn_c, w_out, mod_c, g1, w1, w3, w2)
    return x


def kernel(x, c, ctx, c_ctx, ada_w, ada_b, norm_g, w_in_ab, w_out_ab, diff_lambda, diff_sub_g, na_rpb, w_in_c,
           w_out_c, attn_sinks, ffn_w1, ffn_w3, ffn_w2, final_g):
    batch, _, d = x.shape
    depth = ada_w.shape[0]
    outs = []
    for b in range(batch):
        cc = jnp.zeros((8, d), F32).at[0].set(c[b].astype(F32)).at[1].set(c_ctx.astype(F32))
        mods = _adaln(cc, ada_w, ada_b)[:, :2].reshape(depth, 2, 6, d)
        outs.append(_trunk(x[b], ctx[b], mods, norm_g, w_in_ab, w_out_ab, diff_lambda, diff_sub_g, na_rpb,
                           w_in_c, w_out_c, attn_sinks, ffn_w1, ffn_w3, ffn_w2, final_g))
    return jnp.stack(outs).astype(x.dtype)
```

```python
import functools
import math

import jax
import jax.numpy as jnp
from jax import lax
from jax.experimental import pallas as pl
from jax.experimental.pallas import tpu as pltpu

F32 = jnp.float32
BF16 = jnp.bfloat16

D_MODEL = 1024
GRID_W = 64
HEAD_DIM = 64
ROPE_BASE = 10000.0
NORM_EPS = 1e-6
NEG_INF = -1e30
DIFF_HEADS = 4
DIFF_WIDTH = DIFF_HEADS * 2 * HEAD_DIM
NA_HEADS = 8
NA_WIDTH = NA_HEADS * HEAD_DIM
NA_MAX_ROWS = 8
NA_COLS = 16
GQA_HEADS = 16
GQA_KV_HEADS = 4
GQA_GROUP = GQA_HEADS // GQA_KV_HEADS
SWA_WINDOW = 128
C_Q_COLS = GQA_HEADS * HEAD_DIM
C_KV_COLS = GQA_KV_HEADS * HEAD_DIM
LOG2E = math.log2(math.e)
Q_SCALE = HEAD_DIM ** -0.5 * LOG2E

LANES = 128
ONES_ROWS = 16
VMEM_LIMIT = 56 * 1024 * 1024

ROW_TILE = 512
FFN_CHUNK = 256
DIFF_TQ = 1024
DIFF_TK = 512
DIFF_QC = 256
DIFF_SWEEP = 4
DIFF_RC = 64
DIFF_DEPTH = 4
assert DIFF_TK == ROW_TILE
NA_TILE_ROWS = 4
NA_TQ = NA_TILE_ROWS * GRID_W
NA_STEP_PAIRS = 2
NA_DEPTH = 2
SWA_TQ = SWA_WINDOW
SWA_ITEM_HEADS = 2
SWA_DEPTH = 3


def _cparams(*sem):
    return pltpu.CompilerParams(dimension_semantics=sem, vmem_limit_bytes=VMEM_LIMIT)


def _resident(shape):
    nd = len(shape)
    return pl.BlockSpec(shape, lambda *_: (0,) * nd, pipeline_mode=pl.Buffered(1))


def _dot(a, b):
    return jnp.dot(a, b, preferred_element_type=F32)


def _adaln_kernel(cc_ref, w_ref, b_ref, o_ref):
    cc = cc_ref[...]
    h = (cc * jax.nn.sigmoid(cc)).astype(BF16)
    o_ref[0] = _dot(h, w_ref[0].astype(BF16)) + b_ref[0]


def _adaln(cc, ada_w, ada_b):
    depth, d, d6 = ada_w.shape
    nj = d6 // d
    return pl.pallas_call(
        _adaln_kernel,
        grid=(depth, nj),
        in_specs=[
            pl.BlockSpec((8, d), lambda l, j: (0, 0)),
            pl.BlockSpec((1, d, d), lambda l, j: (l, 0, j)),
            pl.BlockSpec((1, 1, d), lambda l, j: (l, 0, j)),
        ],
        out_specs=pl.BlockSpec((1, 8, d), lambda l, j: (l, 0, j)),
        out_shape=jax.ShapeDtypeStruct((depth, 8, d6), F32),
        compiler_params=_cparams("arbitrary", "arbitrary"),
        name="adaln",
    )(cc, ada_w, ada_b.reshape(depth, 1, d6))


def _norm_mod(x, g, shift, scale):
    ms = jnp.mean(x * x, axis=-1, keepdims=True)
    h = x * lax.rsqrt(ms + NORM_EPS) * g
    return h * (1.0 + scale) + shift


def _rope_lanes(t, cos, sin_signed):
    lane = lax.broadcasted_iota(jnp.int32, (t.shape[0], LANES), 1)
    first = (lane % (HEAD_DIM // 2)) < (HEAD_DIM // 4)
    out = []
    for c in range(t.shape[1] // LANES):
        tc = t[:, c * LANES:(c + 1) * LANES]
        partner = jnp.where(first, pltpu.roll(tc, LANES - HEAD_DIM // 4, 1), pltpu.roll(tc, HEAD_DIM // 4, 1))
        out.append(tc * cos + partner * sin_signed)
    return jnp.concatenate(out, axis=1)


def _proj_in_kernel(*refs, n_rope_q, n_q, n_rope_k, n_k, rope, n_v_tiled):
    n_in = 6 if rope else 4
    if rope:
        x_ref, g_ref, mod_ref, cos_ref, sin_ref, w_ref = refs[:n_in]
    else:
        x_ref, g_ref, mod_ref, w_ref = refs[:n_in]
    qT_ref, k_ref, vT_ref = refs[n_in:n_in + 3]
    h = _norm_mod(x_ref[...], g_ref[...], mod_ref[0:1, :], mod_ref[1:2, :])
    y = _dot(h.astype(BF16), w_ref[...])
    q = y[:, :n_q]
    k = y[:, n_q:n_q + n_k]
    v = y[:, n_q + n_k:]
    if rope:
        cos, sin = cos_ref[...], sin_ref[...]
        q = jnp.concatenate([_rope_lanes(q[:, :n_rope_q], cos, sin), q[:, n_rope_q:]], axis=1) \
            if n_rope_q < n_q else _rope_lanes(q, cos, sin)
        k = jnp.concatenate([_rope_lanes(k[:, :n_rope_k], cos, sin), k[:, n_rope_k:]], axis=1) \
            if n_rope_k < n_k else _rope_lanes(k, cos, sin)
    qT_ref[...] = (q * Q_SCALE).T.astype(BF16)
    k_ref[...] = k.astype(BF16)
    vT_ref[...] = v[:, n_v_tiled:].T.astype(BF16)
    if n_v_tiled:
        refs[n_in + 3][0] = v[:, :n_v_tiled].T.astype(BF16)


def _proj_in(x, g, mod, w, tables, *, n_rope_q, n_q, n_rope_k, n_k, n_v_tiled=0):
    n, d = x.shape
    cols = w.shape[1]
    n_v = cols - n_q - n_k - n_v_tiled
    tm = min(ROW_TILE, n)
    rope = tables is not None
    row = lambda i: (i, 0)
    in_specs = [pl.BlockSpec((tm, d), row), _resident((1, d)), _resident(mod.shape)]
    args = [x, g, mod]
    if rope:
        in_specs += [pl.BlockSpec((tm, LANES), row), pl.BlockSpec((tm, LANES), row)]
        args += list(tables)
    in_specs.append(_resident(w.shape))
    args.append(w)
    out_specs = [
        pl.BlockSpec((n_q, tm), lambda i: (0, i)),
        pl.BlockSpec((tm, n_k), row),
        pl.BlockSpec((n_v, tm), lambda i: (0, i)),
    ]
    out_shape = [
        jax.ShapeDtypeStruct((n_q, n), BF16),
        jax.ShapeDtypeStruct((n, n_k), BF16),
        jax.ShapeDtypeStruct((n_v, n), BF16),
    ]
    if n_v_tiled:
        out_specs.append(pl.BlockSpec((1, n_v_tiled, tm), lambda i: (i, 0, 0)))
        out_shape.append(jax.ShapeDtypeStruct((n // tm, n_v_tiled, tm), BF16))
    return pl.pallas_call(
        functools.partial(_proj_in_kernel, n_rope_q=n_rope_q, n_q=n_q, n_rope_k=n_rope_k, n_k=n_k, rope=rope,
                          n_v_tiled=n_v_tiled),
        grid=(n // tm,),
        in_specs=in_specs,
        out_specs=out_specs,
        out_shape=out_shape,
        compiler_params=_cparams("arbitrary"),
        name="proj_in",
    )(*args)


def _half_masked(qT_pair, e):
    z = jnp.zeros((HEAD_DIM, qT_pair.shape[1]), qT_pair.dtype)
    if e == 0:
        return jnp.concatenate([qT_pair[:HEAD_DIM], z], axis=0)
    return jnp.concatenate([z, qT_pair[HEAD_DIM:]], axis=0)


def _diff_kernel(*refs, lam_init, n_tiles):
    if n_tiles:
        qT_ref, k_ref, v3_ref, kc_ref, vcT_ref, lam_ref, subg_ref, o_ref = refs[:8]
    else:
        qT_ref, kc_ref, vcT_ref, lam_ref, subg_ref, o_ref = refs[:6]
    qm_ref, m_ref, alpha_ref, l_ref, acc_ref, s_ref, p_ref = refs[-7:]

    tq = qm_ref.shape[2]
    items = [(h, e, c) for h in range(DIFF_HEADS) for e in range(2) for c in range(0, tq, DIFF_QC)]
    n = len(items)
    s_slots = s_ref.shape[0]
    assert n % s_slots == 0 and s_slots > DIFF_DEPTH, (n, s_slots)
    sub = 8

    def keys_of(tile, h):
        lanes = slice(h * LANES, (h + 1) * LANES)
        if tile is None:
            return kc_ref[:, lanes]
        return k_ref[pl.ds(pl.multiple_of(tile * DIFF_TK, DIFF_TK), DIFF_TK), lanes]

    def vals_of(tile, h):
        rows = slice(h * LANES, (h + 1) * LANES)
        return vcT_ref[rows, :] if tile is None else v3_ref[tile, rows, :]

    def scores(i, tile):
        h, e, c = items[i]
        mp = 2 * h + e
        keys = keys_of(tile, h)
        nk = keys.shape[0]
        s = _dot(keys, qm_ref[mp, :, c:c + DIFF_QC])
        s_ref[i % s_slots, :nk, :] = s
        m_part = jnp.max(s.reshape(nk // sub, sub, DIFF_QC), axis=0)
        m_old = m_ref[mp, :, c:c + DIFF_QC]
        m_new = jnp.maximum(m_old, jnp.max(m_part, axis=0, keepdims=True))
        m_ref[mp, :, c:c + DIFF_QC] = m_new
        alpha_ref[mp, :, c:c + DIFF_QC] = jnp.exp2(m_old - m_new)

    def absorb(i, tile):
        h, e, c = items[i]
        mp = 2 * h + e
        cols = slice(c, c + DIFF_QC)
        vals = vals_of(tile, h)
        nk = vals.shape[1]
        m_new = m_ref[mp, :, cols]
        alpha = alpha_ref[mp, :, cols]
        for r in range(0, nk, DIFF_RC):
            p = jnp.exp2(s_ref[i % s_slots, r:r + DIFF_RC, :] - m_new)
            p_ref[i % 2, r:r + DIFF_RC, :] = p.astype(BF16)
        pv = _dot(jnp.concatenate([vals, jnp.ones((ONES_ROWS, nk), BF16)], axis=0), p_ref[i % 2, :nk, :])
        l_ref[mp, :, cols] = alpha * l_ref[mp, :, cols] + pv[LANES:LANES + 1]
        acc_ref[mp, :, cols] = alpha * acc_ref[mp, :, cols] + pv[:LANES]

    def sweep(tiles):
        work = [(i, tile) for tile in tiles for i in range(n)]
        for i, tile in work[:DIFF_DEPTH]:
            scores(i, tile)
        for j, (i, tile) in enumerate(work):
            if j + DIFF_DEPTH < len(work):
                scores(*work[j + DIFF_DEPTH])
            absorb(i, tile)

    for h in range(DIFF_HEADS):
        pair = qT_ref[h * LANES:(h + 1) * LANES, :]
        for e in range(2):
            qm_ref[2 * h + e] = _half_masked(pair, e)
    m_ref[...] = jnp.full(m_ref.shape, NEG_INF, F32)
    l_ref[...] = jnp.zeros(l_ref.shape, F32)
    acc_ref[...] = jnp.zeros(acc_ref.shape, F32)
    sweep([None])
    if n_tiles:
        def body(t, carry):
            sweep([DIFF_SWEEP * t + u for u in range(DIFF_SWEEP)])
            return carry
        lax.fori_loop(0, n_tiles // DIFF_SWEEP, body, 0)

    lam = lam_ref[...]
    lam_val = (jnp.exp(jnp.sum(lam[0:1] * lam[1:2], axis=1, keepdims=True))
               - jnp.exp(jnp.sum(lam[2:3] * lam[3:4], axis=1, keepdims=True)) + lam_init)
    subg = subg_ref[...]
    for h in range(DIFF_HEADS):
        a0 = acc_ref[2 * h] / l_ref[2 * h]
        a1 = acc_ref[2 * h + 1] / l_ref[2 * h + 1]
        a = a0 - lam_val * a1
        ms = jnp.mean(a * a, axis=0, keepdims=True)
        a = a * lax.rsqrt(ms + NORM_EPS) * subg * (1.0 - lam_init)
        o_ref[:, h * LANES:(h + 1) * LANES] = a.T.astype(o_ref.dtype)


def _diff_attention(qT, k, v3, kc, vcT, lam, subg, lam_init):
    n = qT.shape[1]
    nc = kc.shape[0]
    w = DIFF_WIDTH
    small = [_resident((4, HEAD_DIM)), _resident((2 * HEAD_DIM, 1))]
    if k is not None:
        tq, tk = DIFF_TQ, DIFF_TK
        n_tiles = v3.shape[0]
        assert v3.shape == (k.shape[0] // tk, w, tk) and n_tiles % DIFF_SWEEP == 0, (v3.shape, k.shape)
        in_specs = [
            pl.BlockSpec((w, tq), lambda i: (0, i)),
            _resident((k.shape[0], w)),
            _resident(v3.shape),
            pl.BlockSpec((nc, w), lambda i: (0, 0)),
            pl.BlockSpec((w, nc), lambda i: (0, 0)),
        ] + small
        args = (qT, k, v3, kc, vcT, lam, subg)
    else:
        tq, tk, n_tiles = n, 0, 0
        in_specs = [
            pl.BlockSpec((w, tq), lambda i: (0, 0)),
            pl.BlockSpec((nc, w), lambda i: (0, 0)),
            pl.BlockSpec((w, nc), lambda i: (0, 0)),
        ] + small
        args = (qT, kc, vcT, lam, subg)
    return pl.pallas_call(
        functools.partial(_diff_kernel, lam_init=lam_init, n_tiles=n_tiles),
        grid=(n // tq,),
        in_specs=in_specs,
        out_specs=pl.BlockSpec((tq, w), lambda i: (i, 0)),
        out_shape=jax.ShapeDtypeStruct((n, w), BF16),
        scratch_shapes=[
            pltpu.VMEM((2 * DIFF_HEADS, LANES, tq), BF16),
            pltpu.VMEM((2 * DIFF_HEADS, 1, tq), F32),
            pltpu.VMEM((2 * DIFF_HEADS, 1, tq), F32),
            pltpu.VMEM((2 * DIFF_HEADS, 1, tq), F32),
            pltpu.VMEM((2 * DIFF_HEADS, LANES, tq), F32),
            pltpu.VMEM((2 * DIFF_DEPTH, max(tk, nc), DIFF_QC), F32),
            pltpu.VMEM((2, max(tk, nc), DIFF_QC), BF16),
        ],
        compiler_params=_cparams("arbitrary"),
        name="diff_attn" if n_tiles else "diff_attn_ctx",
    )(*args)


def _na_bias_kernel(rpb_ref, o_ref):
    h = pl.program_id(0)
    n_dr, n_dc = 2 * NA_MAX_ROWS - 1, 2 * NA_COLS - 1
    kc = lax.broadcasted_iota(jnp.int32, (GRID_W, LANES), 0)
    lane = lax.broadcasted_iota(jnp.int32, (GRID_W, LANES), 1)
    c = lane % GRID_W
    upper = lane >= GRID_W
    dc = kc - c + (NA_COLS - 1)
    cs = jnp.clip(c - NA_COLS // 2, 0, GRID_W - NA_COLS)
    col_ok = (kc >= cs) & (kc < cs + NA_COLS)
    for i in range(o_ref.shape[1]):
        dr_idx = (i - 1, i - 2)
        val = jnp.full((GRID_W, LANES), NEG_INF, F32)
        for d in range(n_dc):
            lo, up = (rpb_ref[(h * n_dr + r) * n_dc + d] * LOG2E if 0 <= r < n_dr else NEG_INF for r in dr_idx)
            val = jnp.where(dc == d, jnp.where(upper, up, lo), val)
        o_ref[0, i] = jnp.where(col_ok, val, NEG_INF)


def _na_bias_blocks(rpb):
    n_heads = rpb.shape[0]
    n_blocks = 2 * NA_MAX_ROWS
    return pl.pallas_call(
        _na_bias_kernel,
        grid=(n_heads,),
        in_specs=[pl.BlockSpec(memory_space=pltpu.SMEM)],
        out_specs=pl.BlockSpec((1, n_blocks, GRID_W, LANES), lambda h: (h, 0, 0, 0)),
        out_shape=jax.ShapeDtypeStruct((n_heads, n_blocks, GRID_W, LANES), F32),
        compiler_params=_cparams("arbitrary"),
        name="na_bias",
    )(rpb.astype(F32).reshape(-1))


def _na_window_bias(bias_ref, e, t, grid_rows):
    lane = lax.broadcasted_iota(jnp.int32, (GRID_W, LANES), 1)
    r0 = t * NA_TILE_ROWS
    rows = []
    for rk in range(3 * NA_TILE_ROWS):
        kr = r0 - NA_TILE_ROWS + rk
        cols = []
        for p in range(NA_TILE_ROWS // 2):
            blk = bias_ref[e, rk - NA_TILE_ROWS - 2 * p + NA_MAX_ROWS]
            r = r0 + 2 * p + lane // GRID_W
            rs = jnp.clip(r - NA_MAX_ROWS // 2, 0, grid_rows - NA_MAX_ROWS)
            ok = (kr >= rs) & (kr < rs + NA_MAX_ROWS)
            cols.append(jnp.where(ok, blk, NEG_INF))
        rows.append(jnp.concatenate(cols, axis=1))
    return jnp.concatenate(rows, axis=0)


def _na_kernel(*refs, has_latent, grid_rows):
    if has_latent:
        qT_ref, km_ref, k0_ref, kp_ref, vm_ref, v0_ref, vp_ref, kc_ref, vcT_ref, bias_ref, o_ref = refs
    else:
        qT_ref, kc_ref, vcT_ref, o_ref = refs
    items = [(pp, e) for pp in range(NA_STEP_PAIRS) for e in range(2)]

    def scores(item):
        pp, e = item
        lanes = slice(pp * LANES, (pp + 1) * LANES)
        qm = _half_masked(qT_ref[lanes, :], e)
        s = _dot(kc_ref[:, lanes], qm)
        if has_latent:
            s_nb = jnp.concatenate([_dot(r[:, lanes], qm) for r in (km_ref, k0_ref, kp_ref)], axis=0)
            s = jnp.concatenate([s_nb + _na_window_bias(bias_ref, 2 * pp + e, pl.program_id(1), grid_rows), s], axis=0)
        return s

    def attend(item, s):
        pp, e = item
        rows = slice(pp * LANES + e * HEAD_DIM, pp * LANES + (e + 1) * HEAD_DIM)
        vT = vcT_ref[rows, :]
        if has_latent:
            vT = jnp.concatenate([vm_ref[rows, :], v0_ref[rows, :], vp_ref[rows, :], vT], axis=1)
        p = jnp.exp2(s - jnp.max(s, axis=0, keepdims=True))
        pv = _dot(jnp.concatenate([vT, jnp.ones((ONES_ROWS, vT.shape[1]), BF16)], axis=0), p.astype(BF16))
        return pv[:HEAD_DIM] / pv[HEAD_DIM:HEAD_DIM + 1]

    pending = [scores(item) for item in items[:NA_DEPTH]]
    outs = []
    for i, item in enumerate(items):
        if i + NA_DEPTH < len(items):
            pending.append(scores(items[i + NA_DEPTH]))
        outs.append(attend(item, pending.pop(0)))
    o_ref[...] = jnp.concatenate(outs, axis=0).T.astype(o_ref.dtype)


def _na_attention(qT_all, k_all, vT_all, kc_all, vcT_all, bias, *, has_latent):
    n = qT_all.shape[1]
    nc = kc_all.shape[0]
    w = NA_STEP_PAIRS * LANES
    g0 = DIFF_WIDTH // w
    n_steps = NA_WIDTH // w
    if has_latent:
        tq = NA_TQ
        nt = n // tq
        prev = lambda t: jnp.maximum(t - 1, 0)
        nxt = lambda t: jnp.minimum(t + 1, nt - 1)
        in_specs = [
            pl.BlockSpec((w, tq), lambda hp, t: (g0 + hp, t)),
            pl.BlockSpec((tq, w), lambda hp, t: (prev(t), g0 + hp)),
            pl.BlockSpec((tq, w), lambda hp, t: (t, g0 + hp)),
            pl.BlockSpec((tq, w), lambda hp, t: (nxt(t), g0 + hp)),
            pl.BlockSpec((w, tq), lambda hp, t: (hp, prev(t))),
            pl.BlockSpec((w, tq), lambda hp, t: (hp, t)),
            pl.BlockSpec((w, tq), lambda hp, t: (hp, nxt(t))),
            pl.BlockSpec((nc, w), lambda hp, t: (0, g0 + hp)),
            pl.BlockSpec((w, nc), lambda hp, t: (g0 + hp, 0)),
            pl.BlockSpec((2 * NA_STEP_PAIRS,) + bias.shape[1:], lambda hp, t: (hp, 0, 0, 0)),
        ]
        args = (qT_all, k_all, k_all, k_all, vT_all, vT_all, vT_all, kc_all, vcT_all, bias)
    else:
        tq = n
        nt = 1
        in_specs = [
            pl.BlockSpec((w, tq), lambda hp, t: (g0 + hp, 0)),
            pl.BlockSpec((nc, w), lambda hp, t: (0, g0 + hp)),
            pl.BlockSpec((w, nc), lambda hp, t: (g0 + hp, 0)),
        ]
        args = (qT_all, kc_all, vcT_all)
    return pl.pallas_call(
        functools.partial(_na_kernel, has_latent=has_latent, grid_rows=n // GRID_W),
        grid=(n_steps, nt),
        in_specs=in_specs,
        out_specs=pl.BlockSpec((tq, w), lambda hp, t: (t, hp)),
        out_shape=jax.ShapeDtypeStruct((n, NA_WIDTH), BF16),
        compiler_params=_cparams("arbitrary", "arbitrary"),
        name="na_attn" if has_latent else "na_attn_ctx",
    )(*args)


def _swa_kernel(*refs, has_latent):
    if has_latent:
        qT_ref, km_ref, k0_ref, kp_ref, vm_ref, v0_ref, vp_ref, kc_ref, vcT_ref, sink_ref, o_ref = refs
        b = pl.program_id(0)
        nb = pl.num_programs(0)
        tq = qT_ref.shape[1]
        ki = lax.broadcasted_iota(jnp.int32, (tq, tq), 0)
        qi = lax.broadcasted_iota(jnp.int32, (tq, tq), 1)
        band_prev = jnp.concatenate([(ki >= qi) & (b > 0)] * SWA_ITEM_HEADS, axis=1)
        band_next = jnp.concatenate([(ki <= qi) & (b < nb - 1)] * SWA_ITEM_HEADS, axis=1)
    else:
        qT_ref, kc_ref, vcT_ref, sink_ref, o_ref = refs
        tq = qT_ref.shape[1]
    zeros = jnp.zeros((HEAD_DIM, tq), BF16)
    width = SWA_ITEM_HEADS * tq
    items = [(g, j0) for g in range(GQA_KV_HEADS) for j0 in range(0, GQA_GROUP, SWA_ITEM_HEADS)]

    def scores(item):
        g, j0 = item
        lanes = slice((g // 2) * LANES, (g // 2 + 1) * LANES)
        q_cols = []
        for jh in range(j0, j0 + SWA_ITEM_HEADS):
            hq = g * GQA_GROUP + jh
            qh = qT_ref[hq * HEAD_DIM:(hq + 1) * HEAD_DIM, :]
            q_cols.append(jnp.concatenate([qh, zeros] if g % 2 == 0 else [zeros, qh], axis=0))
        qm = jnp.concatenate(q_cols, axis=1)
        s = _dot(kc_ref[:, lanes], qm)
        if has_latent:
            s_prev = jnp.where(band_prev, _dot(km_ref[:, lanes], qm), NEG_INF)
            s_next = jnp.where(band_next, _dot(kp_ref[:, lanes], qm), NEG_INF)
            s = jnp.concatenate([s, s_prev, _dot(k0_ref[:, lanes], qm), s_next], axis=0)
        return s

    def attend(item, s):
        g, j0 = item
        rows = slice(g * HEAD_DIM, (g + 1) * HEAD_DIM)
        vT = vcT_ref[rows, :]
        if has_latent:
            vT = jnp.concatenate([vT, vm_ref[rows, :], v0_ref[rows, :], vp_ref[rows, :]], axis=1)
        sink = sink_ref[g][:, j0 * tq:j0 * tq + width] * LOG2E
        m = jnp.maximum(jnp.max(s, axis=0, keepdims=True), sink)
        p = jnp.exp2(s - m)
        pv = _dot(jnp.concatenate([vT, jnp.ones((ONES_ROWS, vT.shape[1]), BF16)], axis=0), p.astype(BF16))
        o = pv[:HEAD_DIM] / (pv[HEAD_DIM:HEAD_DIM + 1] + jnp.exp2(sink - m))
        return [o[:, jh * tq:(jh + 1) * tq] for jh in range(SWA_ITEM_HEADS)]

    pending = [scores(item) for item in items[:SWA_DEPTH]]
    outs = []
    for i, item in enumerate(items):
        if i + SWA_DEPTH < len(items):
            pending.append(scores(items[i + SWA_DEPTH]))
        outs += attend(item, pending.pop(0))
    o_ref[...] = jnp.concatenate(outs, axis=0).T.astype(o_ref.dtype)


def _swa_attention(qT, k, vT, kc, vcT, sinks, *, has_latent):
    n = qT.shape[1]
    nc = kc.shape[0]
    tq = SWA_TQ
    nb = n // tq
    sink_rows = jnp.repeat(sinks.astype(F32).reshape(GQA_KV_HEADS, 1, GQA_GROUP), tq, axis=2)
    ctx_specs = [_resident((nc, C_KV_COLS)), _resident((C_KV_COLS, nc)), _resident(sink_rows.shape)]
    if has_latent:
        prev = lambda b: jnp.maximum(b - 1, 0)
        nxt = lambda b: jnp.minimum(b + 1, nb - 1)
        in_specs = [
            pl.BlockSpec((C_Q_COLS, tq), lambda b: (0, b)),
            pl.BlockSpec((tq, C_KV_COLS), lambda b: (prev(b), 0)),
            pl.BlockSpec((tq, C_KV_COLS), lambda b: (b, 0)),
            pl.BlockSpec((tq, C_KV_COLS), lambda b: (nxt(b), 0)),
            pl.BlockSpec((C_KV_COLS, tq), lambda b: (0, prev(b))),
            pl.BlockSpec((C_KV_COLS, tq), lambda b: (0, b)),
            pl.BlockSpec((C_KV_COLS, tq), lambda b: (0, nxt(b))),
        ] + ctx_specs
        args = (qT, k, k, k, vT, vT, vT, kc, vcT, sink_rows)
    else:
        in_specs = [pl.BlockSpec((C_Q_COLS, tq), lambda b: (0, b))] + ctx_specs
        args = (qT, kc, vcT, sink_rows)
    return pl.pallas_call(
        functools.partial(_swa_kernel, has_latent=has_latent),
        grid=(nb,),
        in_specs=in_specs,
        out_specs=pl.BlockSpec((tq, C_Q_COLS), lambda b: (b, 0)),
        out_shape=jax.ShapeDtypeStruct((n, C_Q_COLS), BF16),
        compiler_params=_cparams("arbitrary"),
        name="swa_attn" if has_latent else "swa_attn_ctx",
    )(*args)


def _out_ffn_kernel(*refs, n_attn, final):
    x_ref = refs[0]
    attn_refs = refs[1:1 + n_attn]
    wout_ref, mod_ref, g_ref, w1_ref, w3_ref, w2_ref = refs[1 + n_attn:7 + n_attn]
    rest = refs[7 + n_attn:]
    if final:
        fg_ref, o_ref = rest
    else:
        (o_ref,) = rest
    proj = None
    r0 = 0
    for a_ref in attn_refs:
        wdt = a_ref.shape[1]
        part = _dot(a_ref[...], wout_ref[r0:r0 + wdt, :])
        proj = part if proj is None else proj + part
        r0 += wdt
    x1 = x_ref[...] + mod_ref[2:3, :] * proj
    h = _norm_mod(x1, g_ref[...], mod_ref[3:4, :], mod_ref[4:5, :]).astype(BF16)
    hidden = w1_ref.shape[1]
    chunk = FFN_CHUNK
    ffn = None
    for c0 in range(0, hidden, chunk):
        a = _dot(h, w1_ref[:, c0:c0 + chunk])
        b = _dot(h, w3_ref[:, c0:c0 + chunk])
        gated = (a * jax.nn.sigmoid(a) * b).astype(BF16)
        part = _dot(gated, w2_ref[c0:c0 + chunk, :])
        ffn = part if ffn is None else ffn + part
    x2 = x1 + mod_ref[5:6, :] * ffn
    if final:
        ms = jnp.mean(x2 * x2, axis=-1, keepdims=True)
        x2 = x2 * lax.rsqrt(ms + NORM_EPS) * fg_ref[...]
    o_ref[...] = x2


def _out_ffn(x, attn, w_out, mod, g, w1, w3, w2, final_g=None):
    n, d = x.shape
    tm = min(ROW_TILE, n)
    row = lambda i: (i, 0)
    final = final_g is not None
    in_specs = [pl.BlockSpec((tm, d), row)]
    in_specs += [pl.BlockSpec((tm, a.shape[1]), row) for a in attn]
    in_specs += [_resident(w_out.shape), _resident(mod.shape), _resident((1, d)),
                 _resident(w1.shape), _resident(w3.shape), _resident(w2.shape)]
    args = [x, *attn, w_out, mod, g, w1, w3, w2]
    if final:
        in_specs.append(_resident((1, d)))
        args.append(final_g)
    return pl.pallas_call(
        functools.partial(_out_ffn_kernel, n_attn=len(attn), final=final),
        grid=(n // tm,),
        in_specs=in_specs,
        out_specs=pl.BlockSpec((tm, d), row),
        out_shape=jax.ShapeDtypeStruct((n, d), F32),
        compiler_params=_cparams("arbitrary"),
        name="out_ffn",
    )(*args)


def _rope_tables(n):
    t = jnp.arange(n)
    row = (t // GRID_W).astype(F32)
    col = (t % GRID_W).astype(F32)
    quarter = HEAD_DIM // 4
    inv = ROPE_BASE ** (-jnp.arange(quarter, dtype=F32) / quarter)
    ar = row[:, None] * inv
    ac = col[:, None] * inv
    ang = jnp.concatenate([ar, ar, ac, ac], axis=-1)
    sign = jnp.tile(jnp.concatenate([-jnp.ones(quarter, F32), jnp.ones(quarter, F32)]), 2)
    cos = jnp.tile(jnp.cos(ang), (1, LANES // HEAD_DIM))
    sin = jnp.tile(jnp.sin(ang) * sign, (1, LANES // HEAD_DIM))
    return cos, sin


def _trunk(x, ctx, mods, norm_g, w_in_ab, w_out_ab, diff_lambda, diff_sub_g, na_rpb, w_in_c, w_out_c,
           attn_sinks, ffn_w1, ffn_w3, ffn_w2, final_g):
    depth = mods.shape[0]
    s_len = x.shape[0]
    tables = _rope_tables(s_len)
    for layer in range(depth):
        ctx_out = layer < depth - 1
        mod_x, mod_c = mods[layer, 0], mods[layer, 1]
        g0 = norm_g[layer, 0].reshape(1, -1)
        g1 = norm_g[layer, 1].reshape(1, -1)
        if layer % 2 == 0:
            e = layer // 2
            lam_init = 0.8 - 0.6 * math.exp(-0.3 * layer)
            w = w_in_ab[e]
            q_cols = DIFF_WIDTH + NA_WIDTH
            a_k, a_v, b_k, b_v = (w[:, q_cols + i * DIFF_WIDTH:q_cols + (i + 1) * DIFF_WIDTH] for i in range(4))
            w = jnp.concatenate([w[:, :q_cols], a_k, b_k, a_v, b_v], axis=1).astype(BF16)
            dims = dict(n_rope_q=DIFF_WIDTH, n_q=q_cols, n_rope_k=DIFF_WIDTH, n_k=q_cols)
            qT, k, vT, v3 = _proj_in(x, g0, mod_x, w, tables, n_v_tiled=DIFF_WIDTH, **dims)
            qcT, kc, vcT = _proj_in(ctx, g0, mod_c, w, None, **dims)
            lam = diff_lambda[e].astype(F32)
            subg = diff_sub_g[e].astype(F32).reshape(-1, 1)
            bias = _na_bias_blocks(na_rpb[e])
            a_x = _diff_attention(qT, k, v3, kc, vcT, lam, subg, lam_init)
            b_x = _na_attention(qT, k, vT, kc, vcT, bias, has_latent=True)
            attn_x = [a_x, b_x]
            if ctx_out:
                a_c = _diff_attention(qcT, None, None, kc, vcT, lam, subg, lam_init)
                b_c = _na_attention(qcT, None, None, kc, vcT, None, has_latent=False)
                attn_c = [a_c, b_c]
            w_out = w_out_ab[e].astype(BF16)
        else:
            o = layer // 2
            w = w_in_c[o].astype(BF16)
            dims = dict(n_rope_q=C_Q_COLS, n_q=C_Q_COLS, n_rope_k=C_KV_COLS, n_k=C_KV_COLS)
            qT, k, vT = _proj_in(x, g0, mod_x, w, tables, **dims)
            qcT, kc, vcT = _proj_in(ctx, g0, mod_c, w, None, **dims)
            attn_x = [_swa_attention(qT, k, vT, kc, vcT, attn_sinks[o], has_latent=True)]
            if ctx_out:
                attn_c = [_swa_attention(qcT, None, None, kc, vcT, attn_sinks[o], has_latent=False)]
            w_out = w_out_c[o].astype(BF16)
        w1, w3, w2 = ffn_w1[layer].astype(BF16), ffn_w3[layer].astype(BF16), ffn_w2[layer].astype(BF16)
        fg = final_g.reshape(1, -1) if layer == depth - 1 else None
        x = _out_ffn(x, attn_x, w_out, mod_x, g1, w1, w3, w2, fg)
        if ctx_out:
            ctx = _out_ffn(ctx, attn_c, w_out, mod_c, g1, w1, w3, w2)
    return x


def kernel(x, c, ctx, c_ctx, ada_w, ada_b, norm_g, w_in_ab, w_out_ab, diff_lambda, diff_sub_g, na_rpb, w_in_c,
           w_out_c, attn_sinks, ffn_w1, ffn_w3, ffn_w2, final_g):
    batch, _, d = x.shape
    depth = ada_w.shape[0]
    outs = []
    for b in range(batch):
        cc = jnp.zeros((8, d), F32).at[0].set(c[b].astype(F32)).at[1].set(c_ctx.astype(F32))
        mods = _adaln(cc, ada_w, ada_b)[:, :2].reshape(depth, 2, 6, d)
        outs.append(_trunk(x[b], ctx[b], mods, norm_g, w_in_ab, w_out_ab, diff_lambda, diff_sub_g, na_rpb,
                           w_in_c, w_out_c, attn_sinks, ffn_w1, ffn_w3, ffn_w2, final_g))
    return jnp.stack(outs).astype(x.dtype)
```

```python
import functools
import math

import jax
import jax.numpy as jnp
from jax import lax
from jax.experimental import pallas as pl
from jax.experimental.pallas import tpu as pltpu

F32 = jnp.float32
BF16 = jnp.bfloat16

D_MODEL = 1024
GRID_W = 64
HEAD_DIM = 64
ROPE_BASE = 10000.0
NORM_EPS = 1e-6
NEG_INF = -1e30
DIFF_HEADS = 4
DIFF_WIDTH = DIFF_HEADS * 2 * HEAD_DIM
NA_HEADS = 8
NA_WIDTH = NA_HEADS * HEAD_DIM
NA_MAX_ROWS = 8
NA_COLS = 16
GQA_HEADS = 16
GQA_KV_HEADS = 4
GQA_GROUP = GQA_HEADS // GQA_KV_HEADS
SWA_WINDOW = 128
C_Q_COLS = GQA_HEADS * HEAD_DIM
C_KV_COLS = GQA_KV_HEADS * HEAD_DIM
LOG2E = math.log2(math.e)
Q_SCALE = HEAD_DIM ** -0.5 * LOG2E

LANES = 128
ONES_ROWS = 16
VMEM_LIMIT = 56 * 1024 * 1024

ROW_TILE = 512
FFN_CHUNK = 256
DIFF_TQ = 1024
DIFF_TK = 512
DIFF_QC = 256
DIFF_SWEEP = 4
DIFF_RC = 64
DIFF_DEPTH = 4
assert DIFF_TK == ROW_TILE
NA_TILE_ROWS = 4
NA_TQ = NA_TILE_ROWS * GRID_W
NA_STEP_PAIRS = 4
NA_DEPTH = 2
SWA_TQ = SWA_WINDOW
SWA_ITEM_HEADS = 2
SWA_DEPTH = 3


def _cparams(*sem):
    return pltpu.CompilerParams(dimension_semantics=sem, vmem_limit_bytes=VMEM_LIMIT)


def _resident(shape):
    nd = len(shape)
    return pl.BlockSpec(shape, lambda *_: (0,) * nd, pipeline_mode=pl.Buffered(1))


def _dot(a, b):
    return jnp.dot(a, b, preferred_element_type=F32)


def _adaln_kernel(cc_ref, w_ref, b_ref, o_ref):
    cc = cc_ref[...]
    h = (cc * jax.nn.sigmoid(cc)).astype(BF16)
    o_ref[0] = _dot(h, w_ref[0].astype(BF16)) + b_ref[0]


def _adaln(cc, ada_w, ada_b):
    depth, d, d6 = ada_w.shape
    nj = d6 // d
    return pl.pallas_call(
        _adaln_kernel,
        grid=(depth, nj),
        in_specs=[
            pl.BlockSpec((8, d), lambda l, j: (0, 0)),
            pl.BlockSpec((1, d, d), lambda l, j: (l, 0, j)),
            pl.BlockSpec((1, 1, d), lambda l, j: (l, 0, j)),
        ],
        out_specs=pl.BlockSpec((1, 8, d), lambda l, j: (l, 0, j)),
        out_shape=jax.ShapeDtypeStruct((depth, 8, d6), F32),
        compiler_params=_cparams("arbitrary", "arbitrary"),
        name="adaln",
    )(cc, ada_w, ada_b.reshape(depth, 1, d6))


def _norm_mod(x, g, shift, scale):
    ms = jnp.mean(x * x, axis=-1, keepdims=True)
    h = x * lax.rsqrt(ms + NORM_EPS) * g
    return h * (1.0 + scale) + shift


def _rope_lanes(t, cos, sin_signed):
    lane = lax.broadcasted_iota(jnp.int32, (t.shape[0], LANES), 1)
    first = (lane % (HEAD_DIM // 2)) < (HEAD_DIM // 4)
    out = []
    for c in range(t.shape[1] // LANES):
        tc = t[:, c * LANES:(c + 1) * LANES]
        partner = jnp.where(first, pltpu.roll(tc, LANES - HEAD_DIM // 4, 1), pltpu.roll(tc, HEAD_DIM // 4, 1))
        out.append(tc * cos + partner * sin_signed)
    return jnp.concatenate(out, axis=1)


def _proj_in_kernel(*refs, n_rope_q, n_q, n_rope_k, n_k, rope, n_v_tiled):
    n_in = 6 if rope else 4
    if rope:
        x_ref, g_ref, mod_ref, cos_ref, sin_ref, w_ref = refs[:n_in]
    else:
        x_ref, g_ref, mod_ref, w_ref = refs[:n_in]
    qT_ref, k_ref, vT_ref = refs[n_in:n_in + 3]
    h = _norm_mod(x_ref[...], g_ref[...], mod_ref[0:1, :], mod_ref[1:2, :])
    y = _dot(h.astype(BF16), w_ref[...])
    q = y[:, :n_q]
    k = y[:, n_q:n_q + n_k]
    v = y[:, n_q + n_k:]
    if rope:
        cos, sin = cos_ref[...], sin_ref[...]
        q = jnp.concatenate([_rope_lanes(q[:, :n_rope_q], cos, sin), q[:, n_rope_q:]], axis=1) \
            if n_rope_q < n_q else _rope_lanes(q, cos, sin)
        k = jnp.concatenate([_rope_lanes(k[:, :n_rope_k], cos, sin), k[:, n_rope_k:]], axis=1) \
            if n_rope_k < n_k else _rope_lanes(k, cos, sin)
    qT_ref[...] = (q * Q_SCALE).T.astype(BF16)
    k_ref[...] = k.astype(BF16)
    vT_ref[...] = v[:, n_v_tiled:].T.astype(BF16)
    if n_v_tiled:
        refs[n_in + 3][0] = v[:, :n_v_tiled].T.astype(BF16)


def _proj_in(x, g, mod, w, tables, *, n_rope_q, n_q, n_rope_k, n_k, n_v_tiled=0):
    n, d = x.shape
    cols = w.shape[1]
    n_v = cols - n_q - n_k - n_v_tiled
    tm = min(ROW_TILE, n)
    rope = tables is not None
    row = lambda i: (i, 0)
    in_specs = [pl.BlockSpec((tm, d), row), _resident((1, d)), _resident(mod.shape)]
    args = [x, g, mod]
    if rope:
        in_specs += [pl.BlockSpec((tm, LANES), row), pl.BlockSpec((tm, LANES), row)]
        args += list(tables)
    in_specs.append(_resident(w.shape))
    args.append(w)
    out_specs = [
        pl.BlockSpec((n_q, tm), lambda i: (0, i)),
        pl.BlockSpec((tm, n_k), row),
        pl.BlockSpec((n_v, tm), lambda i: (0, i)),
    ]
    out_shape = [
        jax.ShapeDtypeStruct((n_q, n), BF16),
        jax.ShapeDtypeStruct((n, n_k), BF16),
        jax.ShapeDtypeStruct((n_v, n), BF16),
    ]
    if n_v_tiled:
        out_specs.append(pl.BlockSpec((1, n_v_tiled, tm), lambda i: (i, 0, 0)))
        out_shape.append(jax.ShapeDtypeStruct((n // tm, n_v_tiled, tm), BF16))
    return pl.pallas_call(
        functools.partial(_proj_in_kernel, n_rope_q=n_rope_q, n_q=n_q, n_rope_k=n_rope_k, n_k=n_k, rope=rope,
                          n_v_tiled=n_v_tiled),
        grid=(n // tm,),
        in_specs=in_specs,
        out_specs=out_specs,
        out_shape=out_shape,
        compiler_params=_cparams("arbitrary"),
        name="proj_in",
    )(*args)


def _half_masked(qT_pair, e):
    z = jnp.zeros((HEAD_DIM, qT_pair.shape[1]), qT_pair.dtype)
    if e == 0:
        return jnp.concatenate([qT_pair[:HEAD_DIM], z], axis=0)
    return jnp.concatenate([z, qT_pair[HEAD_DIM:]], axis=0)


def _diff_kernel(*refs, lam_init, n_tiles):
    if n_tiles:
        qT_ref, k_ref, v3_ref, kc_ref, vcT_ref, lam_ref, subg_ref, o_ref = refs[:8]
    else:
        qT_ref, kc_ref, vcT_ref, lam_ref, subg_ref, o_ref = refs[:6]
    qm_ref, m_ref, alpha_ref, l_ref, acc_ref, s_ref, p_ref = refs[-7:]

    tq = qm_ref.shape[2]
    items = [(h, e, c) for h in range(DIFF_HEADS) for e in range(2) for c in range(0, tq, DIFF_QC)]
    n = len(items)
    s_slots = s_ref.shape[0]
    assert n % s_slots == 0 and s_slots > DIFF_DEPTH, (n, s_slots)
    sub = 8

    def keys_of(tile, h):
        lanes = slice(h * LANES, (h + 1) * LANES)
        if tile is None:
            return kc_ref[:, lanes]
        return k_ref[pl.ds(pl.multiple_of(tile * DIFF_TK, DIFF_TK), DIFF_TK), lanes]

    def vals_of(tile, h):
        rows = slice(h * LANES, (h + 1) * LANES)
        return vcT_ref[rows, :] if tile is None else v3_ref[tile, rows, :]

    def scores(i, tile):
        h, e, c = items[i]
        mp = 2 * h + e
        keys = keys_of(tile, h)
        nk = keys.shape[0]
        s = _dot(keys, qm_ref[mp, :, c:c + DIFF_QC])
        s_ref[i % s_slots, :nk, :] = s
        m_part = jnp.max(s.reshape(nk // sub, sub, DIFF_QC), axis=0)
        m_old = m_ref[mp, :, c:c + DIFF_QC]
        m_new = jnp.maximum(m_old, jnp.max(m_part, axis=0, keepdims=True))
        m_ref[mp, :, c:c + DIFF_QC] = m_new
        alpha_ref[mp, :, c:c + DIFF_QC] = jnp.exp2(m_old - m_new)

    def absorb(i, tile):
        h, e, c = items[i]
        mp = 2 * h + e
        cols = slice(c, c + DIFF_QC)
        vals = vals_of(tile, h)
        nk = vals.shape[1]
        m_new = m_ref[mp, :, cols]
        alpha = alpha_ref[mp, :, cols]
        for r in range(0, nk, DIFF_RC):
            p = jnp.exp2(s_ref[i % s_slots, r:r + DIFF_RC, :] - m_new)
            p_ref[i % 2, r:r + DIFF_RC, :] = p.astype(BF16)
        pv = _dot(jnp.concatenate([vals, jnp.ones((ONES_ROWS, nk), BF16)], axis=0), p_ref[i % 2, :nk, :])
        l_ref[mp, :, cols] = alpha * l_ref[mp, :, cols] + pv[LANES:LANES + 1]
        acc_ref[mp, :, cols] = alpha * acc_ref[mp, :, cols] + pv[:LANES]

    def sweep(tiles):
        work = [(i, tile) for tile in tiles for i in range(n)]
        for i, tile in work[:DIFF_DEPTH]:
            scores(i, tile)
        for j, (i, tile) in enumerate(work):
            if j + DIFF_DEPTH < len(work):
                scores(*work[j + DIFF_DEPTH])
            absorb(i, tile)

    for h in range(DIFF_HEADS):
        pair = qT_ref[h * LANES:(h + 1) * LANES, :]
        for e in range(2):
            qm_ref[2 * h + e] = _half_masked(pair, e)
    m_ref[...] = jnp.full(m_ref.shape, NEG_INF, F32)
    l_ref[...] = jnp.zeros(l_ref.shape, F32)
    acc_ref[...] = jnp.zeros(acc_ref.shape, F32)
    sweep([None])
    if n_tiles:
        def body(t, carry):
            sweep([DIFF_SWEEP * t + u for u in range(DIFF_SWEEP)])
            return carry
        lax.fori_loop(0, n_tiles // DIFF_SWEEP, body, 0)

    lam = lam_ref[...]
    lam_val = (jnp.exp(jnp.sum(lam[0:1] * lam[1:2], axis=1, keepdims=True))
               - jnp.exp(jnp.sum(lam[2:3] * lam[3:4], axis=1, keepdims=True)) + lam_init)
    subg = subg_ref[...]
    for h in range(DIFF_HEADS):
        a0 = acc_ref[2 * h] / l_ref[2 * h]
        a1 = acc_ref[2 * h + 1] / l_ref[2 * h + 1]
        a = a0 - lam_val * a1
        ms = jnp.mean(a * a, axis=0, keepdims=True)
        a = a * lax.rsqrt(ms + NORM_EPS) * subg * (1.0 - lam_init)
        o_ref[:, h * LANES:(h + 1) * LANES] = a.T.astype(o_ref.dtype)


def _diff_attention(qT, k, v3, kc, vcT, lam, subg, lam_init):
    n = qT.shape[1]
    nc = kc.shape[0]
    w = DIFF_WIDTH
    small = [_resident((4, HEAD_DIM)), _resident((2 * HEAD_DIM, 1))]
    if k is not None:
        tq, tk = DIFF_TQ, DIFF_TK
        n_tiles = v3.shape[0]
        assert v3.shape == (k.shape[0] // tk, w, tk) and n_tiles % DIFF_SWEEP == 0, (v3.shape, k.shape)
        in_specs = [
            pl.BlockSpec((w, tq), lambda i: (0, i)),
            _resident((k.shape[0], w)),
            _resident(v3.shape),
            pl.BlockSpec((nc, w), lambda i: (0, 0)),
            pl.BlockSpec((w, nc), lambda i: (0, 0)),
        ] + small
        args = (qT, k, v3, kc, vcT, lam, subg)
    else:
        tq, tk, n_tiles = n, 0, 0
        in_specs = [
            pl.BlockSpec((w, tq), lambda i: (0, 0)),
            pl.BlockSpec((nc, w), lambda i: (0, 0)),
            pl.BlockSpec((w, nc), lambda i: (0, 0)),
        ] + small
        args = (qT, kc, vcT, lam, subg)
    return pl.pallas_call(
        functools.partial(_diff_kernel, lam_init=lam_init, n_tiles=n_tiles),
        grid=(n // tq,),
        in_specs=in_specs,
        out_specs=pl.BlockSpec((tq, w), lambda i: (i, 0)),
        out_shape=jax.ShapeDtypeStruct((n, w), BF16),
        scratch_shapes=[
            pltpu.VMEM((2 * DIFF_HEADS, LANES, tq), BF16),
            pltpu.VMEM((2 * DIFF_HEADS, 1, tq), F32),
            pltpu.VMEM((2 * DIFF_HEADS, 1, tq), F32),
            pltpu.VMEM((2 * DIFF_HEADS, 1, tq), F32),
            pltpu.VMEM((2 * DIFF_HEADS, LANES, tq), F32),
            pltpu.VMEM((2 * DIFF_DEPTH, max(tk, nc), DIFF_QC), F32),
            pltpu.VMEM((2, max(tk, nc), DIFF_QC), BF16),
        ],
        compiler_params=_cparams("arbitrary"),
        name="diff_attn" if n_tiles else "diff_attn_ctx",
    )(*args)


def _na_bias_kernel(rpb_ref, o_ref):
    h = pl.program_id(0)
    n_dr, n_dc = 2 * NA_MAX_ROWS - 1, 2 * NA_COLS - 1
    kc = lax.broadcasted_iota(jnp.int32, (GRID_W, LANES), 0)
    lane = lax.broadcasted_iota(jnp.int32, (GRID_W, LANES), 1)
    c = lane % GRID_W
    upper = lane >= GRID_W
    dc = kc - c + (NA_COLS - 1)
    cs = jnp.clip(c - NA_COLS // 2, 0, GRID_W - NA_COLS)
    col_ok = (kc >= cs) & (kc < cs + NA_COLS)
    for i in range(o_ref.shape[1]):
        dr_idx = (i - 1, i - 2)
        val = jnp.full((GRID_W, LANES), NEG_INF, F32)
        for d in range(n_dc):
            lo, up = (rpb_ref[(h * n_dr + r) * n_dc + d] * LOG2E if 0 <= r < n_dr else NEG_INF for r in dr_idx)
            val = jnp.where(dc == d, jnp.where(upper, up, lo), val)
        o_ref[0, i] = jnp.where(col_ok, val, NEG_INF)


def _na_bias_blocks(rpb):
    n_heads = rpb.shape[0]
    n_blocks = 2 * NA_MAX_ROWS
    return pl.pallas_call(
        _na_bias_kernel,
        grid=(n_heads,),
        in_specs=[pl.BlockSpec(memory_space=pltpu.SMEM)],
        out_specs=pl.BlockSpec((1, n_blocks, GRID_W, LANES), lambda h: (h, 0, 0, 0)),
        out_shape=jax.ShapeDtypeStruct((n_heads, n_blocks, GRID_W, LANES), F32),
        compiler_params=_cparams("arbitrary"),
        name="na_bias",
    )(rpb.astype(F32).reshape(-1))


def _na_window_bias(bias_ref, e, t, grid_rows):
    lane = lax.broadcasted_iota(jnp.int32, (GRID_W, LANES), 1)
    r0 = t * NA_TILE_ROWS
    rows = []
    for rk in range(3 * NA_TILE_ROWS):
        kr = r0 - NA_TILE_ROWS + rk
        cols = []
        for p in range(NA_TILE_ROWS // 2):
            blk = bias_ref[e, rk - NA_TILE_ROWS - 2 * p + NA_MAX_ROWS]
            r = r0 + 2 * p + lane // GRID_W
            rs = jnp.clip(r - NA_MAX_ROWS // 2, 0, grid_rows - NA_MAX_ROWS)
            ok = (kr >= rs) & (kr < rs + NA_MAX_ROWS)
            cols.append(jnp.where(ok, blk, NEG_INF))
        rows.append(jnp.concatenate(cols, axis=1))
    return jnp.concatenate(rows, axis=0)


def _na_kernel(*refs, has_latent, grid_rows):
    if has_latent:
        qT_ref, km_ref, k0_ref, kp_ref, vm_ref, v0_ref, vp_ref, kc_ref, vcT_ref, bias_ref, o_ref = refs
    else:
        qT_ref, kc_ref, vcT_ref, o_ref = refs
    items = [(pp, e) for pp in range(NA_STEP_PAIRS) for e in range(2)]

    def scores(item):
        pp, e = item
        lanes = slice(pp * LANES, (pp + 1) * LANES)
        qm = _half_masked(qT_ref[lanes, :], e)
        s = _dot(kc_ref[:, lanes], qm)
        if has_latent:
            s_nb = jnp.concatenate([_dot(r[:, lanes], qm) for r in (km_ref, k0_ref, kp_ref)], axis=0)
            s = jnp.concatenate([s_nb + _na_window_bias(bias_ref, 2 * pp + e, pl.program_id(1), grid_rows), s], axis=0)
        return s

    def attend(item, s):
        pp, e = item
        rows = slice(pp * LANES + e * HEAD_DIM, pp * LANES + (e + 1) * HEAD_DIM)
        vT = vcT_ref[rows, :]
        if has_latent:
            vT = jnp.concatenate([vm_ref[rows, :], v0_ref[rows, :], vp_ref[rows, :], vT], axis=1)
        p = jnp.exp2(s - jnp.max(s, axis=0, keepdims=True))
        pv = _dot(jnp.concatenate([vT, jnp.ones((ONES_ROWS, vT.shape[1]), BF16)], axis=0), p.astype(BF16))
        return pv[:HEAD_DIM] / pv[HEAD_DIM:HEAD_DIM + 1]

    pending = [scores(item) for item in items[:NA_DEPTH]]
    outs = []
    for i, item in enumerate(items):
        if i + NA_DEPTH < len(items):
            pending.append(scores(items[i + NA_DEPTH]))
        outs.append(attend(item, pending.pop(0)))
    o_ref[...] = jnp.concatenate(outs, axis=0).T.astype(o_ref.dtype)


def _na_attention(qT_all, k_all, vT_all, kc_all, vcT_all, bias, *, has_latent):
    n = qT_all.shape[1]
    nc = kc_all.shape[0]
    w = NA_STEP_PAIRS * LANES
    g0 = DIFF_WIDTH // w
    n_steps = NA_WIDTH // w
    if has_latent:
        tq = NA_TQ
        nt = n // tq
        prev = lambda t: jnp.maximum(t - 1, 0)
        nxt = lambda t: jnp.minimum(t + 1, nt - 1)
        in_specs = [
            pl.BlockSpec((w, tq), lambda hp, t: (g0 + hp, t)),
            pl.BlockSpec((tq, w), lambda hp, t: (prev(t), g0 + hp)),
            pl.BlockSpec((tq, w), lambda hp, t: (t, g0 + hp)),
            pl.BlockSpec((tq, w), lambda hp, t: (nxt(t), g0 + hp)),
            pl.BlockSpec((w, tq), lambda hp, t: (hp, prev(t))),
            pl.BlockSpec((w, tq), lambda hp, t: (hp, t)),
            pl.BlockSpec((w, tq), lambda hp, t: (hp, nxt(t))),
            pl.BlockSpec((nc, w), lambda hp, t: (0, g0 + hp)),
            pl.BlockSpec((w, nc), lambda hp, t: (g0 + hp, 0)),
            pl.BlockSpec((2 * NA_STEP_PAIRS,) + bias.shape[1:], lambda hp, t: (hp, 0, 0, 0)),
        ]
        args = (qT_all, k_all, k_all, k_all, vT_all, vT_all, vT_all, kc_all, vcT_all, bias)
    else:
        tq = n
        nt = 1
        in_specs = [
            pl.BlockSpec((w, tq), lambda hp, t: (g0 + hp, 0)),
            pl.BlockSpec((nc, w), lambda hp, t: (0, g0 + hp)),
            pl.BlockSpec((w, nc), lambda hp, t: (g0 + hp, 0)),
        ]
        args = (qT_all, kc_all, vcT_all)
    return pl.pallas_call(
        functools.partial(_na_kernel, has_latent=has_latent, grid_rows=n // GRID_W),
        grid=(n_steps, nt),
        in_specs=in_specs,
        out_specs=pl.BlockSpec((tq, w), lambda hp, t: (t, hp)),
        out_shape=jax.ShapeDtypeStruct((n, NA_WIDTH), BF16),
        compiler_params=_cparams("arbitrary", "arbitrary"),
        name="na_attn" if has_latent else "na_attn_ctx",
    )(*args)


def _swa_kernel(*refs, has_latent):
    if has_latent:
        qT_ref, km_ref, k0_ref, kp_ref, vm_ref, v0_ref, vp_ref, kc_ref, vcT_ref, sink_ref, o_ref = refs
        b = pl.program_id(0)
        nb = pl.num_programs(0)
        tq = qT_ref.shape[1]
        ki = lax.broadcasted_iota(jnp.int32, (tq, tq), 0)
        qi = lax.broadcasted_iota(jnp.int32, (tq, tq), 1)
        band_prev = jnp.concatenate([(ki >= qi) & (b > 0)] * SWA_ITEM_HEADS, axis=1)
        band_next = jnp.concatenate([(ki <= qi) & (b < nb - 1)] * SWA_ITEM_HEADS, axis=1)
    else:
        qT_ref, kc_ref, vcT_ref, sink_ref, o_ref = refs
        tq = qT_ref.shape[1]
    zeros = jnp.zeros((HEAD_DIM, tq), BF16)
    width = SWA_ITEM_HEADS * tq
    items = [(g, j0) for g in range(GQA_KV_HEADS) for j0 in range(0, GQA_GROUP, SWA_ITEM_HEADS)]

    def scores(item):
        g, j0 = item
        lanes = slice((g // 2) * LANES, (g // 2 + 1) * LANES)
        q_cols = []
        for jh in range(j0, j0 + SWA_ITEM_HEADS):
            hq = g * GQA_GROUP + jh
            qh = qT_ref[hq * HEAD_DIM:(hq + 1) * HEAD_DIM, :]
            q_cols.append(jnp.concatenate([qh, zeros] if g % 2 == 0 else [zeros, qh], axis=0))
        qm = jnp.concatenate(q_cols, axis=1)
        s = _dot(kc_ref[:, lanes], qm)
        if has_latent:
            s_prev = jnp.where(band_prev, _dot(km_ref[:, lanes], qm), NEG_INF)
            s_next = jnp.where(band_next, _dot(kp_ref[:, lanes], qm), NEG_INF)
            s = jnp.concatenate([s, s_prev, _dot(k0_ref[:, lanes], qm), s_next], axis=0)
        return s

    def attend(item, s):
        g, j0 = item
        rows = slice(g * HEAD_DIM, (g + 1) * HEAD_DIM)
        vT = vcT_ref[rows, :]
        if has_latent:
            vT = jnp.concatenate([vT, vm_ref[rows, :], v0_ref[rows, :], vp_ref[rows, :]], axis=1)
        sink = sink_ref[g][:, j0 * tq:j0 * tq + width] * LOG2E
        m = jnp.maximum(jnp.max(s, axis=0, keepdims=True), sink)
        p = jnp.exp2(s - m)
        pv = _dot(jnp.concatenate([vT, jnp.ones((ONES_ROWS, vT.shape[1]), BF16)], axis=0), p.astype(BF16))
        o = pv[:HEAD_DIM] / (pv[HEAD_DIM:HEAD_DIM + 1] + jnp.exp2(sink - m))
        return [o[:, jh * tq:(jh + 1) * tq] for jh in range(SWA_ITEM_HEADS)]

    pending = [scores(item) for item in items[:SWA_DEPTH]]
    outs = []
    for i, item in enumerate(items):
        if i + SWA_DEPTH < len(items):
            pending.append(scores(items[i + SWA_DEPTH]))
        outs += attend(item, pending.pop(0))
    o_ref[...] = jnp.concatenate(outs, axis=0).T.astype(o_ref.dtype)


def _swa_attention(qT, k, vT, kc, vcT, sinks, *, has_latent):
    n = qT.shape[1]
    nc = kc.shape[0]
    tq = SWA_TQ
    nb = n // tq
    sink_rows = jnp.repeat(sinks.astype(F32).reshape(GQA_KV_HEADS, 1, GQA_GROUP), tq, axis=2)
    ctx_specs = [_resident((nc, C_KV_COLS)), _resident((C_KV_COLS, nc)), _resident(sink_rows.shape)]
    if has_latent:
        prev = lambda b: jnp.maximum(b - 1, 0)
        nxt = lambda b: jnp.minimum(b + 1, nb - 1)
        in_specs = [
            pl.BlockSpec((C_Q_COLS, tq), lambda b: (0, b)),
            pl.BlockSpec((tq, C_KV_COLS), lambda b: (prev(b), 0)),
            pl.BlockSpec((tq, C_KV_COLS), lambda b: (b, 0)),
            pl.BlockSpec((tq, C_KV_COLS), lambda b: (nxt(b), 0)),
            pl.BlockSpec((C_KV_COLS, tq), lambda b: (0, prev(b))),
            pl.BlockSpec((C_KV_COLS, tq), lambda b: (0, b)),
            pl.BlockSpec((C_KV_COLS, tq), lambda b: (0, nxt(b))),
        ] + ctx_specs
        args = (qT, k, k, k, vT, vT, vT, kc, vcT, sink_rows)
    else:
        in_specs = [pl.BlockSpec((C_Q_COLS, tq), lambda b: (0, b))] + ctx_specs
        args = (qT, kc, vcT, sink_rows)
    return pl.pallas_call(
        functools.partial(_swa_kernel, has_latent=has_latent),
        grid=(nb,),
        in_specs=in_specs,
        out_specs=pl.BlockSpec((tq, C_Q_COLS), lambda b: (b, 0)),
        out_shape=jax.ShapeDtypeStruct((n, C_Q_COLS), BF16),
        compiler_params=_cparams("arbitrary"),
        name="swa_attn" if has_latent else "swa_attn_ctx",
    )(*args)


def _out_ffn_kernel(*refs, n_attn, final):
    x_ref = refs[0]
    attn_refs = refs[1:1 + n_attn]
    wout_ref, mod_ref, g_ref, w1_ref, w3_ref, w2_ref = refs[1 + n_attn:7 + n_attn]
    rest = refs[7 + n_attn:]
    if final:
        fg_ref, o_ref = rest
    else:
        (o_ref,) = rest
    proj = None
    r0 = 0
    for a_ref in attn_refs:
        wdt = a_ref.shape[1]
        part = _dot(a_ref[...], wout_ref[r0:r0 + wdt, :])
        proj = part if proj is None else proj + part
        r0 += wdt
    x1 = x_ref[...] + mod_ref[2:3, :] * proj
    h = _norm_mod(x1, g_ref[...], mod_ref[3:4, :], mod_ref[4:5, :]).astype(BF16)
    hidden = w1_ref.shape[1]
    chunk = FFN_CHUNK
    ffn = None
    for c0 in range(0, hidden, chunk):
        a = _dot(h, w1_ref[:, c0:c0 + chunk])
        b = _dot(h, w3_ref[:, c0:c0 + chunk])
        gated = (a * jax.nn.sigmoid(a) * b).astype(BF16)
        part = _dot(gated, w2_ref[c0:c0 + chunk, :])
        ffn = part if ffn is None else ffn + part
    x2 = x1 + mod_ref[5:6, :] * ffn
    if final:
        ms = jnp.mean(x2 * x2, axis=-1, keepdims=True)
        x2 = x2 * lax.rsqrt(ms + NORM_EPS) * fg_ref[...]
    o_ref[...] = x2


def _out_ffn(x, attn, w_out, mod, g, w1, w3, w2, final_g=None):
    n, d = x.shape
    tm = min(ROW_TILE, n)
    row = lambda i: (i, 0)
    final = final_g is not None
    in_specs = [pl.BlockSpec((tm, d), row)]
    in_specs += [pl.BlockSpec((tm, a.shape[1]), row) for a in attn]
    in_specs += [_resident(w_out.shape), _resident(mod.shape), _resident((1, d)),
                 _resident(w1.shape), _resident(w3.shape), _resident(w2.shape)]
    args = [x, *attn, w_out, mod, g, w1, w3, w2]
    if final:
        in_specs.append(_resident((1, d)))
        args.append(final_g)
    return pl.pallas_call(
        functools.partial(_out_ffn_kernel, n_attn=len(attn), final=final),
        grid=(n // tm,),
        in_specs=in_specs,
        out_specs=pl.BlockSpec((tm, d), row),
        out_shape=jax.ShapeDtypeStruct((n, d), F32),
        compiler_params=_cparams("arbitrary"),
        name="out_ffn",
    )(*args)


def _rope_tables(n):
    t = jnp.arange(n)
    row = (t // GRID_W).astype(F32)
    col = (t % GRID_W).astype(F32)
    quarter = HEAD_DIM // 4
    inv = ROPE_BASE ** (-jnp.arange(quarter, dtype=F32) / quarter)
    ar = row[:, None] * inv
    ac = col[:, None] * inv
    ang = jnp.concatenate([ar, ar, ac, ac], axis=-1)
    sign = jnp.tile(jnp.concatenate([-jnp.ones(quarter, F32), jnp.ones(quarter, F32)]), 2)
    cos = jnp.tile(jnp.cos(ang), (1, LANES // HEAD_DIM))
    sin = jnp.tile(jnp.sin(ang) * sign, (1, LANES // HEAD_DIM))
    return cos, sin


def _trunk(x, ctx, mods, norm_g, w_in_ab, w_out_ab, diff_lambda, diff_sub_g, na_rpb, w_in_c, w_out_c,
           attn_sinks, ffn_w1, ffn_w3, ffn_w2, final_g):
    depth = mods.shape[0]
    s_len = x.shape[0]
    tables = _rope_tables(s_len)
    for layer in range(depth):
        ctx_out = layer < depth - 1
        mod_x, mod_c = mods[layer, 0], mods[layer, 1]
        g0 = norm_g[layer, 0].reshape(1, -1)
        g1 = norm_g[layer, 1].reshape(1, -1)
        if layer % 2 == 0:
            e = layer // 2
            lam_init = 0.8 - 0.6 * math.exp(-0.3 * layer)
            w = w_in_ab[e]
            q_cols = DIFF_WIDTH + NA_WIDTH
            a_k, a_v, b_k, b_v = (w[:, q_cols + i * DIFF_WIDTH:q_cols + (i + 1) * DIFF_WIDTH] for i in range(4))
            w = jnp.concatenate([w[:, :q_cols], a_k, b_k, a_v, b_v], axis=1).astype(BF16)
            dims = dict(n_rope_q=DIFF_WIDTH, n_q=q_cols, n_rope_k=DIFF_WIDTH, n_k=q_cols)
            qT, k, vT, v3 = _proj_in(x, g0, mod_x, w, tables, n_v_tiled=DIFF_WIDTH, **dims)
            qcT, kc, vcT = _proj_in(ctx, g0, mod_c, w, None, **dims)
            lam = diff_lambda[e].astype(F32)
            subg = diff_sub_g[e].astype(F32).reshape(-1, 1)
            bias = _na_bias_blocks(na_rpb[e])
            a_x = _diff_attention(qT, k, v3, kc, vcT, lam, subg, lam_init)
            b_x = _na_attention(qT, k, vT, kc, vcT, bias, has_latent=True)
            attn_x = [a_x, b_x]
            if ctx_out:
                a_c = _diff_attention(qcT, None, None, kc, vcT, lam, subg, lam_init)
                b_c = _na_attention(qcT, None, None, kc, vcT, None, has_latent=False)
                attn_c = [a_c, b_c]
            w_out = w_out_ab[e].astype(BF16)
        else:
            o = layer // 2
            w = w_in_c[o].astype(BF16)
            dims = dict(n_rope_q=C_Q_COLS, n_q=C_Q_COLS, n_rope_k=C_KV_COLS, n_k=C_KV_COLS)
            qT, k, vT = _proj_in(x, g0, mod_x, w, tables, **dims)
            qcT, kc, vcT = _proj_in(ctx, g0, mod_c, w, None, **dims)
            attn_x = [_swa_attention(qT, k, vT, kc, vcT, attn_sinks[o], has_latent=True)]
            if ctx_out:
                attn_c = [_swa_attention(qcT, None, None, kc, vcT, attn_sinks[o], has_latent=False)]
            w_out = w_out_c[o].astype(BF16)
        w1, w3, w2 = ffn_w1[layer].astype(BF16), ffn_w3[layer].astype(BF16), ffn_w2[layer].astype(BF16)
        fg = final_g.reshape(1, -1) if layer == depth - 1 else None
        x = _out_ffn(x, attn_x, w_out, mod_x, g1, w1, w3, w2, fg)
        if ctx_out:
            ctx = _out_ffn(ctx, attn_c, w_out, mod_c, g1, w1, w3, w2)
    return x


def kernel(x, c, ctx, c_ctx, ada_w, ada_b, norm_g, w_in_ab, w_out_ab, diff_lambda, diff_sub_g, na_rpb, w_in_c,
           w_out_c, attn_sinks, ffn_w1, ffn_w3, ffn_w2, final_g):
    batch, _, d = x.shape
    depth = ada_w.shape[0]
    outs = []
    for b in range(batch):
        cc = jnp.zeros((8, d), F32).at[0].set(c[b].astype(F32)).at[1].set(c_ctx.astype(F32))
        mods = _adaln(cc, ada_w, ada_b)[:, :2].reshape(depth, 2, 6, d)
        outs.append(_trunk(x[b], ctx[b], mods, norm_g, w_in_ab, w_out_ab, diff_lambda, diff_sub_g, na_rpb,
                           w_in_c, w_out_c, attn_sinks, ffn_w1, ffn_w3, ffn_w2, final_g))
    return jnp.stack(outs).astype(x.dtype)
```

```python
import functools
import math

import jax
import jax.numpy as jnp
from jax import lax
from jax.experimental import pallas as pl
from jax.experimental.pallas import tpu as pltpu

F32 = jnp.float32
BF16 = jnp.bfloat16

D_MODEL = 1024
GRID_W = 64
HEAD_DIM = 64
ROPE_BASE = 10000.0
NORM_EPS = 1e-6
NEG_INF = -1e30
DIFF_HEADS = 4
DIFF_WIDTH = DIFF_HEADS * 2 * HEAD_DIM
NA_HEADS = 8
NA_WIDTH = NA_HEADS * HEAD_DIM
NA_MAX_ROWS = 8
NA_COLS = 16
GQA_HEADS = 16
GQA_KV_HEADS = 4
GQA_GROUP = GQA_HEADS // GQA_KV_HEADS
SWA_WINDOW = 128
C_Q_COLS = GQA_HEADS * HEAD_DIM
C_KV_COLS = GQA_KV_HEADS * HEAD_DIM
LOG2E = math.log2(math.e)
Q_SCALE = HEAD_DIM ** -0.5 * LOG2E

LANES = 128
ONES_ROWS = 16
VMEM_LIMIT = 56 * 1024 * 1024

ROW_TILE = 512
FFN_CHUNK = 256
DIFF_TQ = 1024
DIFF_TK = 512
DIFF_QC = 256
DIFF_SWEEP = 8
DIFF_RC = 64
DIFF_DEPTH = 4
assert DIFF_TK == ROW_TILE
NA_TILE_ROWS = 4
NA_TQ = NA_TILE_ROWS * GRID_W
NA_STEP_PAIRS = 4
NA_DEPTH = 2
SWA_TQ = SWA_WINDOW
SWA_ITEM_HEADS = 2
SWA_DEPTH = 3


def _cparams(*sem):
    return pltpu.CompilerParams(dimension_semantics=sem, vmem_limit_bytes=VMEM_LIMIT)


def _resident(shape):
    nd = len(shape)
    return pl.BlockSpec(shape, lambda *_: (0,) * nd, pipeline_mode=pl.Buffered(1))


def _dot(a, b):
    return jnp.dot(a, b, preferred_element_type=F32)


def _adaln_kernel(cc_ref, w_ref, b_ref, o_ref):
    cc = cc_ref[...]
    h = (cc * jax.nn.sigmoid(cc)).astype(BF16)
    o_ref[0] = _dot(h, w_ref[0].astype(BF16)) + b_ref[0]


def _adaln(cc, ada_w, ada_b):
    depth, d, d6 = ada_w.shape
    nj = d6 // d
    return pl.pallas_call(
        _adaln_kernel,
        grid=(depth, nj),
        in_specs=[
            pl.BlockSpec((8, d), lambda l, j: (0, 0)),
            pl.BlockSpec((1, d, d), lambda l, j: (l, 0, j)),
            pl.BlockSpec((1, 1, d), lambda l, j: (l, 0, j)),
        ],
        out_specs=pl.BlockSpec((1, 8, d), lambda l, j: (l, 0, j)),
        out_shape=jax.ShapeDtypeStruct((depth, 8, d6), F32),
        compiler_params=_cparams("arbitrary", "arbitrary"),
        name="adaln",
    )(cc, ada_w, ada_b.reshape(depth, 1, d6))


def _norm_mod(x, g, shift, scale):
    ms = jnp.mean(x * x, axis=-1, keepdims=True)
    h = x * lax.rsqrt(ms + NORM_EPS) * g
    return h * (1.0 + scale) + shift


def _rope_lanes(t, cos, sin_signed):
    lane = lax.broadcasted_iota(jnp.int32, (t.shape[0], LANES), 1)
    first = (lane % (HEAD_DIM // 2)) < (HEAD_DIM // 4)
    out = []
    for c in range(t.shape[1] // LANES):
        tc = t[:, c * LANES:(c + 1) * LANES]
        partner = jnp.where(first, pltpu.roll(tc, LANES - HEAD_DIM // 4, 1), pltpu.roll(tc, HEAD_DIM // 4, 1))
        out.append(tc * cos + partner * sin_signed)
    return jnp.concatenate(out, axis=1)


def _take_cols(y, cols):
    parts = [y[:, a:b] for a, b in cols]
    return parts[0] if len(parts) == 1 else jnp.concatenate(parts, axis=1)


def _proj_in_kernel(*refs, n_rope_q, n_q, n_rope_k, k_cols, v_cols, vt_cols, rope):
    n_in = 6 if rope else 4
    if rope:
        x_ref, g_ref, mod_ref, cos_ref, sin_ref, w_ref = refs[:n_in]
    else:
        x_ref, g_ref, mod_ref, w_ref = refs[:n_in]
    qT_ref, k_ref, vT_ref = refs[n_in:n_in + 3]
    h = _norm_mod(x_ref[...], g_ref[...], mod_ref[0:1, :], mod_ref[1:2, :])
    y = _dot(h.astype(BF16), w_ref[...])
    q = y[:, :n_q]
    k = _take_cols(y, k_cols)
    if rope:
        cos, sin = cos_ref[...], sin_ref[...]
        q = jnp.concatenate([_rope_lanes(q[:, :n_rope_q], cos, sin), q[:, n_rope_q:]], axis=1) \
            if n_rope_q < n_q else _rope_lanes(q, cos, sin)
        k = jnp.concatenate([_rope_lanes(k[:, :n_rope_k], cos, sin), k[:, n_rope_k:]], axis=1) \
            if n_rope_k < k.shape[1] else _rope_lanes(k, cos, sin)
    qT_ref[...] = (q * Q_SCALE).T.astype(BF16)
    k_ref[...] = k.astype(BF16)
    vT_ref[...] = _take_cols(y, v_cols).T.astype(BF16)
    if vt_cols:
        refs[n_in + 3][0] = _take_cols(y, vt_cols).T.astype(BF16)


def _proj_in(x, g, mod, w, tables, *, n_rope_q, n_q, n_rope_k, k_cols, v_cols, vt_cols=()):
    n, d = x.shape
    n_k, n_v, n_v_tiled = (sum(b - a for a, b in cols) for cols in (k_cols, v_cols, vt_cols))
    tm = min(ROW_TILE, n)
    rope = tables is not None
    row = lambda i: (i, 0)
    in_specs = [pl.BlockSpec((tm, d), row), _resident((1, d)), _resident(mod.shape)]
    args = [x, g, mod]
    if rope:
        in_specs += [pl.BlockSpec((tm, LANES), row), pl.BlockSpec((tm, LANES), row)]
        args += list(tables)
    in_specs.append(_resident(w.shape))
    args.append(w)
    out_specs = [
        pl.BlockSpec((n_q, tm), lambda i: (0, i)),
        pl.BlockSpec((tm, n_k), row),
        pl.BlockSpec((n_v, tm), lambda i: (0, i)),
    ]
    out_shape = [
        jax.ShapeDtypeStruct((n_q, n), BF16),
        jax.ShapeDtypeStruct((n, n_k), BF16),
        jax.ShapeDtypeStruct((n_v, n), BF16),
    ]
    if n_v_tiled:
        out_specs.append(pl.BlockSpec((1, n_v_tiled, tm), lambda i: (i, 0, 0)))
        out_shape.append(jax.ShapeDtypeStruct((n // tm, n_v_tiled, tm), BF16))
    return pl.pallas_call(
        functools.partial(_proj_in_kernel, n_rope_q=n_rope_q, n_q=n_q, n_rope_k=n_rope_k, k_cols=k_cols,
                          v_cols=v_cols, vt_cols=vt_cols, rope=rope),
        grid=(n // tm,),
        in_specs=in_specs,
        out_specs=out_specs,
        out_shape=out_shape,
        compiler_params=_cparams("arbitrary"),
        name="proj_in",
    )(*args)


def _half_masked(qT_pair, e):
    z = jnp.zeros((HEAD_DIM, qT_pair.shape[1]), qT_pair.dtype)
    if e == 0:
        return jnp.concatenate([qT_pair[:HEAD_DIM], z], axis=0)
    return jnp.concatenate([z, qT_pair[HEAD_DIM:]], axis=0)


def _diff_kernel(*refs, lam_init, n_tiles):
    if n_tiles:
        qT_ref, k_ref, v3_ref, kc_ref, vcT_ref, lam_ref, subg_ref, o_ref = refs[:8]
    else:
        qT_ref, kc_ref, vcT_ref, lam_ref, subg_ref, o_ref = refs[:6]
    qm_ref, m_ref, alpha_ref, l_ref, acc_ref, s_ref, p_ref = refs[-7:]

    tq = qm_ref.shape[2]
    items = [(h, e, c) for h in range(DIFF_HEADS) for e in range(2) for c in range(0, tq, DIFF_QC)]
    n = len(items)
    s_slots = s_ref.shape[0]
    assert n % s_slots == 0 and s_slots > DIFF_DEPTH, (n, s_slots)
    sub = 8

    def keys_of(tile, h):
        lanes = slice(h * LANES, (h + 1) * LANES)
        if tile is None:
            return kc_ref[:, lanes]
        return k_ref[pl.ds(pl.multiple_of(tile * DIFF_TK, DIFF_TK), DIFF_TK), lanes]

    def vals_of(tile, h):
        rows = slice(h * LANES, (h + 1) * LANES)
        return vcT_ref[rows, :] if tile is None else v3_ref[tile, rows, :]

    def scores(i, tile):
        h, e, c = items[i]
        mp = 2 * h + e
        keys = keys_of(tile, h)
        nk = keys.shape[0]
        s = _dot(keys, qm_ref[mp, :, c:c + DIFF_QC])
        s_ref[i % s_slots, :nk, :] = s
        m_part = jnp.max(s.reshape(nk // sub, sub, DIFF_QC), axis=0)
        m_old = m_ref[mp, :, c:c + DIFF_QC]
        m_new = jnp.maximum(m_old, jnp.max(m_part, axis=0, keepdims=True))
        m_ref[mp, :, c:c + DIFF_QC] = m_new
        alpha_ref[mp, :, c:c + DIFF_QC] = jnp.exp2(m_old - m_new)

    def absorb(i, tile):
        h, e, c = items[i]
        mp = 2 * h + e
        cols = slice(c, c + DIFF_QC)
        vals = vals_of(tile, h)
        nk = vals.shape[1]
        m_new = m_ref[mp, :, cols]
        alpha = alpha_ref[mp, :, cols]
        for r in range(0, nk, DIFF_RC):
            p = jnp.exp2(s_ref[i % s_slots, r:r + DIFF_RC, :] - m_new)
            p_ref[i % 2, r:r + DIFF_RC, :] = p.astype(BF16)
        pv = _dot(jnp.concatenate([vals, jnp.ones((ONES_ROWS, nk), BF16)], axis=0), p_ref[i % 2, :nk, :])
        l_ref[mp, :, cols] = alpha * l_ref[mp, :, cols] + pv[LANES:LANES + 1]
        acc_ref[mp, :, cols] = alpha * acc_ref[mp, :, cols] + pv[:LANES]

    def sweep(tiles):
        work = [(i, tile) for tile in tiles for i in range(n)]
        for i, tile in work[:DIFF_DEPTH]:
            scores(i, tile)
        for j, (i, tile) in enumerate(work):
            if j + DIFF_DEPTH < len(work):
                scores(*work[j + DIFF_DEPTH])
            absorb(i, tile)

    for h in range(DIFF_HEADS):
        pair = qT_ref[h * LANES:(h + 1) * LANES, :]
        for e in range(2):
            qm_ref[2 * h + e] = _half_masked(pair, e)
    m_ref[...] = jnp.full(m_ref.shape, NEG_INF, F32)
    l_ref[...] = jnp.zeros(l_ref.shape, F32)
    acc_ref[...] = jnp.zeros(acc_ref.shape, F32)
    sweep([None])
    if n_tiles:
        def body(t, carry):
            sweep([DIFF_SWEEP * t + u for u in range(DIFF_SWEEP)])
            return carry
        lax.fori_loop(0, n_tiles // DIFF_SWEEP, body, 0)

    lam = lam_ref[...]
    lam_val = (jnp.exp(jnp.sum(lam[0:1] * lam[1:2], axis=1, keepdims=True))
               - jnp.exp(jnp.sum(lam[2:3] * lam[3:4], axis=1, keepdims=True)) + lam_init)
    subg = subg_ref[...]
    for h in range(DIFF_HEADS):
        a0 = acc_ref[2 * h] / l_ref[2 * h]
        a1 = acc_ref[2 * h + 1] / l_ref[2 * h + 1]
        a = a0 - lam_val * a1
        ms = jnp.mean(a * a, axis=0, keepdims=True)
        a = a * lax.rsqrt(ms + NORM_EPS) * subg * (1.0 - lam_init)
        o_ref[:, h * LANES:(h + 1) * LANES] = a.T.astype(o_ref.dtype)


def _diff_attention(qT, k, v3, kc, vcT, lam, subg, lam_init):
    n = qT.shape[1]
    nc = kc.shape[0]
    w = DIFF_WIDTH
    small = [_resident((4, HEAD_DIM)), _resident((2 * HEAD_DIM, 1))]
    if k is not None:
        tq, tk = DIFF_TQ, DIFF_TK
        n_tiles = v3.shape[0]
        assert v3.shape == (k.shape[0] // tk, w, tk) and n_tiles % DIFF_SWEEP == 0, (v3.shape, k.shape)
        in_specs = [
            pl.BlockSpec((w, tq), lambda i: (0, i)),
            _resident((k.shape[0], w)),
            _resident(v3.shape),
            pl.BlockSpec((nc, w), lambda i: (0, 0)),
            pl.BlockSpec((w, nc), lambda i: (0, 0)),
        ] + small
        args = (qT, k, v3, kc, vcT, lam, subg)
    else:
        tq, tk, n_tiles = n, 0, 0
        in_specs = [
            pl.BlockSpec((w, tq), lambda i: (0, 0)),
            pl.BlockSpec((nc, w), lambda i: (0, 0)),
            pl.BlockSpec((w, nc), lambda i: (0, 0)),
        ] + small
        args = (qT, kc, vcT, lam, subg)
    return pl.pallas_call(
        functools.partial(_diff_kernel, lam_init=lam_init, n_tiles=n_tiles),
        grid=(n // tq,),
        in_specs=in_specs,
        out_specs=pl.BlockSpec((tq, w), lambda i: (i, 0)),
        out_shape=jax.ShapeDtypeStruct((n, w), BF16),
        scratch_shapes=[
            pltpu.VMEM((2 * DIFF_HEADS, LANES, tq), BF16),
            pltpu.VMEM((2 * DIFF_HEADS, 1, tq), F32),
            pltpu.VMEM((2 * DIFF_HEADS, 1, tq), F32),
            pltpu.VMEM((2 * DIFF_HEADS, 1, tq), F32),
            pltpu.VMEM((2 * DIFF_HEADS, LANES, tq), F32),
            pltpu.VMEM((2 * DIFF_DEPTH, max(tk, nc), DIFF_QC), F32),
            pltpu.VMEM((2, max(tk, nc), DIFF_QC), BF16),
        ],
        compiler_params=_cparams("arbitrary"),
        name="diff_attn" if n_tiles else "diff_attn_ctx",
    )(*args)


def _na_bias_kernel(rpb_ref, o_ref):
    h = pl.program_id(0)
    n_dr, n_dc = 2 * NA_MAX_ROWS - 1, 2 * NA_COLS - 1
    kc = lax.broadcasted_iota(jnp.int32, (GRID_W, LANES), 0)
    lane = lax.broadcasted_iota(jnp.int32, (GRID_W, LANES), 1)
    c = lane % GRID_W
    upper = lane >= GRID_W
    dc = kc - c + (NA_COLS - 1)
    cs = jnp.clip(c - NA_COLS // 2, 0, GRID_W - NA_COLS)
    col_ok = (kc >= cs) & (kc < cs + NA_COLS)
    for i in range(o_ref.shape[1]):
        dr_idx = (i - 1, i - 2)
        val = jnp.full((GRID_W, LANES), NEG_INF, F32)
        for d in range(n_dc):
            lo, up = (rpb_ref[(h * n_dr + r) * n_dc + d] * LOG2E if 0 <= r < n_dr else NEG_INF for r in dr_idx)
            val = jnp.where(dc == d, jnp.where(upper, up, lo), val)
        o_ref[0, i] = jnp.where(col_ok, val, NEG_INF)


def _na_bias_blocks(rpb):
    n_heads = rpb.shape[0]
    n_blocks = 2 * NA_MAX_ROWS
    return pl.pallas_call(
        _na_bias_kernel,
        grid=(n_heads,),
        in_specs=[pl.BlockSpec(memory_space=pltpu.SMEM)],
        out_specs=pl.BlockSpec((1, n_blocks, GRID_W, LANES), lambda h: (h, 0, 0, 0)),
        out_shape=jax.ShapeDtypeStruct((n_heads, n_blocks, GRID_W, LANES), F32),
        compiler_params=_cparams("arbitrary"),
        name="na_bias",
    )(rpb.astype(F32).reshape(-1))


def _na_window_bias(bias_ref, e, t, grid_rows):
    lane = lax.broadcasted_iota(jnp.int32, (GRID_W, LANES), 1)
    r0 = t * NA_TILE_ROWS
    rows = []
    for rk in range(3 * NA_TILE_ROWS):
        kr = r0 - NA_TILE_ROWS + rk
        cols = []
        for p in range(NA_TILE_ROWS // 2):
            blk = bias_ref[e, rk - NA_TILE_ROWS - 2 * p + NA_MAX_ROWS]
            r = r0 + 2 * p + lane // GRID_W
            rs = jnp.clip(r - NA_MAX_ROWS // 2, 0, grid_rows - NA_MAX_ROWS)
            ok = (kr >= rs) & (kr < rs + NA_MAX_ROWS)
            cols.append(jnp.where(ok, blk, NEG_INF))
        rows.append(jnp.concatenate(cols, axis=1))
    return jnp.concatenate(rows, axis=0)


def _na_kernel(*refs, has_latent, grid_rows):
    if has_latent:
        qT_ref, km_ref, k0_ref, kp_ref, vm_ref, v0_ref, vp_ref, kc_ref, vcT_ref, bias_ref, o_ref = refs
    else:
        qT_ref, kc_ref, vcT_ref, o_ref = refs
    items = [(pp, e) for pp in range(NA_STEP_PAIRS) for e in range(2)]

    def scores(item):
        pp, e = item
        lanes = slice(pp * LANES, (pp + 1) * LANES)
        qm = _half_masked(qT_ref[lanes, :], e)
        s = _dot(kc_ref[:, lanes], qm)
        if has_latent:
            s_nb = jnp.concatenate([_dot(r[:, lanes], qm) for r in (km_ref, k0_ref, kp_ref)], axis=0)
            s = jnp.concatenate([s_nb + _na_window_bias(bias_ref, 2 * pp + e, pl.program_id(1), grid_rows), s], axis=0)
        return s

    def attend(item, s):
        pp, e = item
        rows = slice(pp * LANES + e * HEAD_DIM, pp * LANES + (e + 1) * HEAD_DIM)
        vT = vcT_ref[rows, :]
        if has_latent:
            vT = jnp.concatenate([vm_ref[rows, :], v0_ref[rows, :], vp_ref[rows, :], vT], axis=1)
        p = jnp.exp2(s - jnp.max(s, axis=0, keepdims=True))
        pv = _dot(jnp.concatenate([vT, jnp.ones((ONES_ROWS, vT.shape[1]), BF16)], axis=0), p.astype(BF16))
        return pv[:HEAD_DIM] / pv[HEAD_DIM:HEAD_DIM + 1]

    pending = [scores(item) for item in items[:NA_DEPTH]]
    outs = []
    for i, item in enumerate(items):
        if i + NA_DEPTH < len(items):
            pending.append(scores(items[i + NA_DEPTH]))
        outs.append(attend(item, pending.pop(0)))
    o_ref[...] = jnp.concatenate(outs, axis=0).T.astype(o_ref.dtype)


def _na_attention(qT_all, k_all, vT_all, kc_all, vcT_all, bias, *, has_latent):
    n = qT_all.shape[1]
    nc = kc_all.shape[0]
    w = NA_STEP_PAIRS * LANES
    g0 = DIFF_WIDTH // w
    n_steps = NA_WIDTH // w
    if has_latent:
        tq = NA_TQ
        nt = n // tq
        prev = lambda t: jnp.maximum(t - 1, 0)
        nxt = lambda t: jnp.minimum(t + 1, nt - 1)
        in_specs = [
            pl.BlockSpec((w, tq), lambda hp, t: (g0 + hp, t)),
            pl.BlockSpec((tq, w), lambda hp, t: (prev(t), g0 + hp)),
            pl.BlockSpec((tq, w), lambda hp, t: (t, g0 + hp)),
            pl.BlockSpec((tq, w), lambda hp, t: (nxt(t), g0 + hp)),
            pl.BlockSpec((w, tq), lambda hp, t: (hp, prev(t))),
            pl.BlockSpec((w, tq), lambda hp, t: (hp, t)),
            pl.BlockSpec((w, tq), lambda hp, t: (hp, nxt(t))),
            pl.BlockSpec((nc, w), lambda hp, t: (0, g0 + hp)),
            pl.BlockSpec((w, nc), lambda hp, t: (g0 + hp, 0)),
            pl.BlockSpec((2 * NA_STEP_PAIRS,) + bias.shape[1:], lambda hp, t: (hp, 0, 0, 0)),
        ]
        args = (qT_all, k_all, k_all, k_all, vT_all, vT_all, vT_all, kc_all, vcT_all, bias)
    else:
        tq = n
        nt = 1
        in_specs = [
            pl.BlockSpec((w, tq), lambda hp, t: (g0 + hp, 0)),
            pl.BlockSpec((nc, w), lambda hp, t: (0, g0 + hp)),
            pl.BlockSpec((w, nc), lambda hp, t: (g0 + hp, 0)),
        ]
        args = (qT_all, kc_all, vcT_all)
    return pl.pallas_call(
        functools.partial(_na_kernel, has_latent=has_latent, grid_rows=n // GRID_W),
        grid=(n_steps, nt),
        in_specs=in_specs,
        out_specs=pl.BlockSpec((tq, w), lambda hp, t: (t, hp)),
        out_shape=jax.ShapeDtypeStruct((n, NA_WIDTH), BF16),
        compiler_params=_cparams("arbitrary", "arbitrary"),
        name="na_attn" if has_latent else "na_attn_ctx",
    )(*args)


def _swa_kernel(*refs, has_latent):
    if has_latent:
        qT_ref, km_ref, k0_ref, kp_ref, vm_ref, v0_ref, vp_ref, kc_ref, vcT_ref, sink_ref, o_ref = refs
        b = pl.program_id(0)
        nb = pl.num_programs(0)
        tq = qT_ref.shape[1]
        ki = lax.broadcasted_iota(jnp.int32, (tq, tq), 0)
        qi = lax.broadcasted_iota(jnp.int32, (tq, tq), 1)
        band_prev = jnp.concatenate([(ki >= qi) & (b > 0)] * SWA_ITEM_HEADS, axis=1)
        band_next = jnp.concatenate([(ki <= qi) & (b < nb - 1)] * SWA_ITEM_HEADS, axis=1)
    else:
        qT_ref, kc_ref, vcT_ref, sink_ref, o_ref = refs
        tq = qT_ref.shape[1]
    zeros = jnp.zeros((HEAD_DIM, tq), BF16)
    width = SWA_ITEM_HEADS * tq
    items = [(g, j0) for g in range(GQA_KV_HEADS) for j0 in range(0, GQA_GROUP, SWA_ITEM_HEADS)]

    def scores(item):
        g, j0 = item
        lanes = slice((g // 2) * LANES, (g // 2 + 1) * LANES)
        q_cols = []
        for jh in range(j0, j0 + SWA_ITEM_HEADS):
            hq = g * GQA_GROUP + jh
            qh = qT_ref[hq * HEAD_DIM:(hq + 1) * HEAD_DIM, :]
            q_cols.append(jnp.concatenate([qh, zeros] if g % 2 == 0 else [zeros, qh], axis=0))
        qm = jnp.concatenate(q_cols, axis=1)
        s = _dot(kc_ref[:, lanes], qm)
        if has_latent:
            s_prev = jnp.where(band_prev, _dot(km_ref[:, lanes], qm), NEG_INF)
            s_next = jnp.where(band_next, _dot(kp_ref[:, lanes], qm), NEG_INF)
            s = jnp.concatenate([s, s_prev, _dot(k0_ref[:, lanes], qm), s_next], axis=0)
        return s

    def attend(item, s):
        g, j0 = item
        rows = slice(g * HEAD_DIM, (g + 1) * HEAD_DIM)
        vT = vcT_ref[rows, :]
        if has_latent:
            vT = jnp.concatenate([vT, vm_ref[rows, :], v0_ref[rows, :], vp_ref[rows, :]], axis=1)
        sink = sink_ref[g][:, j0 * tq:j0 * tq + width] * LOG2E
        m = jnp.maximum(jnp.max(s, axis=0, keepdims=True), sink)
        p = jnp.exp2(s - m)
        pv = _dot(jnp.concatenate([vT, jnp.ones((ONES_ROWS, vT.shape[1]), BF16)], axis=0), p.astype(BF16))
        o = pv[:HEAD_DIM] / (pv[HEAD_DIM:HEAD_DIM + 1] + jnp.exp2(sink - m))
        return [o[:, jh * tq:(jh + 1) * tq] for jh in range(SWA_ITEM_HEADS)]

    pending = [scores(item) for item in items[:SWA_DEPTH]]
    outs = []
    for i, item in enumerate(items):
        if i + SWA_DEPTH < len(items):
            pending.append(scores(items[i + SWA_DEPTH]))
        outs += attend(item, pending.pop(0))
    o_ref[...] = jnp.concatenate(outs, axis=0).T.astype(o_ref.dtype)


def _swa_attention(qT, k, vT, kc, vcT, sinks, *, has_latent):
    n = qT.shape[1]
    nc = kc.shape[0]
    tq = SWA_TQ
    nb = n // tq
    sink_rows = jnp.repeat(sinks.astype(F32).reshape(GQA_KV_HEADS, 1, GQA_GROUP), tq, axis=2)
    ctx_specs = [_resident((nc, C_KV_COLS)), _resident((C_KV_COLS, nc)), _resident(sink_rows.shape)]
    if has_latent:
        prev = lambda b: jnp.maximum(b - 1, 0)
        nxt = lambda b: jnp.minimum(b + 1, nb - 1)
        in_specs = [
            pl.BlockSpec((C_Q_COLS, tq), lambda b: (0, b)),
            pl.BlockSpec((tq, C_KV_COLS), lambda b: (prev(b), 0)),
            pl.BlockSpec((tq, C_KV_COLS), lambda b: (b, 0)),
            pl.BlockSpec((tq, C_KV_COLS), lambda b: (nxt(b), 0)),
            pl.BlockSpec((C_KV_COLS, tq), lambda b: (0, prev(b))),
            pl.BlockSpec((C_KV_COLS, tq), lambda b: (0, b)),
            pl.BlockSpec((C_KV_COLS, tq), lambda b: (0, nxt(b))),
        ] + ctx_specs
        args = (qT, k, k, k, vT, vT, vT, kc, vcT, sink_rows)
    else:
        in_specs = [pl.BlockSpec((C_Q_COLS, tq), lambda b: (0, b))] + ctx_specs
        args = (qT, kc, vcT, sink_rows)
    return pl.pallas_call(
        functools.partial(_swa_kernel, has_latent=has_latent),
        grid=(nb,),
        in_specs=in_specs,
        out_specs=pl.BlockSpec((tq, C_Q_COLS), lambda b: (b, 0)),
        out_shape=jax.ShapeDtypeStruct((n, C_Q_COLS), BF16),
        compiler_params=_cparams("arbitrary"),
        name="swa_attn" if has_latent else "swa_attn_ctx",
    )(*args)


def _out_ffn_kernel(*refs, n_attn, final):
    x_ref = refs[0]
    attn_refs = refs[1:1 + n_attn]
    wout_ref, mod_ref, g_ref, w1_ref, w3_ref, w2_ref = refs[1 + n_attn:7 + n_attn]
    rest = refs[7 + n_attn:]
    if final:
        fg_ref, o_ref = rest
    else:
        (o_ref,) = rest
    proj = None
    r0 = 0
    for a_ref in attn_refs:
        wdt = a_ref.shape[1]
        part = _dot(a_ref[...], wout_ref[r0:r0 + wdt, :])
        proj = part if proj is None else proj + part
        r0 += wdt
    x1 = x_ref[...] + mod_ref[2:3, :] * proj
    h = _norm_mod(x1, g_ref[...], mod_ref[3:4, :], mod_ref[4:5, :]).astype(BF16)
    hidden = w1_ref.shape[1]
    chunk = FFN_CHUNK
    ffn = None
    for c0 in range(0, hidden, chunk):
        a = _dot(h, w1_ref[:, c0:c0 + chunk])
        b = _dot(h, w3_ref[:, c0:c0 + chunk])
        gated = (a * jax.nn.sigmoid(a) * b).astype(BF16)
        part = _dot(gated, w2_ref[c0:c0 + chunk, :])
        ffn = part if ffn is None else ffn + part
    x2 = x1 + mod_ref[5:6, :] * ffn
    if final:
        ms = jnp.mean(x2 * x2, axis=-1, keepdims=True)
        x2 = x2 * lax.rsqrt(ms + NORM_EPS) * fg_ref[...]
    o_ref[...] = x2


def _out_ffn(x, attn, w_out, mod, g, w1, w3, w2, final_g=None):
    n, d = x.shape
    tm = min(ROW_TILE, n)
    row = lambda i: (i, 0)
    final = final_g is not None
    in_specs = [pl.BlockSpec((tm, d), row)]
    in_specs += [pl.BlockSpec((tm, a.shape[1]), row) for a in attn]
    in_specs += [_resident(w_out.shape), _resident(mod.shape), _resident((1, d)),
                 _resident(w1.shape), _resident(w3.shape), _resident(w2.shape)]
    args = [x, *attn, w_out, mod, g, w1, w3, w2]
    if final:
        in_specs.append(_resident((1, d)))
        args.append(final_g)
    return pl.pallas_call(
        functools.partial(_out_ffn_kernel, n_attn=len(attn), final=final),
        grid=(n // tm,),
        in_specs=in_specs,
        out_specs=pl.BlockSpec((tm, d), row),
        out_shape=jax.ShapeDtypeStruct((n, d), F32),
        compiler_params=_cparams("arbitrary"),
        name="out_ffn",
    )(*args)


def _rope_tables(n):
    t = jnp.arange(n)
    row = (t // GRID_W).astype(F32)
    col = (t % GRID_W).astype(F32)
    quarter = HEAD_DIM // 4
    inv = ROPE_BASE ** (-jnp.arange(quarter, dtype=F32) / quarter)
    ar = row[:, None] * inv
    ac = col[:, None] * inv
    ang = jnp.concatenate([ar, ar, ac, ac], axis=-1)
    sign = jnp.tile(jnp.concatenate([-jnp.ones(quarter, F32), jnp.ones(quarter, F32)]), 2)
    cos = jnp.tile(jnp.cos(ang), (1, LANES // HEAD_DIM))
    sin = jnp.tile(jnp.sin(ang) * sign, (1, LANES // HEAD_DIM))
    return cos, sin


def _trunk(x, ctx, mods, norm_g, w_in_ab, w_out_ab, diff_lambda, diff_sub_g, na_rpb, w_in_c, w_out_c,
           attn_sinks, ffn_w1, ffn_w3, ffn_w2, final_g):
    depth = mods.shape[0]
    s_len = x.shape[0]
    tables = _rope_tables(s_len)
    for layer in range(depth):
        ctx_out = layer < depth - 1
        mod_x, mod_c = mods[layer, 0], mods[layer, 1]
        g0 = norm_g[layer, 0].reshape(1, -1)
        g1 = norm_g[layer, 1].reshape(1, -1)
        if layer % 2 == 0:
            e = layer // 2
            lam_init = 0.8 - 0.6 * math.exp(-0.3 * layer)
            w = w_in_ab[e].astype(BF16)
            q_cols = DIFF_WIDTH + NA_WIDTH
            a_k, a_v, b_k, b_v = ((q_cols + i * DIFF_WIDTH, q_cols + (i + 1) * DIFF_WIDTH) for i in range(4))
            dims = dict(n_rope_q=DIFF_WIDTH, n_q=q_cols, n_rope_k=DIFF_WIDTH, k_cols=(a_k, b_k))
            qT, k, vT, v3 = _proj_in(x, g0, mod_x, w, tables, v_cols=(b_v,), vt_cols=(a_v,), **dims)
            qcT, kc, vcT = _proj_in(ctx, g0, mod_c, w, None, v_cols=(a_v, b_v), **dims)
            lam = diff_lambda[e].astype(F32)
            subg = diff_sub_g[e].astype(F32).reshape(-1, 1)
            bias = _na_bias_blocks(na_rpb[e])
            a_x = _diff_attention(qT, k, v3, kc, vcT, lam, subg, lam_init)
            b_x = _na_attention(qT, k, vT, kc, vcT, bias, has_latent=True)
            attn_x = [a_x, b_x]
            if ctx_out:
                a_c = _diff_attention(qcT, None, None, kc, vcT, lam, subg, lam_init)
                b_c = _na_attention(qcT, None, None, kc, vcT, None, has_latent=False)
                attn_c = [a_c, b_c]
            w_out = w_out_ab[e].astype(BF16)
        else:
            o = layer // 2
            w = w_in_c[o].astype(BF16)
            dims = dict(n_rope_q=C_Q_COLS, n_q=C_Q_COLS, n_rope_k=C_KV_COLS,
                        k_cols=((C_Q_COLS, C_Q_COLS + C_KV_COLS),),
                        v_cols=((C_Q_COLS + C_KV_COLS, C_Q_COLS + 2 * C_KV_COLS),))
            qT, k, vT = _proj_in(x, g0, mod_x, w, tables, **dims)
            qcT, kc, vcT = _proj_in(ctx, g0, mod_c, w, None, **dims)
            attn_x = [_swa_attention(qT, k, vT, kc, vcT, attn_sinks[o], has_latent=True)]
            if ctx_out:
                attn_c = [_swa_attention(qcT, None, None, kc, vcT, attn_sinks[o], has_latent=False)]
            w_out = w_out_c[o].astype(BF16)
        w1, w3, w2 = ffn_w1[layer].astype(BF16), ffn_w3[layer].astype(BF16), ffn_w2[layer].astype(BF16)
        fg = final_g.reshape(1, -1) if layer == depth - 1 else None
        x = _out_ffn(x, attn_x, w_out, mod_x, g1, w1, w3, w2, fg)
        if ctx_out:
            ctx = _out_ffn(ctx, attn_c, w_out, mod_c, g1, w1, w3, w2)
    return x


def kernel(x, c, ctx, c_ctx, ada_w, ada_b, norm_g, w_in_ab, w_out_ab, diff_lambda, diff_sub_g, na_rpb, w_in_c,
           w_out_c, attn_sinks, ffn_w1, ffn_w3, ffn_w2, final_g):
    batch, _, d = x.shape
    depth = ada_w.shape[0]
    outs = []
    for b in range(batch):
        cc = jnp.zeros((8, d), F32).at[0].set(c[b].astype(F32)).at[1].set(c_ctx.astype(F32))
        mods = _adaln(cc, ada_w, ada_b)[:, :2].reshape(depth, 2, 6, d)
        outs.append(_trunk(x[b], ctx[b], mods, norm_g, w_in_ab, w_out_ab, diff_lambda, diff_sub_g, na_rpb,
                           w_in_c, w_out_c, attn_sinks, ffn_w1, ffn_w3, ffn_w2, final_g))
    return jnp.stack(outs).astype(x.dtype)
```

```python
import functools
import math

import jax
import jax.numpy as jnp
from jax import lax
from jax.experimental import pallas as pl
from jax.experimental.pallas import tpu as pltpu

F32 = jnp.float32
BF16 = jnp.bfloat16

D_MODEL = 1024
GRID_W = 64
HEAD_DIM = 64
ROPE_BASE = 10000.0
NORM_EPS = 1e-6
NEG_INF = -1e30
DIFF_HEADS = 4
DIFF_WIDTH = DIFF_HEADS * 2 * HEAD_DIM
NA_HEADS = 8
NA_WIDTH = NA_HEADS * HEAD_DIM
NA_MAX_ROWS = 8
NA_COLS = 16
GQA_HEADS = 16
GQA_KV_HEADS = 4
GQA_GROUP = GQA_HEADS // GQA_KV_HEADS
SWA_WINDOW = 128
C_Q_COLS = GQA_HEADS * HEAD_DIM
C_KV_COLS = GQA_KV_HEADS * HEAD_DIM
LOG2E = math.log2(math.e)
Q_SCALE = HEAD_DIM ** -0.5 * LOG2E

LANES = 128
ONES_ROWS = 16
VMEM_LIMIT = 56 * 1024 * 1024

ROW_TILE = 512
FFN_CHUNK = 256
DIFF_TQ = 1024
DIFF_TK = 512
DIFF_QC = 256
DIFF_SWEEP = 8
DIFF_RC = 64
DIFF_DEPTH = 4
assert DIFF_TK == ROW_TILE
NA_TILE_ROWS = 4
NA_TQ = NA_TILE_ROWS * GRID_W
NA_STEP_PAIRS = 4
NA_DEPTH = 2
SWA_TQ = SWA_WINDOW
SWA_STEP_BLOCKS = 2
SWA_ITEM_HEADS = 2
SWA_DEPTH = 3


def _cparams(*sem):
    return pltpu.CompilerParams(dimension_semantics=sem, vmem_limit_bytes=VMEM_LIMIT)


def _resident(shape):
    nd = len(shape)
    return pl.BlockSpec(shape, lambda *_: (0,) * nd, pipeline_mode=pl.Buffered(1))


def _dot(a, b):
    return jnp.dot(a, b, preferred_element_type=F32)


def _adaln_kernel(cc_ref, w_ref, b_ref, o_ref):
    cc = cc_ref[...]
    h = (cc * jax.nn.sigmoid(cc)).astype(BF16)
    o_ref[0] = _dot(h, w_ref[0].astype(BF16)) + b_ref[0]


def _adaln(cc, ada_w, ada_b):
    depth, d, d6 = ada_w.shape
    nj = d6 // d
    return pl.pallas_call(
        _adaln_kernel,
        grid=(depth, nj),
        in_specs=[
            pl.BlockSpec((8, d), lambda l, j: (0, 0)),
            pl.BlockSpec((1, d, d), lambda l, j: (l, 0, j)),
            pl.BlockSpec((1, 1, d), lambda l, j: (l, 0, j)),
        ],
        out_specs=pl.BlockSpec((1, 8, d), lambda l, j: (l, 0, j)),
        out_shape=jax.ShapeDtypeStruct((depth, 8, d6), F32),
        compiler_params=_cparams("arbitrary", "arbitrary"),
        name="adaln",
    )(cc, ada_w, ada_b.reshape(depth, 1, d6))


def _norm_mod(x, g, shift, scale):
    ms = jnp.mean(x * x, axis=-1, keepdims=True)
    h = x * lax.rsqrt(ms + NORM_EPS) * g
    return h * (1.0 + scale) + shift


def _rope_lanes(t, cos, sin_signed):
    lane = lax.broadcasted_iota(jnp.int32, (t.shape[0], LANES), 1)
    first = (lane % (HEAD_DIM // 2)) < (HEAD_DIM // 4)
    out = []
    for c in range(t.shape[1] // LANES):
        tc = t[:, c * LANES:(c + 1) * LANES]
        partner = jnp.where(first, pltpu.roll(tc, LANES - HEAD_DIM // 4, 1), pltpu.roll(tc, HEAD_DIM // 4, 1))
        out.append(tc * cos + partner * sin_signed)
    return jnp.concatenate(out, axis=1)


def _take_cols(y, cols):
    parts = [y[:, a:b] for a, b in cols]
    return parts[0] if len(parts) == 1 else jnp.concatenate(parts, axis=1)


def _proj_in_kernel(*refs, n_rope_q, n_q, n_rope_k, k_cols, v_cols, vt_cols, rope):
    n_in = 6 if rope else 4
    if rope:
        x_ref, g_ref, mod_ref, cos_ref, sin_ref, w_ref = refs[:n_in]
    else:
        x_ref, g_ref, mod_ref, w_ref = refs[:n_in]
    qT_ref, k_ref, vT_ref = refs[n_in:n_in + 3]
    h = _norm_mod(x_ref[...], g_ref[...], mod_ref[0:1, :], mod_ref[1:2, :])
    y = _dot(h.astype(BF16), w_ref[...])
    q = y[:, :n_q]
    k = _take_cols(y, k_cols)
    if rope:
        cos, sin = cos_ref[...], sin_ref[...]
        q = jnp.concatenate([_rope_lanes(q[:, :n_rope_q], cos, sin), q[:, n_rope_q:]], axis=1) \
            if n_rope_q < n_q else _rope_lanes(q, cos, sin)
        k = jnp.concatenate([_rope_lanes(k[:, :n_rope_k], cos, sin), k[:, n_rope_k:]], axis=1) \
            if n_rope_k < k.shape[1] else _rope_lanes(k, cos, sin)
    qT_ref[...] = (q * Q_SCALE).T.astype(BF16)
    k_ref[...] = k.astype(BF16)
    vT_ref[...] = _take_cols(y, v_cols).T.astype(BF16)
    if vt_cols:
        refs[n_in + 3][0] = _take_cols(y, vt_cols).T.astype(BF16)


def _proj_in(x, g, mod, w, tables, *, n_rope_q, n_q, n_rope_k, k_cols, v_cols, vt_cols=()):
    n, d = x.shape
    n_k, n_v, n_v_tiled = (sum(b - a for a, b in cols) for cols in (k_cols, v_cols, vt_cols))
    tm = min(ROW_TILE, n)
    rope = tables is not None
    row = lambda i: (i, 0)
    in_specs = [pl.BlockSpec((tm, d), row), _resident((1, d)), _resident(mod.shape)]
    args = [x, g, mod]
    if rope:
        in_specs += [pl.BlockSpec((tm, LANES), row), pl.BlockSpec((tm, LANES), row)]
        args += list(tables)
    in_specs.append(_resident(w.shape))
    args.append(w)
    out_specs = [
        pl.BlockSpec((n_q, tm), lambda i: (0, i)),
        pl.BlockSpec((tm, n_k), row),
        pl.BlockSpec((n_v, tm), lambda i: (0, i)),
    ]
    out_shape = [
        jax.ShapeDtypeStruct((n_q, n), BF16),
        jax.ShapeDtypeStruct((n, n_k), BF16),
        jax.ShapeDtypeStruct((n_v, n), BF16),
    ]
    if n_v_tiled:
        out_specs.append(pl.BlockSpec((1, n_v_tiled, tm), lambda i: (i, 0, 0)))
        out_shape.append(jax.ShapeDtypeStruct((n // tm, n_v_tiled, tm), BF16))
    return pl.pallas_call(
        functools.partial(_proj_in_kernel, n_rope_q=n_rope_q, n_q=n_q, n_rope_k=n_rope_k, k_cols=k_cols,
                          v_cols=v_cols, vt_cols=vt_cols, rope=rope),
        grid=(n // tm,),
        in_specs=in_specs,
        out_specs=out_specs,
        out_shape=out_shape,
        compiler_params=_cparams("arbitrary"),
        name="proj_in",
    )(*args)


def _half_masked(qT_pair, e):
    z = jnp.zeros((HEAD_DIM, qT_pair.shape[1]), qT_pair.dtype)
    if e == 0:
        return jnp.concatenate([qT_pair[:HEAD_DIM], z], axis=0)
    return jnp.concatenate([z, qT_pair[HEAD_DIM:]], axis=0)


def _diff_kernel(*refs, lam_init, n_tiles):
    if n_tiles:
        qT_ref, k_ref, v3_ref, kc_ref, vcT_ref, lam_ref, subg_ref, o_ref = refs[:8]
    else:
        qT_ref, kc_ref, vcT_ref, lam_ref, subg_ref, o_ref = refs[:6]
    qm_ref, m_ref, alpha_ref, l_ref, acc_ref, s_ref, p_ref = refs[-7:]

    tq = qm_ref.shape[2]
    items = [(h, e, c) for h in range(DIFF_HEADS) for e in range(2) for c in range(0, tq, DIFF_QC)]
    n = len(items)
    s_slots = s_ref.shape[0]
    assert n % s_slots == 0 and s_slots > DIFF_DEPTH, (n, s_slots)
    sub = 8

    def keys_of(tile, h):
        lanes = slice(h * LANES, (h + 1) * LANES)
        if tile is None:
            return kc_ref[:, lanes]
        return k_ref[pl.ds(pl.multiple_of(tile * DIFF_TK, DIFF_TK), DIFF_TK), lanes]

    def vals_of(tile, h):
        rows = slice(h * LANES, (h + 1) * LANES)
        return vcT_ref[rows, :] if tile is None else v3_ref[tile, rows, :]

    def scores(i, tile):
        h, e, c = items[i]
        mp = 2 * h + e
        keys = keys_of(tile, h)
        nk = keys.shape[0]
        s = _dot(keys, qm_ref[mp, :, c:c + DIFF_QC])
        s_ref[i % s_slots, :nk, :] = s
        m_part = jnp.max(s.reshape(nk // sub, sub, DIFF_QC), axis=0)
        m_old = m_ref[mp, :, c:c + DIFF_QC]
        m_new = jnp.maximum(m_old, jnp.max(m_part, axis=0, keepdims=True))
        m_ref[mp, :, c:c + DIFF_QC] = m_new
        alpha_ref[mp, :, c:c + DIFF_QC] = jnp.exp2(m_old - m_new)

    def absorb(i, tile):
        h, e, c = items[i]
        mp = 2 * h + e
        cols = slice(c, c + DIFF_QC)
        vals = vals_of(tile, h)
        nk = vals.shape[1]
        m_new = m_ref[mp, :, cols]
        alpha = alpha_ref[mp, :, cols]
        for r in range(0, nk, DIFF_RC):
            p = jnp.exp2(s_ref[i % s_slots, r:r + DIFF_RC, :] - m_new)
            p_ref[i % 2, r:r + DIFF_RC, :] = p.astype(BF16)
        pv = _dot(jnp.concatenate([vals, jnp.ones((ONES_ROWS, nk), BF16)], axis=0), p_ref[i % 2, :nk, :])
        l_ref[mp, :, cols] = alpha * l_ref[mp, :, cols] + pv[LANES:LANES + 1]
        acc_ref[mp, :, cols] = alpha * acc_ref[mp, :, cols] + pv[:LANES]

    def sweep(tiles):
        work = [(i, tile) for tile in tiles for i in range(n)]
        for i, tile in work[:DIFF_DEPTH]:
            scores(i, tile)
        for j, (i, tile) in enumerate(work):
            if j + DIFF_DEPTH < len(work):
                scores(*work[j + DIFF_DEPTH])
            absorb(i, tile)

    for h in range(DIFF_HEADS):
        pair = qT_ref[h * LANES:(h + 1) * LANES, :]
        for e in range(2):
            qm_ref[2 * h + e] = _half_masked(pair, e)
    m_ref[...] = jnp.full(m_ref.shape, NEG_INF, F32)
    l_ref[...] = jnp.zeros(l_ref.shape, F32)
    acc_ref[...] = jnp.zeros(acc_ref.shape, F32)
    sweep([None])
    if n_tiles:
        def body(t, carry):
            sweep([DIFF_SWEEP * t + u for u in range(DIFF_SWEEP)])
            return carry
        lax.fori_loop(0, n_tiles // DIFF_SWEEP, body, 0)

    lam = lam_ref[...]
    lam_val = (jnp.exp(jnp.sum(lam[0:1] * lam[1:2], axis=1, keepdims=True))
               - jnp.exp(jnp.sum(lam[2:3] * lam[3:4], axis=1, keepdims=True)) + lam_init)
    subg = subg_ref[...]
    for h in range(DIFF_HEADS):
        a0 = acc_ref[2 * h] / l_ref[2 * h]
        a1 = acc_ref[2 * h + 1] / l_ref[2 * h + 1]
        a = a0 - lam_val * a1
        ms = jnp.mean(a * a, axis=0, keepdims=True)
        a = a * lax.rsqrt(ms + NORM_EPS) * subg * (1.0 - lam_init)
        o_ref[:, h * LANES:(h + 1) * LANES] = a.T.astype(o_ref.dtype)


def _diff_attention(qT, k, v3, kc, vcT, lam, subg, lam_init):
    n = qT.shape[1]
    nc = kc.shape[0]
    w = DIFF_WIDTH
    small = [_resident((4, HEAD_DIM)), _resident((2 * HEAD_DIM, 1))]
    if k is not None:
        tq, tk = DIFF_TQ, DIFF_TK
        n_tiles = v3.shape[0]
        assert v3.shape == (k.shape[0] // tk, w, tk) and n_tiles % DIFF_SWEEP == 0, (v3.shape, k.shape)
        in_specs = [
            pl.BlockSpec((w, tq), lambda i: (0, i)),
            _resident((k.shape[0], w)),
            _resident(v3.shape),
            pl.BlockSpec((nc, w), lambda i: (0, 0)),
            pl.BlockSpec((w, nc), lambda i: (0, 0)),
        ] + small
        args = (qT, k, v3, kc, vcT, lam, subg)
    else:
        tq, tk, n_tiles = n, 0, 0
        in_specs = [
            pl.BlockSpec((w, tq), lambda i: (0, 0)),
            pl.BlockSpec((nc, w), lambda i: (0, 0)),
            pl.BlockSpec((w, nc), lambda i: (0, 0)),
        ] + small
        args = (qT, kc, vcT, lam, subg)
    return pl.pallas_call(
        functools.partial(_diff_kernel, lam_init=lam_init, n_tiles=n_tiles),
        grid=(n // tq,),
        in_specs=in_specs,
        out_specs=pl.BlockSpec((tq, w), lambda i: (i, 0)),
        out_shape=jax.ShapeDtypeStruct((n, w), BF16),
        scratch_shapes=[
            pltpu.VMEM((2 * DIFF_HEADS, LANES, tq), BF16),
            pltpu.VMEM((2 * DIFF_HEADS, 1, tq), F32),
            pltpu.VMEM((2 * DIFF_HEADS, 1, tq), F32),
            pltpu.VMEM((2 * DIFF_HEADS, 1, tq), F32),
            pltpu.VMEM((2 * DIFF_HEADS, LANES, tq), F32),
            pltpu.VMEM((2 * DIFF_DEPTH, max(tk, nc), DIFF_QC), F32),
            pltpu.VMEM((2, max(tk, nc), DIFF_QC), BF16),
        ],
        compiler_params=_cparams("arbitrary"),
        name="diff_attn" if n_tiles else "diff_attn_ctx",
    )(*args)


def _na_bias_kernel(rpb_ref, o_ref):
    h = pl.program_id(0)
    n_dr, n_dc = 2 * NA_MAX_ROWS - 1, 2 * NA_COLS - 1
    kc = lax.broadcasted_iota(jnp.int32, (GRID_W, LANES), 0)
    lane = lax.broadcasted_iota(jnp.int32, (GRID_W, LANES), 1)
    c = lane % GRID_W
    upper = lane >= GRID_W
    dc = kc - c + (NA_COLS - 1)
    cs = jnp.clip(c - NA_COLS // 2, 0, GRID_W - NA_COLS)
    col_ok = (kc >= cs) & (kc < cs + NA_COLS)
    for i in range(o_ref.shape[1]):
        dr_idx = (i - 1, i - 2)
        val = jnp.full((GRID_W, LANES), NEG_INF, F32)
        for d in range(n_dc):
            lo, up = (rpb_ref[(h * n_dr + r) * n_dc + d] * LOG2E if 0 <= r < n_dr else NEG_INF for r in dr_idx)
            val = jnp.where(dc == d, jnp.where(upper, up, lo), val)
        o_ref[0, i] = jnp.where(col_ok, val, NEG_INF)


def _na_bias_blocks(rpb):
    n_heads = rpb.shape[0]
    n_blocks = 2 * NA_MAX_ROWS
    return pl.pallas_call(
        _na_bias_kernel,
        grid=(n_heads,),
        in_specs=[pl.BlockSpec(memory_space=pltpu.SMEM)],
        out_specs=pl.BlockSpec((1, n_blocks, GRID_W, LANES), lambda h: (h, 0, 0, 0)),
        out_shape=jax.ShapeDtypeStruct((n_heads, n_blocks, GRID_W, LANES), F32),
        compiler_params=_cparams("arbitrary"),
        name="na_bias",
    )(rpb.astype(F32).reshape(-1))


def _na_window_bias(bias_ref, e, t, grid_rows):
    lane = lax.broadcasted_iota(jnp.int32, (GRID_W, LANES), 1)
    r0 = t * NA_TILE_ROWS
    rows = []
    for rk in range(3 * NA_TILE_ROWS):
        kr = r0 - NA_TILE_ROWS + rk
        cols = []
        for p in range(NA_TILE_ROWS // 2):
            blk = bias_ref[e, rk - NA_TILE_ROWS - 2 * p + NA_MAX_ROWS]
            r = r0 + 2 * p + lane // GRID_W
            rs = jnp.clip(r - NA_MAX_ROWS // 2, 0, grid_rows - NA_MAX_ROWS)
            ok = (kr >= rs) & (kr < rs + NA_MAX_ROWS)
            cols.append(jnp.where(ok, blk, NEG_INF))
        rows.append(jnp.concatenate(cols, axis=1))
    return jnp.concatenate(rows, axis=0)


def _na_kernel(*refs, has_latent, grid_rows):
    if has_latent:
        qT_ref, km_ref, k0_ref, kp_ref, vm_ref, v0_ref, vp_ref, kc_ref, vcT_ref, bias_ref, o_ref = refs
    else:
        qT_ref, kc_ref, vcT_ref, o_ref = refs
    items = [(pp, e) for pp in range(NA_STEP_PAIRS) for e in range(2)]

    def scores(item):
        pp, e = item
        lanes = slice(pp * LANES, (pp + 1) * LANES)
        qm = _half_masked(qT_ref[lanes, :], e)
        s = _dot(kc_ref[:, lanes], qm)
        if has_latent:
            s_nb = jnp.concatenate([_dot(r[:, lanes], qm) for r in (km_ref, k0_ref, kp_ref)], axis=0)
            s = jnp.concatenate([s_nb + _na_window_bias(bias_ref, 2 * pp + e, pl.program_id(1), grid_rows), s], axis=0)
        return s

    def attend(item, s):
        pp, e = item
        rows = slice(pp * LANES + e * HEAD_DIM, pp * LANES + (e + 1) * HEAD_DIM)
        vT = vcT_ref[rows, :]
        if has_latent:
            vT = jnp.concatenate([vm_ref[rows, :], v0_ref[rows, :], vp_ref[rows, :], vT], axis=1)
        p = jnp.exp2(s - jnp.max(s, axis=0, keepdims=True))
        pv = _dot(jnp.concatenate([vT, jnp.ones((ONES_ROWS, vT.shape[1]), BF16)], axis=0), p.astype(BF16))
        return pv[:HEAD_DIM] / pv[HEAD_DIM:HEAD_DIM + 1]

    pending = [scores(item) for item in items[:NA_DEPTH]]
    outs = []
    for i, item in enumerate(items):
        if i + NA_DEPTH < len(items):
            pending.append(scores(items[i + NA_DEPTH]))
        outs.append(attend(item, pending.pop(0)))
    o_ref[...] = jnp.concatenate(outs, axis=0).T.astype(o_ref.dtype)


def _na_attention(qT_all, k_all, vT_all, kc_all, vcT_all, bias, *, has_latent):
    n = qT_all.shape[1]
    nc = kc_all.shape[0]
    w = NA_STEP_PAIRS * LANES
    g0 = DIFF_WIDTH // w
    n_steps = NA_WIDTH // w
    if has_latent:
        tq = NA_TQ
        nt = n // tq
        prev = lambda t: jnp.maximum(t - 1, 0)
        nxt = lambda t: jnp.minimum(t + 1, nt - 1)
        in_specs = [
            pl.BlockSpec((w, tq), lambda hp, t: (g0 + hp, t)),
            pl.BlockSpec((tq, w), lambda hp, t: (prev(t), g0 + hp)),
            pl.BlockSpec((tq, w), lambda hp, t: (t, g0 + hp)),
            pl.BlockSpec((tq, w), lambda hp, t: (nxt(t), g0 + hp)),
            pl.BlockSpec((w, tq), lambda hp, t: (hp, prev(t))),
            pl.BlockSpec((w, tq), lambda hp, t: (hp, t)),
            pl.BlockSpec((w, tq), lambda hp, t: (hp, nxt(t))),
            pl.BlockSpec((nc, w), lambda hp, t: (0, g0 + hp)),
            pl.BlockSpec((w, nc), lambda hp, t: (g0 + hp, 0)),
            pl.BlockSpec((2 * NA_STEP_PAIRS,) + bias.shape[1:], lambda hp, t: (hp, 0, 0, 0)),
        ]
        args = (qT_all, k_all, k_all, k_all, vT_all, vT_all, vT_all, kc_all, vcT_all, bias)
    else:
        tq = n
        nt = 1
        in_specs = [
            pl.BlockSpec((w, tq), lambda hp, t: (g0 + hp, 0)),
            pl.BlockSpec((nc, w), lambda hp, t: (0, g0 + hp)),
            pl.BlockSpec((w, nc), lambda hp, t: (g0 + hp, 0)),
        ]
        args = (qT_all, kc_all, vcT_all)
    return pl.pallas_call(
        functools.partial(_na_kernel, has_latent=has_latent, grid_rows=n // GRID_W),
        grid=(n_steps, nt),
        in_specs=in_specs,
        out_specs=pl.BlockSpec((tq, w), lambda hp, t: (t, hp)),
        out_shape=jax.ShapeDtypeStruct((n, NA_WIDTH), BF16),
        compiler_params=_cparams("arbitrary", "arbitrary"),
        name="na_attn" if has_latent else "na_attn_ctx",
    )(*args)


def _swa_kernel(*refs, has_latent):
    tq = SWA_TQ
    if has_latent:
        qT_ref, km_ref, k0_ref, kp_ref, vm_ref, v0_ref, vp_ref, kc_ref, vcT_ref, sink_ref, o_ref = refs
        step = pl.program_id(0)
        last = pl.num_programs(0) - 1
        ki = lax.broadcasted_iota(jnp.int32, (tq, tq), 0)
        qi = lax.broadcasted_iota(jnp.int32, (tq, tq), 1)
        lower = jnp.concatenate([ki >= qi] * SWA_ITEM_HEADS, axis=1)
        upper = jnp.concatenate([ki <= qi] * SWA_ITEM_HEADS, axis=1)
    else:
        qT_ref, kc_ref, vcT_ref, sink_ref, o_ref = refs
    n_sub = qT_ref.shape[1] // tq
    zeros = jnp.zeros((HEAD_DIM, tq), BF16)
    width = SWA_ITEM_HEADS * tq
    items = [(u, g, j0) for u in range(n_sub) for g in range(GQA_KV_HEADS)
             for j0 in range(0, GQA_GROUP, SWA_ITEM_HEADS)]

    def window(u, own_ref, before_ref, after_ref, axis):
        def own(v):
            return (slice(v * tq, (v + 1) * tq),) if axis == 0 else (slice(None), slice(v * tq, (v + 1) * tq))
        prev = (before_ref, ()) if u == 0 else (own_ref, own(u - 1))
        nxt = (after_ref, ()) if u == n_sub - 1 else (own_ref, own(u + 1))
        return prev, (own_ref, own(u)), nxt

    def scores(item):
        u, g, j0 = item
        lanes = slice((g // 2) * LANES, (g // 2 + 1) * LANES)
        q_cols = []
        for jh in range(j0, j0 + SWA_ITEM_HEADS):
            hq = g * GQA_GROUP + jh
            qh = qT_ref[hq * HEAD_DIM:(hq + 1) * HEAD_DIM, u * tq:(u + 1) * tq]
            q_cols.append(jnp.concatenate([qh, zeros] if g % 2 == 0 else [zeros, qh], axis=0))
        qm = jnp.concatenate(q_cols, axis=1)
        s = _dot(kc_ref[:, lanes], qm)
        if has_latent:
            (pr, pi), (cr, ci), (nr, ni) = window(u, k0_ref, km_ref, kp_ref, 0)
            band_prev = lower if u > 0 else lower & (step > 0)
            band_next = upper if u < n_sub - 1 else upper & (step < last)
            s_prev = jnp.where(band_prev, _dot(pr[pi + (lanes,) if pi else (slice(None), lanes)], qm), NEG_INF)
            s_next = jnp.where(band_next, _dot(nr[ni + (lanes,) if ni else (slice(None), lanes)], qm), NEG_INF)
            s = jnp.concatenate([s, s_prev, _dot(cr[ci + (lanes,)], qm), s_next], axis=0)
        return s

    def attend(item, s):
        u, g, j0 = item
        rows = slice(g * HEAD_DIM, (g + 1) * HEAD_DIM)
        vT = vcT_ref[rows, :]
        if has_latent:
            (pr, pi), (cr, ci), (nr, ni) = window(u, v0_ref, vm_ref, vp_ref, 1)
            pick = lambda r, i: r[rows, i[1]] if i else r[rows, :]
            vT = jnp.concatenate([vT, pick(pr, pi), pick(cr, ci), pick(nr, ni)], axis=1)
        sink = sink_ref[g][:, j0 * tq:j0 * tq + width] * LOG2E
        m = jnp.maximum(jnp.max(s, axis=0, keepdims=True), sink)
        p = jnp.exp2(s - m)
        pv = _dot(jnp.concatenate([vT, jnp.ones((ONES_ROWS, vT.shape[1]), BF16)], axis=0), p.astype(BF16))
        o = pv[:HEAD_DIM] / (pv[HEAD_DIM:HEAD_DIM + 1] + jnp.exp2(sink - m))
        return [o[:, jh * tq:(jh + 1) * tq] for jh in range(SWA_ITEM_HEADS)]

    pending = [scores(item) for item in items[:SWA_DEPTH]]
    outs = []
    for i, item in enumerate(items):
        if i + SWA_DEPTH < len(items):
            pending.append(scores(items[i + SWA_DEPTH]))
        outs += attend(item, pending.pop(0))
    per_block = len(outs) // n_sub
    for u in range(n_sub):
        o_ref[u * tq:(u + 1) * tq, :] = jnp.concatenate(outs[u * per_block:(u + 1) * per_block], axis=0).T.astype(o_ref.dtype)


def _swa_attention(qT, k, vT, kc, vcT, sinks, *, has_latent):
    n = qT.shape[1]
    nc = kc.shape[0]
    tq = SWA_TQ
    ts = SWA_STEP_BLOCKS * tq
    nb = n // tq
    sink_rows = jnp.repeat(sinks.astype(F32).reshape(GQA_KV_HEADS, 1, GQA_GROUP), tq, axis=2)
    ctx_specs = [_resident((nc, C_KV_COLS)), _resident((C_KV_COLS, nc)), _resident(sink_rows.shape)]
    if has_latent:
        prev = lambda s: jnp.maximum(SWA_STEP_BLOCKS * s - 1, 0)
        nxt = lambda s: jnp.minimum(SWA_STEP_BLOCKS * (s + 1), nb - 1)
        in_specs = [
            pl.BlockSpec((C_Q_COLS, ts), lambda s: (0, s)),
            pl.BlockSpec((tq, C_KV_COLS), lambda s: (prev(s), 0)),
            pl.BlockSpec((ts, C_KV_COLS), lambda s: (s, 0)),
            pl.BlockSpec((tq, C_KV_COLS), lambda s: (nxt(s), 0)),
            pl.BlockSpec((C_KV_COLS, tq), lambda s: (0, prev(s))),
            pl.BlockSpec((C_KV_COLS, ts), lambda s: (0, s)),
            pl.BlockSpec((C_KV_COLS, tq), lambda s: (0, nxt(s))),
        ] + ctx_specs
        args = (qT, k, k, k, vT, vT, vT, kc, vcT, sink_rows)
    else:
        in_specs = [pl.BlockSpec((C_Q_COLS, ts), lambda s: (0, s))] + ctx_specs
        args = (qT, kc, vcT, sink_rows)
    return pl.pallas_call(
        functools.partial(_swa_kernel, has_latent=has_latent),
        grid=(n // ts,),
        in_specs=in_specs,
        out_specs=pl.BlockSpec((ts, C_Q_COLS), lambda s: (s, 0)),
        out_shape=jax.ShapeDtypeStruct((n, C_Q_COLS), BF16),
        compiler_params=_cparams("arbitrary"),
        name="swa_attn" if has_latent else "swa_attn_ctx",
    )(*args)


def _out_ffn_kernel(*refs, n_attn, final):
    x_ref = refs[0]
    attn_refs = refs[1:1 + n_attn]
    wout_ref, mod_ref, g_ref, w1_ref, w3_ref, w2_ref = refs[1 + n_attn:7 + n_attn]
    rest = refs[7 + n_attn:]
    if final:
        fg_ref, o_ref = rest
    else:
        (o_ref,) = rest
    proj = None
    r0 = 0
    for a_ref in attn_refs:
        wdt = a_ref.shape[1]
        part = _dot(a_ref[...], wout_ref[r0:r0 + wdt, :])
        proj = part if proj is None else proj + part
        r0 += wdt
    x1 = x_ref[...] + mod_ref[2:3, :] * proj
    h = _norm_mod(x1, g_ref[...], mod_ref[3:4, :], mod_ref[4:5, :]).astype(BF16)
    hidden = w1_ref.shape[1]
    chunk = FFN_CHUNK
    ffn = None
    for c0 in range(0, hidden, chunk):
        a = _dot(h, w1_ref[:, c0:c0 + chunk])
        b = _dot(h, w3_ref[:, c0:c0 + chunk])
        gated = (a * jax.nn.sigmoid(a) * b).astype(BF16)
        part = _dot(gated, w2_ref[c0:c0 + chunk, :])
        ffn = part if ffn is None else ffn + part
    x2 = x1 + mod_ref[5:6, :] * ffn
    if final:
        ms = jnp.mean(x2 * x2, axis=-1, keepdims=True)
        x2 = x2 * lax.rsqrt(ms + NORM_EPS) * fg_ref[...]
    o_ref[...] = x2


def _out_ffn(x, attn, w_out, mod, g, w1, w3, w2, final_g=None):
    n, d = x.shape
    tm = min(ROW_TILE, n)
    row = lambda i: (i, 0)
    final = final_g is not None
    in_specs = [pl.BlockSpec((tm, d), row)]
    in_specs += [pl.BlockSpec((tm, a.shape[1]), row) for a in attn]
    in_specs += [_resident(w_out.shape), _resident(mod.shape), _resident((1, d)),
                 _resident(w1.shape), _resident(w3.shape), _resident(w2.shape)]
    args = [x, *attn, w_out, mod, g, w1, w3, w2]
    if final:
        in_specs.append(_resident((1, d)))
        args.append(final_g)
    return pl.pallas_call(
        functools.partial(_out_ffn_kernel, n_attn=len(attn), final=final),
        grid=(n // tm,),
        in_specs=in_specs,
        out_specs=pl.BlockSpec((tm, d), row),
        out_shape=jax.ShapeDtypeStruct((n, d), F32),
        compiler_params=_cparams("arbitrary"),
        name="out_ffn",
    )(*args)


def _rope_tables(n):
    t = jnp.arange(n)
    row = (t // GRID_W).astype(F32)
    col = (t % GRID_W).astype(F32)
    quarter = HEAD_DIM // 4
    inv = ROPE_BASE ** (-jnp.arange(quarter, dtype=F32) / quarter)
    ar = row[:, None] * inv
    ac = col[:, None] * inv
    ang = jnp.concatenate([ar, ar, ac, ac], axis=-1)
    sign = jnp.tile(jnp.concatenate([-jnp.ones(quarter, F32), jnp.ones(quarter, F32)]), 2)
    cos = jnp.tile(jnp.cos(ang), (1, LANES // HEAD_DIM))
    sin = jnp.tile(jnp.sin(ang) * sign, (1, LANES // HEAD_DIM))
    return cos, sin


def _trunk(x, ctx, mods, norm_g, w_in_ab, w_out_ab, diff_lambda, diff_sub_g, na_rpb, w_in_c, w_out_c,
           attn_sinks, ffn_w1, ffn_w3, ffn_w2, final_g):
    depth = mods.shape[0]
    s_len = x.shape[0]
    tables = _rope_tables(s_len)
    for layer in range(depth):
        ctx_out = layer < depth - 1
        mod_x, mod_c = mods[layer, 0], mods[layer, 1]
        g0 = norm_g[layer, 0].reshape(1, -1)
        g1 = norm_g[layer, 1].reshape(1, -1)
        if layer % 2 == 0:
            e = layer // 2
            lam_init = 0.8 - 0.6 * math.exp(-0.3 * layer)
            w = w_in_ab[e].astype(BF16)
            q_cols = DIFF_WIDTH + NA_WIDTH
            a_k, a_v, b_k, b_v = ((q_cols + i * DIFF_WIDTH, q_cols + (i + 1) * DIFF_WIDTH) for i in range(4))
            dims = dict(n_rope_q=DIFF_WIDTH, n_q=q_cols, n_rope_k=DIFF_WIDTH, k_cols=(a_k, b_k))
            qT, k, vT, v3 = _proj_in(x, g0, mod_x, w, tables, v_cols=(b_v,), vt_cols=(a_v,), **dims)
            qcT, kc, vcT = _proj_in(ctx, g0, mod_c, w, None, v_cols=(a_v, b_v), **dims)
            lam = diff_lambda[e].astype(F32)
            subg = diff_sub_g[e].astype(F32).reshape(-1, 1)
            bias = _na_bias_blocks(na_rpb[e])
            a_x = _diff_attention(qT, k, v3, kc, vcT, lam, subg, lam_init)
            b_x = _na_attention(qT, k, vT, kc, vcT, bias, has_latent=True)
            attn_x = [a_x, b_x]
            if ctx_out:
                a_c = _diff_attention(qcT, None, None, kc, vcT, lam, subg, lam_init)
                b_c = _na_attention(qcT, None, None, kc, vcT, None, has_latent=False)
                attn_c = [a_c, b_c]
            w_out = w_out_ab[e].astype(BF16)
        else:
            o = layer // 2
            w = w_in_c[o].astype(BF16)
            dims = dict(n_rope_q=C_Q_COLS, n_q=C_Q_COLS, n_rope_k=C_KV_COLS,
                        k_cols=((C_Q_COLS, C_Q_COLS + C_KV_COLS),),
                        v_cols=((C_Q_COLS + C_KV_COLS, C_Q_COLS + 2 * C_KV_COLS),))
            qT, k, vT = _proj_in(x, g0, mod_x, w, tables, **dims)
            qcT, kc, vcT = _proj_in(ctx, g0, mod_c, w, None, **dims)
            attn_x = [_swa_attention(qT, k, vT, kc, vcT, attn_sinks[o], has_latent=True)]
            if ctx_out:
                attn_c = [_swa_attention(qcT, None, None, kc, vcT, attn_sinks[o], has_latent=False)]
            w_out = w_out_c[o].astype(BF16)
        w1, w3, w2 = ffn_w1[layer].astype(BF16), ffn_w3[layer].astype(BF16), ffn_w2[layer].astype(BF16)
        fg = final_g.reshape(1, -1) if layer == depth - 1 else None
        x = _out_ffn(x, attn_x, w_out, mod_x, g1, w1, w3, w2, fg)
        if ctx_out:
            ctx = _out_ffn(ctx, attn_c, w_out, mod_c, g1, w1, w3, w2)
    return x


def kernel(x, c, ctx, c_ctx, ada_w, ada_b, norm_g, w_in_ab, w_out_ab, diff_lambda, diff_sub_g, na_rpb, w_in_c,
           w_out_c, attn_sinks, ffn_w1, ffn_w3, ffn_w2, final_g):
    batch, _, d = x.shape
    depth = ada_w.shape[0]
    outs = []
    for b in range(batch):
        cc = jnp.zeros((8, d), F32).at[0].set(c[b].astype(F32)).at[1].set(c_ctx.astype(F32))
        mods = _adaln(cc, ada_w, ada_b)[:, :2].reshape(depth, 2, 6, d)
        outs.append(_trunk(x[b], ctx[b], mods, norm_g, w_in_ab, w_out_ab, diff_lambda, diff_sub_g, na_rpb,
                           w_in_c, w_out_c, attn_sinks, ffn_w1, ffn_w3, ffn_w2, final_g))
    return jnp.stack(outs).astype(x.dtype)
```
